```python
import math
import jax, jax.numpy as jnp
from jax import lax
import numpy as np

D_MODEL = 2048
BATCH = 8
SEQ = 2048
DEPTH = 1

ATT_HEADS = 16
ATT_KV_HEADS = 4
ATT_HEAD_DIM = 64
ATT_GROUP = ATT_HEADS // ATT_KV_HEADS
WINDOW = 128
ATT_BLOCK = 128
M_HEADS = 4
M_QK_DIM = 128
M_V_DIM = 256
M_CHUNK = 64
CONV_WIDTH = 4
D_FF = 5632
EPS = 1e-6

ATT_Q = ATT_HEADS * ATT_HEAD_DIM
ATT_KV = ATT_KV_HEADS * ATT_HEAD_DIM
M_QK = M_HEADS * M_QK_DIM
M_V = M_HEADS * M_V_DIM
IN_SIZES = (ATT_Q, ATT_KV, ATT_KV, M_QK, M_QK, M_V, M_V, M_HEADS, M_HEADS, D_MODEL, D_MODEL)
D_IN = ATT_Q + 2 * ATT_KV + 2 * M_QK + 2 * M_V + 2 * M_HEADS + 2 * D_MODEL

kernel_name = "hybrid_swa_mlstm_macaron_block"


def rmsnorm(x, g):
    xf = x.astype(jnp.float32)
    y = xf * lax.rsqrt(jnp.mean(xf * xf, axis=-1, keepdims=True) + EPS)
    return (y * g.astype(jnp.float32)).astype(x.dtype)


def swiglu(x, wg, wu, wd):
    return (jax.nn.silu(x @ wg) * (x @ wu)) @ wd


def alibi_slopes(n):
    return jnp.exp2(-8.0 * jnp.arange(1, n + 1, dtype=jnp.float32) / n)


def split_columns(z):
    parts = []
    off = 0
    for size in IN_SIZES:
        parts.append(z[..., off:off + size])
        off += size
    return parts


def causal_depthwise_conv(x, w, b):
    S = x.shape[1]
    xp = jnp.pad(x, ((0, 0), (CONV_WIDTH - 1, 0), (0, 0)))
    y = xp[:, 0:S] * w[0]
    for j in range(1, CONV_WIDTH):
        y = y + xp[:, j:j + S] * w[j]
    return y + b


def sliding_window_attention(q, k, v, sinks):
    B, S = q.shape[0], q.shape[1]
    nb = S // ATT_BLOCK
    qb = q.reshape(B, nb, ATT_BLOCK, ATT_KV_HEADS, ATT_GROUP, ATT_HEAD_DIM)
    kb = k.reshape(B, nb, ATT_BLOCK, ATT_KV_HEADS, ATT_HEAD_DIM)
    vb = v.reshape(B, nb, ATT_BLOCK, ATT_KV_HEADS, ATT_HEAD_DIM)
    zpad = jnp.zeros_like(kb[:, :1])
    kk = jnp.concatenate([jnp.concatenate([zpad, kb[:, :-1]], axis=1), kb], axis=2)
    vv = jnp.concatenate([jnp.concatenate([zpad, vb[:, :-1]], axis=1), vb], axis=2)
    scale = 1.0 / math.sqrt(ATT_HEAD_DIM)
    s = jnp.einsum('bnqhgd,bnkhd->bnhgqk', qb, kk).astype(jnp.float32) * scale
    qi = jnp.arange(ATT_BLOCK)[:, None]
    kj = jnp.arange(2 * ATT_BLOCK)[None, :]
    dist = qi - kj + ATT_BLOCK
    blk = jnp.arange(nb)[:, None, None]
    valid = (dist >= 0) & (dist < WINDOW) & (blk * ATT_BLOCK + kj - ATT_BLOCK >= 0)
    slopes = alibi_slopes(ATT_HEADS).reshape(ATT_KV_HEADS, ATT_GROUP)
    s = s - slopes[None, None, :, :, None, None] * dist.astype(jnp.float32)[None, None, None, None]
    s = jnp.where(valid[None, :, None, None], s, -jnp.inf)
    sink = sinks.astype(jnp.float32).reshape(ATT_KV_HEADS, ATT_GROUP)[None, None, :, :, None, None]
    m = jnp.maximum(jnp.max(s, axis=-1, keepdims=True), sink)
    p = jnp.exp(s - m)
    p = p / (jnp.sum(p, axis=-1, keepdims=True) + jnp.exp(sink - m))
    o = jnp.einsum('bnhgqk,bnkhd->bnqhgd', p.astype(vv.dtype), vv)
    return o.reshape(B, S, ATT_Q)


def mlstm_chunkwise(q, k, v, i_pre, f_pre):
    B, S = q.shape[0], q.shape[1]
    nc = S // M_CHUNK
    L = M_CHUNK
    f32 = jnp.float32

    def chunks(t, d):
        return t.astype(f32).reshape(B, nc, L, M_HEADS, d).transpose(0, 3, 1, 2, 4)

    qc = chunks(q, M_QK_DIM)
    kc = chunks(k, M_QK_DIM) * (1.0 / math.sqrt(M_QK_DIM))
    vc = chunks(v, M_V_DIM)
    logi = i_pre.astype(f32).reshape(B, nc, L, M_HEADS).transpose(0, 3, 1, 2)
    logf = jax.nn.log_sigmoid(f_pre.astype(f32)).reshape(B, nc, L, M_HEADS).transpose(0, 3, 1, 2)
    b = jnp.cumsum(logf, axis=-1)
    b_last = b[..., -1]
    g = b_last[..., None] - b + logi

    def step(carry, xs):
        C, n, m = carry
        bl, gs, ks, vs = xs
        m_new = jnp.maximum(bl + m, jnp.max(gs, axis=-1))
        decay = jnp.exp(bl + m - m_new)
        w = jnp.exp(gs - m_new[..., None])
        C_new = decay[..., None, None] * C + jnp.einsum('bhl,bhlk,bhlv->bhkv', w, ks, vs)
        n_new = decay[..., None] * n + jnp.einsum('bhl,bhlk->bhk', w, ks)
        return (C_new, n_new, m_new), (C, n, m)

    init = (jnp.zeros((B, M_HEADS, M_QK_DIM, M_V_DIM), f32),
            jnp.zeros((B, M_HEADS, M_QK_DIM), f32),
            jnp.zeros((B, M_HEADS), f32))
    xs = (jnp.moveaxis(b_last, 2, 0), jnp.moveaxis(g, 2, 0), jnp.moveaxis(kc, 2, 0), jnp.moveaxis(vc, 2, 0))
    _, (C_in, n_in, m_in) = lax.scan(step, init, xs)
    C_in = jnp.moveaxis(C_in, 0, 2)
    n_in = jnp.moveaxis(n_in, 0, 2)
    m_in = jnp.moveaxis(m_in, 0, 2)

    a = b + m_in[..., None]
    D = b[..., :, None] - b[..., None, :] + logi[..., None, :]
    causal = jnp.tril(jnp.ones((L, L), dtype=bool))
    D = jnp.where(causal, D, -jnp.inf)
    m_t = jnp.maximum(a, jnp.max(D, axis=-1))
    inter_w = jnp.exp(a - m_t)
    P = jnp.einsum('bhcld,bhcsd->bhcls', qc, kc) * jnp.exp(D - m_t[..., None])
    num = jnp.einsum('bhcls,bhcsv->bhclv', P, vc) + inter_w[..., None] * jnp.einsum('bhcld,bhcdv->bhclv', qc, C_in)
    den = jnp.sum(P, axis=-1) + inter_w * jnp.einsum('bhcld,bhcd->bhcl', qc, n_in)
    h = num / jnp.maximum(jnp.abs(den), jnp.exp(-m_t))[..., None]
    return h.transpose(0, 2, 3, 1, 4).reshape(B, S, M_HEADS, M_V_DIM).astype(q.dtype)


def hybrid_mixer(u, w_in, conv_w, conv_b, attn_sinks, m_igate_b, m_fgate_b, m_head_g, w_attn_up, w_mlstm_up, w_out):
    B, S = u.shape[0], u.shape[1]
    aq, ak, av, mq, mk, mv, mo, mi, mf, ga, gm = split_columns(u @ w_in)
    att = sliding_window_attention(aq.reshape(B, S, ATT_HEADS, ATT_HEAD_DIM),
                                   ak.reshape(B, S, ATT_KV_HEADS, ATT_HEAD_DIM),
                                   av.reshape(B, S, ATT_KV_HEADS, ATT_HEAD_DIM), attn_sinks)
    qk = jax.nn.silu(causal_depthwise_conv(jnp.concatenate([mq, mk], axis=-1), conv_w, conv_b))
    h = mlstm_chunkwise(qk[..., :M_QK].reshape(B, S, M_HEADS, M_QK_DIM),
                        qk[..., M_QK:].reshape(B, S, M_HEADS, M_QK_DIM),
                        mv.reshape(B, S, M_HEADS, M_V_DIM), mi + m_igate_b, mf + m_fgate_b)
    h = rmsnorm(h, m_head_g.reshape(M_HEADS, M_V_DIM)).reshape(B, S, M_V)
    h = jax.nn.sigmoid(mo) * h
    y = jax.nn.sigmoid(ga) * (att @ w_attn_up) + jax.nn.sigmoid(gm) * (h @ w_mlstm_up)
    return y @ w_out


def setup_inputs(seed: int = 0) -> dict:
    key = jax.random.key(seed)
    ks = jax.random.split(key, 24)
    Ld = DEPTH

    def nrm(k, shape, scale):
        return jax.random.normal(k, shape, jnp.float32) * scale

    def gain(k, n):
        return 1.0 + 0.02 * jax.random.normal(k, (Ld, n), jnp.float32)

    dsc = D_MODEL ** -0.5
    fsc = D_FF ** -0.5
    return {
        "x": nrm(ks[0], (BATCH, SEQ, D_MODEL), 1.0),
        "ffn1_pre_g": gain(ks[1], D_MODEL),
        "ffn1_wg": nrm(ks[2], (Ld, D_MODEL, D_FF), dsc),
        "ffn1_wu": nrm(ks[3], (Ld, D_MODEL, D_FF), dsc),
        "ffn1_wd": nrm(ks[4], (Ld, D_FF, D_MODEL), fsc),
        "ffn1_post_g": gain(ks[5], D_MODEL),
        "mix_pre_g": gain(ks[6], D_MODEL),
        "w_in": nrm(ks[7], (Ld, D_MODEL, D_IN), dsc),
        "conv_w": nrm(ks[8], (Ld, CONV_WIDTH, 2 * M_QK), CONV_WIDTH ** -0.5),
        "conv_b": nrm(ks[9], (Ld, 2 * M_QK), 0.02),
        "attn_sinks": nrm(ks[10], (Ld, ATT_HEADS), 0.5),
        "m_igate_b": nrm(ks[11], (Ld, M_HEADS), 0.1),
        "m_fgate_b": jnp.linspace(3.0, 6.0, M_HEADS, dtype=jnp.float32)[None, :] + nrm(ks[12], (Ld, M_HEADS), 0.1),
        "m_head_g": gain(ks[13], M_V),
        "w_attn_up": nrm(ks[14], (Ld, ATT_Q, D_MODEL), ATT_Q ** -0.5),
        "w_mlstm_up": nrm(ks[15], (Ld, M_V, D_MODEL), M_V ** -0.5),
        "w_out": nrm(ks[16], (Ld, D_MODEL, D_MODEL), dsc),
        "mix_post_g": gain(ks[17], D_MODEL),
        "ffn2_pre_g": gain(ks[18], D_MODEL),
        "ffn2_wg": nrm(ks[19], (Ld, D_MODEL, D_FF), dsc),
        "ffn2_wu": nrm(ks[20], (Ld, D_MODEL, D_FF), dsc),
        "ffn2_wd": nrm(ks[21], (Ld, D_FF, D_MODEL), fsc),
        "ffn2_post_g": gain(ks[22], D_MODEL),
    }


def reference(x, ffn1_pre_g, ffn1_wg, ffn1_wu, ffn1_wd, ffn1_post_g, mix_pre_g, w_in, conv_w, conv_b,
              attn_sinks, m_igate_b, m_fgate_b, m_head_g, w_attn_up, w_mlstm_up, w_out, mix_post_g,
              ffn2_pre_g, ffn2_wg, ffn2_wu, ffn2_wd, ffn2_post_g):
    h = x
    for l in range(DEPTH):
        f = swiglu(rmsnorm(h, ffn1_pre_g[l]), ffn1_wg[l], ffn1_wu[l], ffn1_wd[l])
        h = h + 0.5 * rmsnorm(f, ffn1_post_g[l])
        y = hybrid_mixer(rmsnorm(h, mix_pre_g[l]), w_in[l], conv_w[l], conv_b[l], attn_sinks[l],
                         m_igate_b[l], m_fgate_b[l], m_head_g[l], w_attn_up[l], w_mlstm_up[l], w_out[l])
        h = h + rmsnorm(y, mix_post_g[l])
        f = swiglu(rmsnorm(h, ffn2_pre_g[l]), ffn2_wg[l], ffn2_wu[l], ffn2_wd[l])
        h = h + 0.5 * rmsnorm(f, ffn2_post_g[l])
    return h
```

```python
import functools
import math

import jax
import jax.numpy as jnp
from jax import lax
from jax.experimental import pallas as pl
from jax.experimental.pallas import tpu as pltpu

F32 = jnp.float32
BF16 = jnp.bfloat16

D_MODEL = 2048
ATT_HEADS = 16
ATT_KV_HEADS = 4
ATT_HEAD_DIM = 64
ATT_GROUP = ATT_HEADS // ATT_KV_HEADS
WINDOW = 128
ATT_BLOCK = 128
M_HEADS = 4
M_QK_DIM = 128
M_V_DIM = 256
CONV_WIDTH = 4
D_FF = 5632
EPS = 1e-6

ATT_Q = ATT_HEADS * ATT_HEAD_DIM
ATT_KV = ATT_KV_HEADS * ATT_HEAD_DIM
M_QK = M_HEADS * M_QK_DIM
M_V = M_HEADS * M_V_DIM

LANES = 128
GATE_PAD = LANES
ZB_COLS = ATT_Q + M_V + 2 * ATT_KV
ZF_COLS = 2 * M_QK + M_V + GATE_PAD

VMEM_LIMIT = 56 * 1024 * 1024

FFN_TM = 512
FFN_TF = 512
PROJ_TM = 512
PROJ_TN = 512
ATT_TQ = 512
M_CHUNK = 256
MERGE_TM = 512
MERGE_TN = 512


def _rms(x, g):
    return x * lax.rsqrt(jnp.mean(x * x, axis=-1, keepdims=True) + EPS) * g


def _log_sigmoid(x):
    return jnp.minimum(x, 0.0) - jnp.log1p(jnp.exp(-jnp.abs(x)))


def _ffn_kernel(x_ref, pre_g_ref, wg_ref, wu_ref, wd_ref, post_g_ref, o_ref, xn_ref, acc_ref):
    f = pl.program_id(1)

    @pl.when(f == 0)
    def _():
        xn_ref[...] = _rms(x_ref[...], pre_g_ref[...]).astype(BF16)

    xn = xn_ref[...]
    g = jnp.dot(xn, wg_ref[...], preferred_element_type=F32)
    u = jnp.dot(xn, wu_ref[...], preferred_element_type=F32)
    hid = (g * jax.nn.sigmoid(g) * u).astype(BF16)
    d = jnp.dot(hid, wd_ref[...], preferred_element_type=F32)

    @pl.when(f == 0)
    def _():
        acc_ref[...] = d

    @pl.when(f > 0)
    def _():
        acc_ref[...] += d

    @pl.when(f == pl.num_programs(1) - 1)
    def _():
        o_ref[...] = x_ref[...] + 0.5 * _rms(acc_ref[...], post_g_ref[...])


def _ffn(h, pre_g, wg, wu, wd, post_g):
    n = h.shape[0]
    tm, tf = FFN_TM, FFN_TF
    return pl.pallas_call(
        _ffn_kernel,
        name="ffn",
        grid=(n // tm, D_FF // tf),
        in_specs=[
            pl.BlockSpec((tm, D_MODEL), lambda i, f: (i, 0)),
            pl.BlockSpec((1, D_MODEL), lambda i, f: (0, 0)),
            pl.BlockSpec((D_MODEL, tf), lambda i, f: (0, f)),
            pl.BlockSpec((D_MODEL, tf), lambda i, f: (0, f)),
            pl.BlockSpec((tf, D_MODEL), lambda i, f: (f, 0)),
            pl.BlockSpec((1, D_MODEL), lambda i, f: (0, 0)),
        ],
        out_specs=pl.BlockSpec((tm, D_MODEL), lambda i, f: (i, 0)),
        out_shape=jax.ShapeDtypeStruct((n, D_MODEL), F32),
        scratch_shapes=[pltpu.VMEM((tm, D_MODEL), BF16), pltpu.VMEM((tm, D_MODEL), F32)],
        compiler_params=pltpu.CompilerParams(
            dimension_semantics=("parallel", "arbitrary"), vmem_limit_bytes=VMEM_LIMIT),
    )(h, pre_g, wg, wu, wd, post_g)


def _inproj_kernel(h_ref, g_ref, w_ref, zb_ref, zf_ref, xn_ref):
    xn_ref[...] = _rms(h_ref[...], g_ref[...]).astype(BF16)
    xn = xn_ref[...]
    for c0 in range(0, ZB_COLS, PROJ_TN):
        zb_ref[:, c0:c0 + PROJ_TN] = jnp.dot(
            xn, w_ref[:, c0:c0 + PROJ_TN], preferred_element_type=F32).astype(BF16)
    for c0 in range(0, ZF_COLS, PROJ_TN):
        cw = min(PROJ_TN, ZF_COLS - c0)
        zf_ref[:, c0:c0 + cw] = jnp.dot(
            xn, w_ref[:, ZB_COLS + c0:ZB_COLS + c0 + cw], preferred_element_type=F32)


def _inproj(h, g, w_cat):
    n = h.shape[0]
    tm = PROJ_TM
    return pl.pallas_call(
        _inproj_kernel,
        name="inproj",
        grid=(n // tm,),
        in_specs=[
            pl.BlockSpec((tm, D_MODEL), lambda i: (i, 0)),
            pl.BlockSpec((1, D_MODEL), lambda i: (0, 0)),
            pl.BlockSpec((D_MODEL, ZB_COLS + ZF_COLS), lambda i: (0, 0),
                         pipeline_mode=pl.Buffered(1)),
        ],
        out_specs=[
            pl.BlockSpec((tm, ZB_COLS), lambda i: (i, 0)),
            pl.BlockSpec((tm, ZF_COLS), lambda i: (i, 0)),
        ],
        out_shape=[jax.ShapeDtypeStruct((n, ZB_COLS), BF16),
                   jax.ShapeDtypeStruct((n, ZF_COLS), F32)],
        scratch_shapes=[pltpu.VMEM((tm, D_MODEL), BF16)],
        compiler_params=pltpu.CompilerParams(
            dimension_semantics=("parallel",), vmem_limit_bytes=VMEM_LIMIT),
    )(h, g, w_cat)


def _attn_kernel(scal_ref, q_ref, kp_ref, kc_ref, vp_ref, vc_ref, o_ref):
    t = pl.program_id(1)
    blk = ATT_BLOCK
    half = ATT_HEAD_DIM
    scale = 1.0 / math.sqrt(ATT_HEAD_DIM)

    kk = jnp.concatenate([kp_ref[...], kc_ref[...]], axis=0).astype(F32)
    vv = jnp.concatenate([vp_ref[...], vc_ref[...]], axis=0).astype(F32)
    rows = kk.shape[0]
    lane = lax.broadcasted_iota(jnp.int32, (rows, LANES), 1)
    low = lane < half

    def split_pair(x):
        xr = pltpu.roll(x, half, axis=1)
        zero = jnp.zeros_like(x)
        first = (jnp.where(low, x, zero).astype(BF16), jnp.where(low, zero, xr).astype(BF16))
        second = (jnp.where(low, xr, zero).astype(BF16), jnp.where(low, zero, x).astype(BF16))
        return first, second

    k_lo_hi, v_lo_hi = [], []
    for pair in range(ATT_KV // LANES):
        ka, kb = split_pair(kk[:, pair * LANES:(pair + 1) * LANES])
        va, vb = split_pair(vv[:, pair * LANES:(pair + 1) * LANES])
        k_lo_hi += [ka, kb]
        v_lo_hi += [va, vb]

    qi = lax.broadcasted_iota(jnp.int32, (blk, 2 * blk), 0)
    kj = lax.broadcasted_iota(jnp.int32, (blk, 2 * blk), 1)
    dist = qi - kj + blk
    band = (dist >= 0) & (dist < WINDOW)
    distf = dist.astype(F32)
    neg_inf = jnp.float32(-jnp.inf)

    for j in range(ATT_TQ // blk):
        r0 = j * blk
        if j == 0:
            valid = band & ((kj >= blk) | (t > 0))
        else:
            valid = band
        for g in range(ATT_KV_HEADS):
            k_lo, k_hi = k_lo_hi[g]
            v_lo, v_hi = v_lo_hi[g]
            k2 = jnp.concatenate([k_lo[r0:r0 + 2 * blk], k_hi[r0:r0 + 2 * blk]], axis=0)
            v2 = jnp.concatenate([v_lo[r0:r0 + 2 * blk], v_hi[r0:r0 + 2 * blk]], axis=0)
            for pr in range(ATT_GROUP // 2):
                h0 = g * ATT_GROUP + 2 * pr
                c0 = h0 * half
                q2 = q_ref[r0:r0 + blk, c0:c0 + LANES]
                s2 = lax.dot_general(q2, k2, (((1,), (1,)), ((), ())),
                                     preferred_element_type=F32)
                ps = []
                for e in range(2):
                    h = h0 + e
                    sink = scal_ref[0, h]
                    slope = scal_ref[1, h]
                    s = s2[:, e * 2 * blk:(e + 1) * 2 * blk] * scale - slope * distf
                    s = jnp.where(valid, s, neg_inf)
                    m = jnp.maximum(jnp.max(s, axis=-1, keepdims=True), sink)
                    p = jnp.exp(s - m)
                    den = jnp.sum(p, axis=-1, keepdims=True) + jnp.exp(sink - m)
                    ps.append((p * (1.0 / den)).astype(BF16))
                p2 = jnp.concatenate(ps, axis=1)
                o2 = jnp.dot(p2, v2, preferred_element_type=F32)
                o_ref[r0:r0 + blk, c0:c0 + LANES] = o2.astype(BF16)


def _attention(zb, scal, batch, seq):
    n = zb.shape[0]
    tq = ATT_TQ
    nt = seq // tq
    per = tq // ATT_BLOCK
    kcol = (ATT_Q + M_V) // ATT_KV
    vcol = kcol + 1

    def prev_map(col):
        return lambda b, t: (b * (seq // ATT_BLOCK) + jnp.maximum(t * per - 1, 0), col)

    def cur_map(col):
        return lambda b, t: (b * nt + t, col)

    return pl.pallas_call(
        _attn_kernel,
        name="attn",
        grid=(batch, nt),
        in_specs=[
            pl.BlockSpec(memory_space=pltpu.SMEM),
            pl.BlockSpec((tq, ATT_Q), cur_map(0)),
            pl.BlockSpec((ATT_BLOCK, ATT_KV), prev_map(kcol)),
            pl.BlockSpec((tq, ATT_KV), cur_map(kcol)),
            pl.BlockSpec((ATT_BLOCK, ATT_KV), prev_map(vcol)),
            pl.BlockSpec((tq, ATT_KV), cur_map(vcol)),
        ],
        out_specs=pl.BlockSpec((tq, ATT_Q), cur_map(0)),
        out_shape=jax.ShapeDtypeStruct((n, ATT_Q), BF16),
        compiler_params=pltpu.CompilerParams(
            dimension_semantics=("parallel", "arbitrary"), vmem_limit_bytes=VMEM_LIMIT),
    )(scal, zb, zb, zb, zb, zb)


def _mlstm_kernel(qk_ref, v_ref, og_ref, gt_ref, convw_ref, convb_ref, gbias_ref, headg_ref,
                  o_ref, xbuf_ref, c_ref, n_ref, m_ref):
    L = M_CHUNK
    pad = 8

    @pl.when(pl.program_id(1) == 0)
    def _():
        xbuf_ref[0:pad, :] = jnp.zeros((pad, 2 * M_QK), F32)
        c_ref[...] = jnp.zeros_like(c_ref)
        n_ref[...] = jnp.zeros_like(n_ref)
        m_ref[...] = jnp.zeros_like(m_ref)

    xbuf_ref[pad:pad + L, :] = qk_ref[...]
    w = convw_ref[...]
    base = pad - (CONV_WIDTH - 1)
    y = xbuf_ref[pl.ds(base, L), :] * w[0:1, :]
    for j in range(1, CONV_WIDTH):
        y = y + xbuf_ref[pl.ds(base + j, L), :] * w[j:j + 1, :]
    y = y + convb_ref[...]
    xbuf_ref[0:pad, :] = xbuf_ref[L:L + pad, :]
    qk = y * jax.nn.sigmoid(y)

    gl = gt_ref[...] + gbias_ref[...]
    lf = _log_sigmoid(gl)
    gl_t = gl.T
    lf_t = _log_sigmoid(gl_t)

    ti = lax.broadcasted_iota(jnp.int32, (L, L), 0)
    si = lax.broadcasted_iota(jnp.int32, (L, L), 1)
    tril = si <= ti
    triu = ti <= si
    neg_inf = jnp.float32(-jnp.inf)
    kscale = 1.0 / math.sqrt(M_QK_DIM)

    for h in range(M_HEADS):
        q_h = qk[:, h * M_QK_DIM:(h + 1) * M_QK_DIM]
        k_h = qk[:, M_QK + h * M_QK_DIM:M_QK + (h + 1) * M_QK_DIM] * kscale
        v_h = v_ref[:, h * M_V_DIM:(h + 1) * M_V_DIM]
        q_b = q_h.astype(BF16)
        k_b = k_h.astype(BF16)

        li_col = gl[:, h:h + 1]
        lf_col = lf[:, M_HEADS + h:M_HEADS + h + 1]
        li_row = gl_t[h:h + 1, :]
        lf_row = lf_t[M_HEADS + h:M_HEADS + h + 1, :]

        b_col = jnp.sum(jnp.where(tril, lf_row, 0.0), axis=1, keepdims=True)
        b_row = jnp.sum(jnp.where(triu, lf_col, 0.0), axis=0, keepdims=True)
        b_last = b_col[L - 1:L, :]
        m_in = m_ref[h, 0:1, 0:1]

        dmat = jnp.where(tril, b_col - b_row + li_row, neg_inf)
        a_col = b_col + m_in
        m_t = jnp.maximum(a_col, jnp.max(dmat, axis=1, keepdims=True))
        inter_w = jnp.exp(a_col - m_t)
        s = lax.dot_general(q_b, k_b, (((1,), (1,)), ((), ())), preferred_element_type=F32)
        p = s * jnp.exp(dmat - m_t)
        c_in = c_ref[h]
        n_in = n_ref[h]
        num = (jnp.dot(p.astype(BF16), v_h, preferred_element_type=F32)
               + inter_w * jnp.dot(q_b, c_in.astype(BF16), preferred_element_type=F32))
        den = (jnp.sum(p, axis=1, keepdims=True)
               + inter_w * jnp.sum(q_h * n_in, axis=1, keepdims=True))
        hh = num / jnp.maximum(jnp.abs(den), jnp.exp(-m_t))

        g_col = b_last - b_col + li_col
        m_new = jnp.maximum(b_last + m_in, jnp.max(g_col, axis=0, keepdims=True))
        decay = jnp.exp(b_last + m_in - m_new)
        kw = k_h * jnp.exp(g_col - m_new)
        c_ref[h] = decay * c_in + lax.dot_general(
            kw.astype(BF16), v_h, (((0,), (0,)), ((), ())), preferred_element_type=F32)
        n_ref[h] = decay * n_in + jnp.sum(kw, axis=0, keepdims=True)
        m_ref[h] = jnp.broadcast_to(m_new, m_ref.shape[1:])

        cs = slice(h * M_V_DIM, (h + 1) * M_V_DIM)
        hn = _rms(hh, headg_ref[:, cs])
        o_ref[:, cs] = (jax.nn.sigmoid(og_ref[:, cs]) * hn).astype(BF16)


def _mlstm(zb, zf, conv_w, conv_b, gbias, head_g, batch, seq):
    n = zb.shape[0]
    L = M_CHUNK
    nc = seq // L
    row = lambda b, c: b * nc + c
    return pl.pallas_call(
        _mlstm_kernel,
        name="mlstm",
        grid=(batch, nc),
        in_specs=[
            pl.BlockSpec((L, 2 * M_QK), lambda b, c: (row(b, c), 0)),
            pl.BlockSpec((L, M_V), lambda b, c: (row(b, c), ATT_Q // M_V)),
            pl.BlockSpec((L, M_V), lambda b, c: (row(b, c), 2 * M_QK // M_V)),
            pl.BlockSpec((L, GATE_PAD), lambda b, c: (row(b, c), (2 * M_QK + M_V) // GATE_PAD)),
            pl.BlockSpec((CONV_WIDTH, 2 * M_QK), lambda b, c: (0, 0)),
            pl.BlockSpec((1, 2 * M_QK), lambda b, c: (0, 0)),
            pl.BlockSpec((1, GATE_PAD), lambda b, c: (0, 0)),
            pl.BlockSpec((1, M_V), lambda b, c: (0, 0)),
        ],
        out_specs=pl.BlockSpec((L, M_V), lambda b, c: (row(b, c), 0)),
        out_shape=jax.ShapeDtypeStruct((n, M_V), BF16),
        scratch_shapes=[
            pltpu.VMEM((L + 8, 2 * M_QK), F32),
            pltpu.VMEM((M_HEADS, M_QK_DIM, M_V_DIM), F32),
            pltpu.VMEM((M_HEADS, 1, M_QK_DIM), F32),
            pltpu.VMEM((M_HEADS, 8, LANES), F32),
        ],
        compiler_params=pltpu.CompilerParams(
            dimension_semantics=("parallel", "arbitrary"), vmem_limit_bytes=VMEM_LIMIT),
    )(zf, zb, zf, zf, conv_w, conv_b, gbias, head_g)


def _merge_kernel(h_ref, att_ref, hm_ref, pre_g_ref, wga_ref, wgm_ref, wa_ref, wm_ref, wo_ref,
                  post_g_ref, o_ref, u_ref, acc_ref):
    j = pl.program_id(1)

    @pl.when(j == 0)
    def _():
        u_ref[...] = _rms(h_ref[...], pre_g_ref[...]).astype(BF16)

    u = u_ref[...]
    ga = jnp.dot(u, wga_ref[...], preferred_element_type=F32)
    gm = jnp.dot(u, wgm_ref[...], preferred_element_type=F32)
    a = jnp.dot(att_ref[...], wa_ref[...], preferred_element_type=F32)
    m = jnp.dot(hm_ref[...], wm_ref[...], preferred_element_type=F32)
    y = (jax.nn.sigmoid(ga) * a + jax.nn.sigmoid(gm) * m).astype(BF16)
    d = jnp.dot(y, wo_ref[...], preferred_element_type=F32)

    @pl.when(j == 0)
    def _():
        acc_ref[...] = d

    @pl.when(j > 0)
    def _():
        acc_ref[...] += d

    @pl.when(j == pl.num_programs(1) - 1)
    def _():
        o_ref[...] = h_ref[...] + _rms(acc_ref[...], post_g_ref[...])


def _merge(h, att, hm, pre_g, wga, wgm, wa, wm, wo, post_g):
    n = h.shape[0]
    tm, tn = MERGE_TM, MERGE_TN
    return pl.pallas_call(
        _merge_kernel,
        name="merge",
        grid=(n // tm, D_MODEL // tn),
        in_specs=[
            pl.BlockSpec((tm, D_MODEL), lambda i, j: (i, 0)),
            pl.BlockSpec((tm, ATT_Q), lambda i, j: (i, 0)),
            pl.BlockSpec((tm, M_V), lambda i, j: (i, 0)),
            pl.BlockSpec((1, D_MODEL), lambda i, j: (0, 0)),
            pl.BlockSpec((D_MODEL, tn), lambda i, j: (0, j)),
            pl.BlockSpec((D_MODEL, tn), lambda i, j: (0, j)),
            pl.BlockSpec((ATT_Q, tn), lambda i, j: (0, j)),
            pl.BlockSpec((M_V, tn), lambda i, j: (0, j)),
            pl.BlockSpec((tn, D_MODEL), lambda i, j: (j, 0)),
            pl.BlockSpec((1, D_MODEL), lambda i, j: (0, 0)),
        ],
        out_specs=pl.BlockSpec((tm, D_MODEL), lambda i, j: (i, 0)),
        out_shape=jax.ShapeDtypeStruct((n, D_MODEL), F32),
        scratch_shapes=[pltpu.VMEM((tm, D_MODEL), BF16), pltpu.VMEM((tm, D_MODEL), F32)],
        compiler_params=pltpu.CompilerParams(
            dimension_semantics=("parallel", "arbitrary"), vmem_limit_bytes=VMEM_LIMIT),
    )(h, att, hm, pre_g, wga, wgm, wa, wm, wo, post_g)


def _split_w_in(w_in):
    sizes = (ATT_Q, ATT_KV, ATT_KV, M_QK, M_QK, M_V, M_V, M_HEADS, M_HEADS, D_MODEL, D_MODEL)
    parts, off = [], 0
    for size in sizes:
        parts.append(w_in[:, off:off + size])
        off += size
    return parts


def kernel(x, ffn1_pre_g, ffn1_wg, ffn1_wu, ffn1_wd, ffn1_post_g, mix_pre_g, w_in, conv_w, conv_b,
           attn_sinks, m_igate_b, m_fgate_b, m_head_g, w_attn_up, w_mlstm_up, w_out, mix_post_g,
           ffn2_pre_g, ffn2_wg, ffn2_wu, ffn2_wd, ffn2_post_g):
    batch, seq, _ = x.shape
    h = x.reshape(batch * seq, D_MODEL)
    for l in range(ffn1_wg.shape[0]):
        bf = lambda w: w.astype(BF16)
        w_aq, w_ak, w_av, w_mq, w_mk, w_mv, w_mo, w_mi, w_mf, w_ga, w_gm = _split_w_in(w_in[l])
        gate_pad = jnp.zeros((D_MODEL, GATE_PAD - 2 * M_HEADS), F32)
        w_cat = bf(jnp.concatenate(
            [w_aq, w_mv, w_ak, w_av, w_mq, w_mk, w_mo, w_mi, w_mf, gate_pad], axis=1))
        gbias = jnp.concatenate(
            [m_igate_b[l], m_fgate_b[l], jnp.zeros((GATE_PAD - 2 * M_HEADS,), F32)])[None, :]
        slopes = jnp.exp2(-8.0 * jnp.arange(1, ATT_HEADS + 1, dtype=F32) / ATT_HEADS)
        scal = jnp.stack([attn_sinks[l].astype(F32), slopes])

        h = _ffn(h, ffn1_pre_g[l][None, :], bf(ffn1_wg[l]), bf(ffn1_wu[l]), bf(ffn1_wd[l]),
                 ffn1_post_g[l][None, :])
        zb, zf = _inproj(h, mix_pre_g[l][None, :], w_cat)
        att = _attention(zb, scal, batch, seq)
        hm = _mlstm(zb, zf, conv_w[l], conv_b[l][None, :], gbias, m_head_g[l][None, :], batch, seq)
        h = _merge(h, att, hm, mix_pre_g[l][None, :], bf(w_ga), bf(w_gm), bf(w_attn_up[l]),
                   bf(w_mlstm_up[l]), bf(w_out[l]), mix_post_g[l][None, :])
        h = _ffn(h, ffn2_pre_g[l][None, :], bf(ffn2_wg[l]), bf(ffn2_wu[l]), bf(ffn2_wd[l]),
                 ffn2_post_g[l][None, :])
    return h.reshape(batch, seq, D_MODEL)
```

```python
import functools
import math

import jax
import jax.numpy as jnp
from jax import lax
from jax.experimental import pallas as pl
from jax.experimental.pallas import tpu as pltpu

F32 = jnp.float32
BF16 = jnp.bfloat16

D_MODEL = 2048
ATT_HEADS = 16
ATT_KV_HEADS = 4
ATT_HEAD_DIM = 64
ATT_GROUP = ATT_HEADS // ATT_KV_HEADS
WINDOW = 128
ATT_BLOCK = 128
M_HEADS = 4
M_QK_DIM = 128
M_V_DIM = 256
CONV_WIDTH = 4
D_FF = 5632
EPS = 1e-6

ATT_Q = ATT_HEADS * ATT_HEAD_DIM
ATT_KV = ATT_KV_HEADS * ATT_HEAD_DIM
M_QK = M_HEADS * M_QK_DIM
M_V = M_HEADS * M_V_DIM

LANES = 128
GATE_PAD = LANES
ZB_COLS = ATT_Q + M_V + 2 * ATT_KV
ZF_COLS = 2 * M_QK + M_V + GATE_PAD

VMEM_LIMIT = 56 * 1024 * 1024

FFN_TM = 512
FFN_TF = 512
PROJ_TM = 512
PROJ_TN = 512
ATT_TQ = 512
M_CHUNK = 256
MERGE_TM = 512
MERGE_TN = 512


def _rms(x, g):
    return x * lax.rsqrt(jnp.mean(x * x, axis=-1, keepdims=True) + EPS) * g


def _log_sigmoid(x):
    return jnp.minimum(x, 0.0) - jnp.log1p(jnp.exp(-jnp.abs(x)))


def _ffn_kernel(x_ref, pre_g_ref, wg_ref, wu_ref, wd_ref, post_g_ref, o_ref, xn_ref):
    f = pl.program_id(1)

    @pl.when(f == 0)
    def _():
        xn_ref[...] = _rms(x_ref[...], pre_g_ref[...]).astype(BF16)
        o_ref[...] = jnp.zeros_like(o_ref)

    xn = xn_ref[...]
    g = jnp.dot(xn, wg_ref[...], preferred_element_type=F32)
    u = jnp.dot(xn, wu_ref[...], preferred_element_type=F32)
    hid = (g * jax.nn.sigmoid(g) * u).astype(BF16)
    o_ref[...] += jnp.dot(hid, wd_ref[...], preferred_element_type=F32)

    @pl.when(f == pl.num_programs(1) - 1)
    def _():
        o_ref[...] = x_ref[...] + 0.5 * _rms(o_ref[...], post_g_ref[...])


def _ffn(h, pre_g, wg, wu, wd, post_g):
    n = h.shape[0]
    tm, tf = FFN_TM, FFN_TF
    return pl.pallas_call(
        _ffn_kernel,
        name="ffn",
        grid=(n // tm, D_FF // tf),
        in_specs=[
            pl.BlockSpec((tm, D_MODEL), lambda i, f: (i, 0)),
            pl.BlockSpec((1, D_MODEL), lambda i, f: (0, 0)),
            pl.BlockSpec((D_MODEL, tf), lambda i, f: (0, f)),
            pl.BlockSpec((D_MODEL, tf), lambda i, f: (0, f)),
            pl.BlockSpec((tf, D_MODEL), lambda i, f: (f, 0)),
            pl.BlockSpec((1, D_MODEL), lambda i, f: (0, 0)),
        ],
        out_specs=pl.BlockSpec((tm, D_MODEL), lambda i, f: (i, 0)),
        out_shape=jax.ShapeDtypeStruct((n, D_MODEL), F32),
        scratch_shapes=[pltpu.VMEM((tm, D_MODEL), BF16)],
        compiler_params=pltpu.CompilerParams(
            dimension_semantics=("parallel", "arbitrary"), vmem_limit_bytes=VMEM_LIMIT),
    )(h, pre_g, wg, wu, wd, post_g)


def _inproj_kernel(h_ref, g_ref, w_ref, zb_ref, zf_ref, xn_ref):
    xn_ref[...] = _rms(h_ref[...], g_ref[...]).astype(BF16)
    xn = xn_ref[...]
    for c0 in range(0, ZB_COLS, PROJ_TN):
        zb_ref[:, c0:c0 + PROJ_TN] = jnp.dot(
            xn, w_ref[:, c0:c0 + PROJ_TN], preferred_element_type=F32).astype(BF16)
    for c0 in range(0, ZF_COLS, PROJ_TN):
        cw = min(PROJ_TN, ZF_COLS - c0)
        zf_ref[:, c0:c0 + cw] = jnp.dot(
            xn, w_ref[:, ZB_COLS + c0:ZB_COLS + c0 + cw], preferred_element_type=F32)


def _inproj(h, g, w_cat):
    n = h.shape[0]
    tm = PROJ_TM
    return pl.pallas_call(
        _inproj_kernel,
        name="inproj",
        grid=(n // tm,),
        in_specs=[
            pl.BlockSpec((tm, D_MODEL), lambda i: (i, 0)),
            pl.BlockSpec((1, D_MODEL), lambda i: (0, 0)),
            pl.BlockSpec((D_MODEL, ZB_COLS + ZF_COLS), lambda i: (0, 0),
                         pipeline_mode=pl.Buffered(1)),
        ],
        out_specs=[
            pl.BlockSpec((tm, ZB_COLS), lambda i: (i, 0)),
            pl.BlockSpec((tm, ZF_COLS), lambda i: (i, 0)),
        ],
        out_shape=[jax.ShapeDtypeStruct((n, ZB_COLS), BF16),
                   jax.ShapeDtypeStruct((n, ZF_COLS), F32)],
        scratch_shapes=[pltpu.VMEM((tm, D_MODEL), BF16)],
        compiler_params=pltpu.CompilerParams(
            dimension_semantics=("parallel",), vmem_limit_bytes=VMEM_LIMIT),
    )(h, g, w_cat)


def _attn_kernel(scal_ref, q_ref, kp_ref, kc_ref, vp_ref, vc_ref, o_ref, bias_ref):
    t = pl.program_id(1)
    blk = ATT_BLOCK
    half = ATT_HEAD_DIM
    scale = 1.0 / math.sqrt(ATT_HEAD_DIM)
    neg_inf = jnp.float32(-jnp.inf)

    qi = lax.broadcasted_iota(jnp.int32, (blk, 2 * blk), 0)
    kj = lax.broadcasted_iota(jnp.int32, (blk, 2 * blk), 1)

    @pl.when(t == 0)
    def _():
        dist = qi - kj + blk
        band = (dist >= 0) & (dist < WINDOW)
        distf = dist.astype(F32)
        for h in range(ATT_HEADS):
            bias_ref[h] = jnp.where(band, -(scal_ref[1, h] * distf), neg_inf)

    kk = jnp.concatenate([kp_ref[...], kc_ref[...]], axis=0).astype(F32) * scale
    vv = jnp.concatenate([vp_ref[...], vc_ref[...]], axis=0).astype(F32)
    rows = kk.shape[0]
    lane = lax.broadcasted_iota(jnp.int32, (rows, LANES), 1)
    low = lane < half

    def split_pair(x):
        xr = pltpu.roll(x, half, axis=1)
        zero = jnp.zeros_like(x)
        first = (jnp.where(low, x, zero).astype(BF16), jnp.where(low, zero, xr).astype(BF16))
        second = (jnp.where(low, xr, zero).astype(BF16), jnp.where(low, zero, x).astype(BF16))
        return first, second

    k_lo_hi, v_lo_hi = [], []
    for pair in range(ATT_KV // LANES):
        ka, kb = split_pair(kk[:, pair * LANES:(pair + 1) * LANES])
        va, vb = split_pair(vv[:, pair * LANES:(pair + 1) * LANES])
        k_lo_hi += [ka, kb]
        v_lo_hi += [va, vb]

    lane_q = lax.broadcasted_iota(jnp.int32, (blk, LANES), 1)
    has_prev = (kj >= blk) | (t > 0)

    low_q = lane_q < half
    pairs = [(g, pr) for g in range(ATT_KV_HEADS) for pr in range(ATT_GROUP // 2)]
    for j in range(ATT_TQ // blk):
        r0 = j * blk
        k2 = [jnp.concatenate([lo[r0:r0 + 2 * blk], hi[r0:r0 + 2 * blk]], axis=0)
              for lo, hi in k_lo_hi]
        v2 = [jnp.concatenate([lo[r0:r0 + 2 * blk], hi[r0:r0 + 2 * blk]], axis=0)
              for lo, hi in v_lo_hi]
        s2 = []
        for g, pr in pairs:
            c0 = (g * ATT_GROUP + 2 * pr) * half
            s2.append(lax.dot_general(q_ref[r0:r0 + blk, c0:c0 + LANES], k2[g],
                                      (((1,), (1,)), ((), ())), preferred_element_type=F32))
        p2, inv2 = [], []
        for idx, (g, pr) in enumerate(pairs):
            h0 = g * ATT_GROUP + 2 * pr
            ps, ms, sums = [], [], []
            for e in range(2):
                s = s2[idx][:, e * 2 * blk:(e + 1) * 2 * blk] + bias_ref[h0 + e]
                if j == 0:
                    s = jnp.where(has_prev, s, neg_inf)
                m = jnp.maximum(jnp.max(s, axis=-1, keepdims=True), scal_ref[0, h0 + e])
                p = jnp.exp(s - m)
                ps.append(p.astype(BF16))
                ms.append(scal_ref[0, h0 + e] - m)
                sums.append(jnp.sum(p, axis=-1, keepdims=True))
            den = jnp.where(low_q, sums[0], sums[1]) + jnp.exp(jnp.where(low_q, ms[0], ms[1]))
            inv2.append(1.0 / den)
            p2.append(jnp.concatenate(ps, axis=1))
        for idx, (g, pr) in enumerate(pairs):
            c0 = (g * ATT_GROUP + 2 * pr) * half
            o2 = jnp.dot(p2[idx], v2[g], preferred_element_type=F32) * inv2[idx]
            o_ref[r0:r0 + blk, c0:c0 + LANES] = o2.astype(BF16)


def _attention(zb, scal, batch, seq):
    n = zb.shape[0]
    tq = ATT_TQ
    nt = seq // tq
    per = tq // ATT_BLOCK
    kcol = (ATT_Q + M_V) // ATT_KV
    vcol = kcol + 1

    def prev_map(col):
        return lambda b, t: (b * (seq // ATT_BLOCK) + jnp.maximum(t * per - 1, 0), col)

    def cur_map(col):
        return lambda b, t: (b * nt + t, col)

    return pl.pallas_call(
        _attn_kernel,
        name="attn",
        grid=(batch, nt),
        in_specs=[
            pl.BlockSpec(memory_space=pltpu.SMEM),
            pl.BlockSpec((tq, ATT_Q), cur_map(0)),
            pl.BlockSpec((ATT_BLOCK, ATT_KV), prev_map(kcol)),
            pl.BlockSpec((tq, ATT_KV), cur_map(kcol)),
            pl.BlockSpec((ATT_BLOCK, ATT_KV), prev_map(vcol)),
            pl.BlockSpec((tq, ATT_KV), cur_map(vcol)),
        ],
        out_specs=pl.BlockSpec((tq, ATT_Q), cur_map(0)),
        out_shape=jax.ShapeDtypeStruct((n, ATT_Q), BF16),
        scratch_shapes=[pltpu.VMEM((ATT_HEADS, ATT_BLOCK, 2 * ATT_BLOCK), F32)],
        compiler_params=pltpu.CompilerParams(
            dimension_semantics=("parallel", "arbitrary"), vmem_limit_bytes=VMEM_LIMIT),
    )(scal, zb, zb, zb, zb, zb)


def _mlstm_kernel(qk_ref, v_ref, og_ref, gt_ref, convw_ref, convb_ref, gbias_ref, headg_ref,
                  o_ref, xbuf_ref, c_ref, n_ref, m_ref):
    L = M_CHUNK
    pad = 8

    @pl.when(pl.program_id(1) == 0)
    def _():
        xbuf_ref[0:pad, :] = jnp.zeros((pad, 2 * M_QK), F32)
        c_ref[...] = jnp.zeros_like(c_ref)
        n_ref[...] = jnp.zeros_like(n_ref)
        m_ref[...] = jnp.zeros_like(m_ref)

    xbuf_ref[pad:pad + L, :] = qk_ref[...]
    w = convw_ref[...]
    base = pad - (CONV_WIDTH - 1)
    y = xbuf_ref[pl.ds(base, L), :] * w[0:1, :]
    for j in range(1, CONV_WIDTH):
        y = y + xbuf_ref[pl.ds(base + j, L), :] * w[j:j + 1, :]
    y = y + convb_ref[...]
    xbuf_ref[0:pad, :] = xbuf_ref[L:L + pad, :]
    qk = y * jax.nn.sigmoid(y)

    gl = gt_ref[...] + gbias_ref[...]
    lf = _log_sigmoid(gl)
    gl_t = gl.T
    lf_t = _log_sigmoid(gl_t)

    ti = lax.broadcasted_iota(jnp.int32, (L, L), 0)
    si = lax.broadcasted_iota(jnp.int32, (L, L), 1)
    tril = si <= ti
    triu = ti <= si
    neg_inf = jnp.float32(-jnp.inf)
    kscale = 1.0 / math.sqrt(M_QK_DIM)

    for h in range(M_HEADS):
        q_h = qk[:, h * M_QK_DIM:(h + 1) * M_QK_DIM]
        k_h = qk[:, M_QK + h * M_QK_DIM:M_QK + (h + 1) * M_QK_DIM] * kscale
        v_h = v_ref[:, h * M_V_DIM:(h + 1) * M_V_DIM]
        q_b = q_h.astype(BF16)
        k_b = k_h.astype(BF16)

        li_col = gl[:, h:h + 1]
        lf_col = lf[:, M_HEADS + h:M_HEADS + h + 1]
        li_row = gl_t[h:h + 1, :]
        lf_row = lf_t[M_HEADS + h:M_HEADS + h + 1, :]

        b_col = jnp.sum(jnp.where(tril, lf_row, 0.0), axis=1, keepdims=True)
        b_row = jnp.sum(jnp.where(triu, lf_col, 0.0), axis=0, keepdims=True)
        b_last = b_col[L - 1:L, :]
        m_in = m_ref[h, 0:1, 0:1]

        dmat = jnp.where(tril, b_col - b_row + li_row, neg_inf)
        a_col = b_col + m_in
        m_t = jnp.maximum(a_col, jnp.max(dmat, axis=1, keepdims=True))
        inter_w = jnp.exp(a_col - m_t)
        s = lax.dot_general(q_b, k_b, (((1,), (1,)), ((), ())), preferred_element_type=F32)
        p = s * jnp.exp(dmat - m_t)
        c_in = c_ref[h]
        n_in = n_ref[h]
        num = (jnp.dot(p.astype(BF16), v_h, preferred_element_type=F32)
               + inter_w * jnp.dot(q_b, c_in.astype(BF16), preferred_element_type=F32))
        den = (jnp.sum(p, axis=1, keepdims=True)
               + inter_w * jnp.sum(q_h * n_in, axis=1, keepdims=True))
        hh = num / jnp.maximum(jnp.abs(den), jnp.exp(-m_t))

        g_col = b_last - b_col + li_col
        m_new = jnp.maximum(b_last + m_in, jnp.max(g_col, axis=0, keepdims=True))
        decay = jnp.exp(b_last + m_in - m_new)
        kw = k_h * jnp.exp(g_col - m_new)
        c_ref[h] = decay * c_in + lax.dot_general(
            kw.astype(BF16), v_h, (((0,), (0,)), ((), ())), preferred_element_type=F32)
        n_ref[h] = decay * n_in + jnp.sum(kw, axis=0, keepdims=True)
        m_ref[h] = jnp.broadcast_to(m_new, m_ref.shape[1:])

        cs = slice(h * M_V_DIM, (h + 1) * M_V_DIM)
        hn = _rms(hh, headg_ref[:, cs])
        o_ref[:, cs] = (jax.nn.sigmoid(og_ref[:, cs]) * hn).astype(BF16)


def _mlstm(zb, zf, conv_w, conv_b, gbias, head_g, batch, seq):
    n = zb.shape[0]
    L = M_CHUNK
    nc = seq // L
    row = lambda b, c: b * nc + c
    return pl.pallas_call(
        _mlstm_kernel,
        name="mlstm",
        grid=(batch, nc),
        in_specs=[
            pl.BlockSpec((L, 2 * M_QK), lambda b, c: (row(b, c), 0)),
            pl.BlockSpec((L, M_V), lambda b, c: (row(b, c), ATT_Q // M_V)),
            pl.BlockSpec((L, M_V), lambda b, c: (row(b, c), 2 * M_QK // M_V)),
            pl.BlockSpec((L, GATE_PAD), lambda b, c: (row(b, c), (2 * M_QK + M_V) // GATE_PAD)),
            pl.BlockSpec((CONV_WIDTH, 2 * M_QK), lambda b, c: (0, 0)),
            pl.BlockSpec((1, 2 * M_QK), lambda b, c: (0, 0)),
            pl.BlockSpec((1, GATE_PAD), lambda b, c: (0, 0)),
            pl.BlockSpec((1, M_V), lambda b, c: (0, 0)),
        ],
        out_specs=pl.BlockSpec((L, M_V), lambda b, c: (row(b, c), 0)),
        out_shape=jax.ShapeDtypeStruct((n, M_V), BF16),
        scratch_shapes=[
            pltpu.VMEM((L + 8, 2 * M_QK), F32),
            pltpu.VMEM((M_HEADS, M_QK_DIM, M_V_DIM), F32),
            pltpu.VMEM((M_HEADS, 1, M_QK_DIM), F32),
            pltpu.VMEM((M_HEADS, 8, LANES), F32),
        ],
        compiler_params=pltpu.CompilerParams(
            dimension_semantics=("parallel", "arbitrary"), vmem_limit_bytes=VMEM_LIMIT),
    )(zf, zb, zf, zf, conv_w, conv_b, gbias, head_g)


def _merge_kernel(h_ref, att_ref, hm_ref, pre_g_ref, wga_ref, wgm_ref, wa_ref, wm_ref, wo_ref,
                  post_g_ref, o_ref, u_ref):
    j = pl.program_id(1)

    @pl.when(j == 0)
    def _():
        u_ref[...] = _rms(h_ref[...], pre_g_ref[...]).astype(BF16)
        o_ref[...] = jnp.zeros_like(o_ref)

    u = u_ref[...]
    ga = jnp.dot(u, wga_ref[...], preferred_element_type=F32)
    gm = jnp.dot(u, wgm_ref[...], preferred_element_type=F32)
    a = jnp.dot(att_ref[...], wa_ref[...], preferred_element_type=F32)
    m = jnp.dot(hm_ref[...], wm_ref[...], preferred_element_type=F32)
    y = (jax.nn.sigmoid(ga) * a + jax.nn.sigmoid(gm) * m).astype(BF16)
    o_ref[...] += jnp.dot(y, wo_ref[...], preferred_element_type=F32)

    @pl.when(j == pl.num_programs(1) - 1)
    def _():
        o_ref[...] = h_ref[...] + _rms(o_ref[...], post_g_ref[...])


def _merge(h, att, hm, pre_g, wga, wgm, wa, wm, wo, post_g):
    n = h.shape[0]
    tm, tn = MERGE_TM, MERGE_TN
    return pl.pallas_call(
        _merge_kernel,
        name="merge",
        grid=(n // tm, D_MODEL // tn),
        in_specs=[
            pl.BlockSpec((tm, D_MODEL), lambda i, j: (i, 0)),
            pl.BlockSpec((tm, ATT_Q), lambda i, j: (i, 0)),
            pl.BlockSpec((tm, M_V), lambda i, j: (i, 0)),
            pl.BlockSpec((1, D_MODEL), lambda i, j: (0, 0)),
            pl.BlockSpec((D_MODEL, tn), lambda i, j: (0, j)),
            pl.BlockSpec((D_MODEL, tn), lambda i, j: (0, j)),
            pl.BlockSpec((ATT_Q, tn), lambda i, j: (0, j)),
            pl.BlockSpec((M_V, tn), lambda i, j: (0, j)),
            pl.BlockSpec((tn, D_MODEL), lambda i, j: (j, 0)),
            pl.BlockSpec((1, D_MODEL), lambda i, j: (0, 0)),
        ],
        out_specs=pl.BlockSpec((tm, D_MODEL), lambda i, j: (i, 0)),
        out_shape=jax.ShapeDtypeStruct((n, D_MODEL), F32),
        scratch_shapes=[pltpu.VMEM((tm, D_MODEL), BF16)],
        compiler_params=pltpu.CompilerParams(
            dimension_semantics=("parallel", "arbitrary"), vmem_limit_bytes=VMEM_LIMIT),
    )(h, att, hm, pre_g, wga, wgm, wa, wm, wo, post_g)


def _split_w_in(w_in):
    sizes = (ATT_Q, ATT_KV, ATT_KV, M_QK, M_QK, M_V, M_V, M_HEADS, M_HEADS, D_MODEL, D_MODEL)
    parts, off = [], 0
    for size in sizes:
        parts.append(w_in[:, off:off + size])
        off += size
    return parts


def kernel(x, ffn1_pre_g, ffn1_wg, ffn1_wu, ffn1_wd, ffn1_post_g, mix_pre_g, w_in, conv_w, conv_b,
           attn_sinks, m_igate_b, m_fgate_b, m_head_g, w_attn_up, w_mlstm_up, w_out, mix_post_g,
           ffn2_pre_g, ffn2_wg, ffn2_wu, ffn2_wd, ffn2_post_g):
    batch, seq, _ = x.shape
    h = x.reshape(batch * seq, D_MODEL)
    for l in range(ffn1_wg.shape[0]):
        bf = lambda w: w.astype(BF16)
        w_aq, w_ak, w_av, w_mq, w_mk, w_mv, w_mo, w_mi, w_mf, w_ga, w_gm = _split_w_in(w_in[l])
        gate_pad = jnp.zeros((D_MODEL, GATE_PAD - 2 * M_HEADS), F32)
        w_cat = bf(jnp.concatenate(
            [w_aq, w_mv, w_ak, w_av, w_mq, w_mk, w_mo, w_mi, w_mf, gate_pad], axis=1))
        gbias = jnp.concatenate(
            [m_igate_b[l], m_fgate_b[l], jnp.zeros((GATE_PAD - 2 * M_HEADS,), F32)])[None, :]
        slopes = jnp.exp2(-8.0 * jnp.arange(1, ATT_HEADS + 1, dtype=F32) / ATT_HEADS)
        scal = jnp.stack([attn_sinks[l].astype(F32), slopes])

        h = _ffn(h, ffn1_pre_g[l][None, :], bf(ffn1_wg[l]), bf(ffn1_wu[l]), bf(ffn1_wd[l]),
                 ffn1_post_g[l][None, :])
        zb, zf = _inproj(h, mix_pre_g[l][None, :], w_cat)
        att = _attention(zb, scal, batch, seq)
        hm = _mlstm(zb, zf, conv_w[l], conv_b[l][None, :], gbias, m_head_g[l][None, :], batch, seq)
        h = _merge(h, att, hm, mix_pre_g[l][None, :], bf(w_ga), bf(w_gm), bf(w_attn_up[l]),
                   bf(w_mlstm_up[l]), bf(w_out[l]), mix_post_g[l][None, :])
        h = _ffn(h, ffn2_pre_g[l][None, :], bf(ffn2_wg[l]), bf(ffn2_wu[l]), bf(ffn2_wd[l]),
                 ffn2_post_g[l][None, :])
    return h.reshape(batch, seq, D_MODEL)
```

```python
import functools
import math

import jax
import jax.numpy as jnp
from jax import lax
from jax.experimental import pallas as pl
from jax.experimental.pallas import tpu as pltpu

F32 = jnp.float32
BF16 = jnp.bfloat16

D_MODEL = 2048
ATT_HEADS = 16
ATT_KV_HEADS = 4
ATT_HEAD_DIM = 64
ATT_GROUP = ATT_HEADS // ATT_KV_HEADS
WINDOW = 128
ATT_BLOCK = 128
M_HEADS = 4
M_QK_DIM = 128
M_V_DIM = 256
CONV_WIDTH = 4
D_FF = 5632
EPS = 1e-6

ATT_Q = ATT_HEADS * ATT_HEAD_DIM
ATT_KV = ATT_KV_HEADS * ATT_HEAD_DIM
M_QK = M_HEADS * M_QK_DIM
M_V = M_HEADS * M_V_DIM

LANES = 128
GATE_PAD = LANES
ZB_COLS = ATT_Q + M_V + 2 * ATT_KV
ZF_COLS = 2 * M_QK + M_V + GATE_PAD

VMEM_LIMIT = 56 * 1024 * 1024

FFN_TM = 1024
FFN_TF = 512
PROJ_TM = 512
PROJ_TN = 512
ATT_TQ = 512
M_CHUNK = 256
MERGE_TM = 512
MERGE_TN = 512


def _rms(x, g):
    return x * lax.rsqrt(jnp.mean(x * x, axis=-1, keepdims=True) + EPS) * g


def _log_sigmoid(x):
    return jnp.minimum(x, 0.0) - jnp.log1p(jnp.exp(-jnp.abs(x)))


SLICE_ROWS = 128


def _slice_index(f, n_slices):
    return jnp.minimum(f, n_slices - 1)


def _slice_rows(n_slices):
    s = _slice_index(pl.program_id(1), n_slices)
    return pl.ds(pl.multiple_of(s * SLICE_ROWS, SLICE_ROWS), SLICE_ROWS)


def _pre_norm_slice(n_slices, xa_ref, pre_g_ref, xn_next):
    xn_next[_slice_rows(n_slices), :] = _rms(xa_ref[...], pre_g_ref[...]).astype(BF16)


def _post_norm_slice(n_slices, res_scale, xb_ref, post_g_ref, o_ref, acc_prev):
    o_ref[...] = xb_ref[...] + res_scale * _rms(acc_prev[_slice_rows(n_slices), :], post_g_ref[...])


def _for_row_roles(xn_bufs, acc_bufs, body):
    r = pl.program_id(0)
    last = pl.num_programs(0) - 1

    @pl.when((r == 0) & (pl.program_id(1) == 0))
    def _():
        for acc in acc_bufs:
            acc[...] = jnp.zeros_like(acc)

    main = (r > 0) & (r < last)
    for p in range(2):
        roles = (xn_bufs[p], acc_bufs[p], xn_bufs[1 - p], acc_bufs[1 - p])
        pl.when(main & (r % 2 == p))(functools.partial(body, *roles, True))
        pl.when(jnp.logical_not(main) & (r % 2 == p))(functools.partial(body, *roles, False))


def _pipeline_maps(n_tiles, n_slices, n_steps):
    def lead(r, f):
        return (jnp.minimum(r, n_tiles - 1) * n_slices + _slice_index(f, n_slices), 0)

    def lag(r, f):
        return (jnp.where(r < 2, 0, (r - 2) * n_slices + _slice_index(f, n_slices)), 0)

    def step(r, f):
        return jnp.where(r == 0, 0, jnp.where(r == n_tiles + 1, n_steps - 1, f))

    def tile(r, f):
        return (jnp.clip(r - 1, 0, n_tiles - 1), 0)

    return lead, lag, step, tile


def _ffn_kernel(xa_ref, xb_ref, pre_g_ref, wg_ref, wu_ref, wd_ref, post_g_ref, o_ref,
                xn0, xn1, acc0, acc1):
    f = pl.program_id(1)

    n_slices = FFN_TM // SLICE_ROWS

    def body(xn_next, acc_prev, xn_cur, acc_cur, do_main):
        _pre_norm_slice(n_slices, xa_ref, pre_g_ref, xn_next)
        _post_norm_slice(n_slices, 0.5, xb_ref, post_g_ref, o_ref, acc_prev)
        if do_main:
            xn = xn_cur[...]
            g = jnp.dot(xn, wg_ref[...], preferred_element_type=F32)
            u = jnp.dot(xn, wu_ref[...], preferred_element_type=F32)
            hid = (g * jax.nn.sigmoid(g) * u).astype(BF16)
            acc_cur[...] = (jnp.where(f > 0, acc_cur[...], 0.0)
                            + jnp.dot(hid, wd_ref[...], preferred_element_type=F32))

    _for_row_roles((xn0, xn1), (acc0, acc1), body)


def _ffn(h, pre_g, wg, wu, wd, post_g):
    n = h.shape[0]
    tm, tf = FFN_TM, FFN_TF
    n_tiles, n_steps, n_slices = n // tm, D_FF // tf, tm // SLICE_ROWS
    assert n_steps >= n_slices
    lead, lag, step, _ = _pipeline_maps(n_tiles, n_slices, n_steps)
    return pl.pallas_call(
        _ffn_kernel,
        name="ffn",
        grid=(n_tiles + 2, n_steps),
        in_specs=[
            pl.BlockSpec((SLICE_ROWS, D_MODEL), lead),
            pl.BlockSpec((SLICE_ROWS, D_MODEL), lag),
            pl.BlockSpec((1, D_MODEL), lambda r, f: (0, 0)),
            pl.BlockSpec((D_MODEL, tf), lambda r, f: (0, step(r, f))),
            pl.BlockSpec((D_MODEL, tf), lambda r, f: (0, step(r, f))),
            pl.BlockSpec((tf, D_MODEL), lambda r, f: (step(r, f), 0)),
            pl.BlockSpec((1, D_MODEL), lambda r, f: (0, 0)),
        ],
        out_specs=pl.BlockSpec((SLICE_ROWS, D_MODEL), lag),
        out_shape=jax.ShapeDtypeStruct((n, D_MODEL), F32),
        scratch_shapes=[pltpu.VMEM((tm, D_MODEL), BF16), pltpu.VMEM((tm, D_MODEL), BF16),
                        pltpu.VMEM((tm, D_MODEL), F32), pltpu.VMEM((tm, D_MODEL), F32)],
        compiler_params=pltpu.CompilerParams(
            dimension_semantics=("arbitrary", "arbitrary"), vmem_limit_bytes=VMEM_LIMIT),
    )(h, h, pre_g, wg, wu, wd, post_g)


def _inproj_kernel(h_ref, g_ref, w_ref, zb_ref, zf_ref, xn_ref):
    xn_ref[...] = _rms(h_ref[...], g_ref[...]).astype(BF16)
    xn = xn_ref[...]
    for c0 in range(0, ZB_COLS, PROJ_TN):
        zb_ref[:, c0:c0 + PROJ_TN] = jnp.dot(
            xn, w_ref[:, c0:c0 + PROJ_TN], preferred_element_type=F32).astype(BF16)
    for c0 in range(0, ZF_COLS, PROJ_TN):
        cw = min(PROJ_TN, ZF_COLS - c0)
        zf_ref[:, c0:c0 + cw] = jnp.dot(
            xn, w_ref[:, ZB_COLS + c0:ZB_COLS + c0 + cw], preferred_element_type=F32)


def _inproj(h, g, w_cat):
    n = h.shape[0]
    tm = PROJ_TM
    return pl.pallas_call(
        _inproj_kernel,
        name="inproj",
        grid=(n // tm,),
        in_specs=[
            pl.BlockSpec((tm, D_MODEL), lambda i: (i, 0)),
            pl.BlockSpec((1, D_MODEL), lambda i: (0, 0)),
            pl.BlockSpec((D_MODEL, ZB_COLS + ZF_COLS), lambda i: (0, 0),
                         pipeline_mode=pl.Buffered(1)),
        ],
        out_specs=[
            pl.BlockSpec((tm, ZB_COLS), lambda i: (i, 0)),
            pl.BlockSpec((tm, ZF_COLS), lambda i: (i, 0)),
        ],
        out_shape=[jax.ShapeDtypeStruct((n, ZB_COLS), BF16),
                   jax.ShapeDtypeStruct((n, ZF_COLS), F32)],
        scratch_shapes=[pltpu.VMEM((tm, D_MODEL), BF16)],
        compiler_params=pltpu.CompilerParams(
            dimension_semantics=("parallel",), vmem_limit_bytes=VMEM_LIMIT),
    )(h, g, w_cat)


def _attn_kernel(scal_ref, q_ref, kp_ref, kc_ref, vp_ref, vc_ref, o_ref, bias_ref):
    t = pl.program_id(1)
    blk = ATT_BLOCK
    half = ATT_HEAD_DIM
    scale = 1.0 / math.sqrt(ATT_HEAD_DIM)
    neg_inf = jnp.float32(-jnp.inf)

    qi = lax.broadcasted_iota(jnp.int32, (blk, 2 * blk), 0)
    kj = lax.broadcasted_iota(jnp.int32, (blk, 2 * blk), 1)

    @pl.when(t == 0)
    def _():
        dist = qi - kj + blk
        band = (dist >= 0) & (dist < WINDOW)
        distf = dist.astype(F32)
        for h in range(ATT_HEADS):
            bias_ref[h] = jnp.where(band, -(scal_ref[1, h] * distf), neg_inf)

    kk = jnp.concatenate([kp_ref[...], kc_ref[...]], axis=0).astype(F32) * scale
    vv = jnp.concatenate([vp_ref[...], vc_ref[...]], axis=0).astype(F32)
    rows = kk.shape[0]
    lane = lax.broadcasted_iota(jnp.int32, (rows, LANES), 1)
    low = lane < half

    def split_pair(x):
        xr = pltpu.roll(x, half, axis=1)
        zero = jnp.zeros_like(x)
        first = (jnp.where(low, x, zero).astype(BF16), jnp.where(low, zero, xr).astype(BF16))
        second = (jnp.where(low, xr, zero).astype(BF16), jnp.where(low, zero, x).astype(BF16))
        return first, second

    k_lo_hi, v_lo_hi = [], []
    for pair in range(ATT_KV // LANES):
        ka, kb = split_pair(kk[:, pair * LANES:(pair + 1) * LANES])
        va, vb = split_pair(vv[:, pair * LANES:(pair + 1) * LANES])
        k_lo_hi += [ka, kb]
        v_lo_hi += [va, vb]

    lane_q = lax.broadcasted_iota(jnp.int32, (blk, LANES), 1)
    has_prev = (kj >= blk) | (t > 0)

    low_q = lane_q < half
    pairs = [(g, pr) for g in range(ATT_KV_HEADS) for pr in range(ATT_GROUP // 2)]
    for j in range(ATT_TQ // blk):
        r0 = j * blk
        k2 = [jnp.concatenate([lo[r0:r0 + 2 * blk], hi[r0:r0 + 2 * blk]], axis=0)
              for lo, hi in k_lo_hi]
        v2 = [jnp.concatenate([lo[r0:r0 + 2 * blk], hi[r0:r0 + 2 * blk]], axis=0)
              for lo, hi in v_lo_hi]
        s2 = []
        for g, pr in pairs:
            c0 = (g * ATT_GROUP + 2 * pr) * half
            s2.append(lax.dot_general(q_ref[r0:r0 + blk, c0:c0 + LANES], k2[g],
                                      (((1,), (1,)), ((), ())), preferred_element_type=F32))
        p2, inv2 = [], []
        for idx, (g, pr) in enumerate(pairs):
            h0 = g * ATT_GROUP + 2 * pr
            ps, ms, sums = [], [], []
            for e in range(2):
                s = s2[idx][:, e * 2 * blk:(e + 1) * 2 * blk] + bias_ref[h0 + e]
                if j == 0:
                    s = jnp.where(has_prev, s, neg_inf)
                m = jnp.maximum(jnp.max(s, axis=-1, keepdims=True), scal_ref[0, h0 + e])
                p = jnp.exp(s - m)
                ps.append(p.astype(BF16))
                ms.append(scal_ref[0, h0 + e] - m)
                sums.append(jnp.sum(p, axis=-1, keepdims=True))
            den = jnp.where(low_q, sums[0], sums[1]) + jnp.exp(jnp.where(low_q, ms[0], ms[1]))
            inv2.append(1.0 / den)
            p2.append(jnp.concatenate(ps, axis=1))
        for idx, (g, pr) in enumerate(pairs):
            c0 = (g * ATT_GROUP + 2 * pr) * half
            o2 = jnp.dot(p2[idx], v2[g], preferred_element_type=F32) * inv2[idx]
            o_ref[r0:r0 + blk, c0:c0 + LANES] = o2.astype(BF16)


def _attention(zb, scal, batch, seq):
    n = zb.shape[0]
    tq = ATT_TQ
    nt = seq // tq
    per = tq // ATT_BLOCK
    kcol = (ATT_Q + M_V) // ATT_KV
    vcol = kcol + 1

    def prev_map(col):
        return lambda b, t: (b * (seq // ATT_BLOCK) + jnp.maximum(t * per - 1, 0), col)

    def cur_map(col):
        return lambda b, t: (b * nt + t, col)

    return pl.pallas_call(
        _attn_kernel,
        name="attn",
        grid=(batch, nt),
        in_specs=[
            pl.BlockSpec(memory_space=pltpu.SMEM),
            pl.BlockSpec((tq, ATT_Q), cur_map(0)),
            pl.BlockSpec((ATT_BLOCK, ATT_KV), prev_map(kcol)),
            pl.BlockSpec((tq, ATT_KV), cur_map(kcol)),
            pl.BlockSpec((ATT_BLOCK, ATT_KV), prev_map(vcol)),
            pl.BlockSpec((tq, ATT_KV), cur_map(vcol)),
        ],
        out_specs=pl.BlockSpec((tq, ATT_Q), cur_map(0)),
        out_shape=jax.ShapeDtypeStruct((n, ATT_Q), BF16),
        scratch_shapes=[pltpu.VMEM((ATT_HEADS, ATT_BLOCK, 2 * ATT_BLOCK), F32)],
        compiler_params=pltpu.CompilerParams(
            dimension_semantics=("parallel", "arbitrary"), vmem_limit_bytes=VMEM_LIMIT),
    )(scal, zb, zb, zb, zb, zb)


def _mlstm_kernel(qk_ref, v_ref, og_ref, gt_ref, convw_ref, convb_ref, gbias_ref, headg_ref,
                  o_ref, xbuf_ref, c_ref, n_ref, m_ref):
    L = M_CHUNK
    pad = 8

    @pl.when(pl.program_id(1) == 0)
    def _():
        xbuf_ref[0:pad, :] = jnp.zeros((pad, 2 * M_QK), F32)
        c_ref[...] = jnp.zeros_like(c_ref)
        n_ref[...] = jnp.zeros_like(n_ref)
        m_ref[...] = jnp.zeros_like(m_ref)

    xbuf_ref[pad:pad + L, :] = qk_ref[...]
    w = convw_ref[...]
    base = pad - (CONV_WIDTH - 1)
    y = xbuf_ref[pl.ds(base, L), :] * w[0:1, :]
    for j in range(1, CONV_WIDTH):
        y = y + xbuf_ref[pl.ds(base + j, L), :] * w[j:j + 1, :]
    y = y + convb_ref[...]
    xbuf_ref[0:pad, :] = xbuf_ref[L:L + pad, :]
    qk = y * jax.nn.sigmoid(y)

    gl = gt_ref[...] + gbias_ref[...]
    lf = _log_sigmoid(gl)
    gl_t = gl.T
    lf_t = _log_sigmoid(gl_t)

    ti = lax.broadcasted_iota(jnp.int32, (L, L), 0)
    si = lax.broadcasted_iota(jnp.int32, (L, L), 1)
    tril = si <= ti
    triu = ti <= si
    neg_inf = jnp.float32(-jnp.inf)
    kscale = 1.0 / math.sqrt(M_QK_DIM)

    for h in range(M_HEADS):
        q_h = qk[:, h * M_QK_DIM:(h + 1) * M_QK_DIM]
        k_h = qk[:, M_QK + h * M_QK_DIM:M_QK + (h + 1) * M_QK_DIM] * kscale
        v_h = v_ref[:, h * M_V_DIM:(h + 1) * M_V_DIM]
        q_b = q_h.astype(BF16)
        k_b = k_h.astype(BF16)

        li_col = gl[:, h:h + 1]
        lf_col = lf[:, M_HEADS + h:M_HEADS + h + 1]
        li_row = gl_t[h:h + 1, :]
        lf_row = lf_t[M_HEADS + h:M_HEADS + h + 1, :]

        b_col = jnp.sum(jnp.where(tril, lf_row, 0.0), axis=1, keepdims=True)
        b_row = jnp.sum(jnp.where(triu, lf_col, 0.0), axis=0, keepdims=True)
        b_last = b_col[L - 1:L, :]
        m_in = m_ref[h, 0:1, 0:1]

        dmat = jnp.where(tril, b_col - b_row + li_row, neg_inf)
        a_col = b_col + m_in
        m_t = jnp.maximum(a_col, jnp.max(dmat, axis=1, keepdims=True))
        inter_w = jnp.exp(a_col - m_t)
        s = lax.dot_general(q_b, k_b, (((1,), (1,)), ((), ())), preferred_element_type=F32)
        p = s * jnp.exp(dmat - m_t)
        c_in = c_ref[h]
        n_in = n_ref[h]
        num = (jnp.dot(p.astype(BF16), v_h, preferred_element_type=F32)
               + inter_w * jnp.dot(q_b, c_in.astype(BF16), preferred_element_type=F32))
        den = (jnp.sum(p, axis=1, keepdims=True)
               + inter_w * jnp.sum(q_h * n_in, axis=1, keepdims=True))
        hh = num / jnp.maximum(jnp.abs(den), jnp.exp(-m_t))

        g_col = b_last - b_col + li_col
        m_new = jnp.maximum(b_last + m_in, jnp.max(g_col, axis=0, keepdims=True))
        decay = jnp.exp(b_last + m_in - m_new)
        kw = k_h * jnp.exp(g_col - m_new)
        c_ref[h] = decay * c_in + lax.dot_general(
            kw.astype(BF16), v_h, (((0,), (0,)), ((), ())), preferred_element_type=F32)
        n_ref[h] = decay * n_in + jnp.sum(kw, axis=0, keepdims=True)
        m_ref[h] = jnp.broadcast_to(m_new, m_ref.shape[1:])

        cs = slice(h * M_V_DIM, (h + 1) * M_V_DIM)
        hn = _rms(hh, headg_ref[:, cs])
        o_ref[:, cs] = (jax.nn.sigmoid(og_ref[:, cs]) * hn).astype(BF16)


def _mlstm(zb, zf, conv_w, conv_b, gbias, head_g, batch, seq):
    n = zb.shape[0]
    L = M_CHUNK
    nc = seq // L
    row = lambda b, c: b * nc + c
    return pl.pallas_call(
        _mlstm_kernel,
        name="mlstm",
        grid=(batch, nc),
        in_specs=[
            pl.BlockSpec((L, 2 * M_QK), lambda b, c: (row(b, c), 0)),
            pl.BlockSpec((L, M_V), lambda b, c: (row(b, c), ATT_Q // M_V)),
            pl.BlockSpec((L, M_V), lambda b, c: (row(b, c), 2 * M_QK // M_V)),
            pl.BlockSpec((L, GATE_PAD), lambda b, c: (row(b, c), (2 * M_QK + M_V) // GATE_PAD)),
            pl.BlockSpec((CONV_WIDTH, 2 * M_QK), lambda b, c: (0, 0)),
            pl.BlockSpec((1, 2 * M_QK), lambda b, c: (0, 0)),
            pl.BlockSpec((1, GATE_PAD), lambda b, c: (0, 0)),
            pl.BlockSpec((1, M_V), lambda b, c: (0, 0)),
        ],
        out_specs=pl.BlockSpec((L, M_V), lambda b, c: (row(b, c), 0)),
        out_shape=jax.ShapeDtypeStruct((n, M_V), BF16),
        scratch_shapes=[
            pltpu.VMEM((L + 8, 2 * M_QK), F32),
            pltpu.VMEM((M_HEADS, M_QK_DIM, M_V_DIM), F32),
            pltpu.VMEM((M_HEADS, 1, M_QK_DIM), F32),
            pltpu.VMEM((M_HEADS, 8, LANES), F32),
        ],
        compiler_params=pltpu.CompilerParams(
            dimension_semantics=("parallel", "arbitrary"), vmem_limit_bytes=VMEM_LIMIT),
    )(zf, zb, zf, zf, conv_w, conv_b, gbias, head_g)


def _merge_kernel(ha_ref, hb_ref, att_ref, hm_ref, pre_g_ref, wga_ref, wgm_ref, wa_ref, wm_ref, wo_ref,
                  post_g_ref, o_ref, u0, u1, acc0, acc1):
    j = pl.program_id(1)

    n_slices = MERGE_TM // SLICE_ROWS

    def body(u_next, acc_prev, u_cur, acc_cur, do_main):
        _pre_norm_slice(n_slices, ha_ref, pre_g_ref, u_next)
        _post_norm_slice(n_slices, 1.0, hb_ref, post_g_ref, o_ref, acc_prev)
        if do_main:
            u = u_cur[...]
            ga = jnp.dot(u, wga_ref[...], preferred_element_type=F32)
            gm = jnp.dot(u, wgm_ref[...], preferred_element_type=F32)
            a = jnp.dot(att_ref[...], wa_ref[...], preferred_element_type=F32)
            m = jnp.dot(hm_ref[...], wm_ref[...], preferred_element_type=F32)
            y = (jax.nn.sigmoid(ga) * a + jax.nn.sigmoid(gm) * m).astype(BF16)
            acc_cur[...] = (jnp.where(j > 0, acc_cur[...], 0.0)
                            + jnp.dot(y, wo_ref[...], preferred_element_type=F32))

    _for_row_roles((u0, u1), (acc0, acc1), body)


def _merge(h, att, hm, pre_g, wga, wgm, wa, wm, wo, post_g):
    n = h.shape[0]
    tm, tn = MERGE_TM, MERGE_TN
    n_tiles, n_steps, n_slices = n // tm, D_MODEL // tn, tm // SLICE_ROWS
    assert n_steps >= n_slices
    lead, lag, step, tile = _pipeline_maps(n_tiles, n_slices, n_steps)
    return pl.pallas_call(
        _merge_kernel,
        name="merge",
        grid=(n_tiles + 2, n_steps),
        in_specs=[
            pl.BlockSpec((SLICE_ROWS, D_MODEL), lead),
            pl.BlockSpec((SLICE_ROWS, D_MODEL), lag),
            pl.BlockSpec((tm, ATT_Q), tile),
            pl.BlockSpec((tm, M_V), tile),
            pl.BlockSpec((1, D_MODEL), lambda r, j: (0, 0)),
            pl.BlockSpec((D_MODEL, tn), lambda r, j: (0, step(r, j))),
            pl.BlockSpec((D_MODEL, tn), lambda r, j: (0, step(r, j))),
            pl.BlockSpec((ATT_Q, tn), lambda r, j: (0, step(r, j))),
            pl.BlockSpec((M_V, tn), lambda r, j: (0, step(r, j))),
            pl.BlockSpec((tn, D_MODEL), lambda r, j: (step(r, j), 0)),
            pl.BlockSpec((1, D_MODEL), lambda r, j: (0, 0)),
        ],
        out_specs=pl.BlockSpec((SLICE_ROWS, D_MODEL), lag),
        out_shape=jax.ShapeDtypeStruct((n, D_MODEL), F32),
        scratch_shapes=[pltpu.VMEM((tm, D_MODEL), BF16), pltpu.VMEM((tm, D_MODEL), BF16),
                        pltpu.VMEM((tm, D_MODEL), F32), pltpu.VMEM((tm, D_MODEL), F32)],
        compiler_params=pltpu.CompilerParams(
            dimension_semantics=("arbitrary", "arbitrary"), vmem_limit_bytes=VMEM_LIMIT),
    )(h, h, att, hm, pre_g, wga, wgm, wa, wm, wo, post_g)


def _split_w_in(w_in):
    sizes = (ATT_Q, ATT_KV, ATT_KV, M_QK, M_QK, M_V, M_V, M_HEADS, M_HEADS, D_MODEL, D_MODEL)
    parts, off = [], 0
    for size in sizes:
        parts.append(w_in[:, off:off + size])
        off += size
    return parts


def kernel(x, ffn1_pre_g, ffn1_wg, ffn1_wu, ffn1_wd, ffn1_post_g, mix_pre_g, w_in, conv_w, conv_b,
           attn_sinks, m_igate_b, m_fgate_b, m_head_g, w_attn_up, w_mlstm_up, w_out, mix_post_g,
           ffn2_pre_g, ffn2_wg, ffn2_wu, ffn2_wd, ffn2_post_g):
    batch, seq, _ = x.shape
    h = x.reshape(batch * seq, D_MODEL)
    for l in range(ffn1_wg.shape[0]):
        bf = lambda w: w.astype(BF16)
        w_aq, w_ak, w_av, w_mq, w_mk, w_mv, w_mo, w_mi, w_mf, w_ga, w_gm = _split_w_in(w_in[l])
        gate_pad = jnp.zeros((D_MODEL, GATE_PAD - 2 * M_HEADS), F32)
        w_cat = bf(jnp.concatenate(
            [w_aq, w_mv, w_ak, w_av, w_mq, w_mk, w_mo, w_mi, w_mf, gate_pad], axis=1))
        gbias = jnp.concatenate(
            [m_igate_b[l], m_fgate_b[l], jnp.zeros((GATE_PAD - 2 * M_HEADS,), F32)])[None, :]
        slopes = jnp.exp2(-8.0 * jnp.arange(1, ATT_HEADS + 1, dtype=F32) / ATT_HEADS)
        scal = jnp.stack([attn_sinks[l].astype(F32), slopes])

        h = _ffn(h, ffn1_pre_g[l][None, :], bf(ffn1_wg[l]), bf(ffn1_wu[l]), bf(ffn1_wd[l]),
                 ffn1_post_g[l][None, :])
        zb, zf = _inproj(h, mix_pre_g[l][None, :], w_cat)
        att = _attention(zb, scal, batch, seq)
        hm = _mlstm(zb, zf, conv_w[l], conv_b[l][None, :], gbias, m_head_g[l][None, :], batch, seq)
        h = _merge(h, att, hm, mix_pre_g[l][None, :], bf(w_ga), bf(w_gm), bf(w_attn_up[l]),
                   bf(w_mlstm_up[l]), bf(w_out[l]), mix_post_g[l][None, :])
        h = _ffn(h, ffn2_pre_g[l][None, :], bf(ffn2_wg[l]), bf(ffn2_wu[l]), bf(ffn2_wd[l]),
                 ffn2_post_g[l][None, :])
    return h.reshape(batch, seq, D_MODEL)
```

```python
import functools
import math

import jax
import jax.numpy as jnp
from jax import lax
from jax.experimental import pallas as pl
from jax.experimental.pallas import tpu as pltpu

F32 = jnp.float32
BF16 = jnp.bfloat16

D_MODEL = 2048
ATT_HEADS = 16
ATT_KV_HEADS = 4
ATT_HEAD_DIM = 64
ATT_GROUP = ATT_HEADS // ATT_KV_HEADS
WINDOW = 128
ATT_BLOCK = 128
M_HEADS = 4
M_QK_DIM = 128
M_V_DIM = 256
CONV_WIDTH = 4
D_FF = 5632
EPS = 1e-6

ATT_Q = ATT_HEADS * ATT_HEAD_DIM
ATT_KV = ATT_KV_HEADS * ATT_HEAD_DIM
M_QK = M_HEADS * M_QK_DIM
M_V = M_HEADS * M_V_DIM

LANES = 128
GATE_PAD = LANES
GATE_ROWS = 2 * M_HEADS
ZB_COLS = ATT_Q + M_V + 2 * ATT_KV
ZF_COLS = 2 * M_QK + M_V + GATE_PAD

VMEM_LIMIT = 56 * 1024 * 1024

FFN_TM = 1024
FFN_TF = 512
PROJ_TM = 512
PROJ_TN = 512
ATT_TQ = 512
M_CHUNK = 256
MERGE_TM = 512
MERGE_TN = 512


def _rms(x, g):
    return x * lax.rsqrt(jnp.mean(x * x, axis=-1, keepdims=True) + EPS) * g


def _log_sigmoid(x):
    return jnp.minimum(x, 0.0) - jnp.log1p(jnp.exp(-jnp.abs(x)))


SLICE_ROWS = 128


def _slice_index(f, n_slices):
    return jnp.minimum(f, n_slices - 1)


def _slice_rows(n_slices):
    s = _slice_index(pl.program_id(1), n_slices)
    return pl.ds(pl.multiple_of(s * SLICE_ROWS, SLICE_ROWS), SLICE_ROWS)


def _pre_norm_slice(n_slices, xa_ref, pre_g_ref, xn_next):
    xn_next[_slice_rows(n_slices), :] = _rms(xa_ref[...], pre_g_ref[...]).astype(BF16)


def _post_norm_slice(n_slices, res_scale, xb_ref, post_g_ref, o_ref, acc_prev):
    o_ref[...] = xb_ref[...] + res_scale * _rms(acc_prev[_slice_rows(n_slices), :], post_g_ref[...])


def _for_row_roles(xn_bufs, acc_bufs, body):
    r = pl.program_id(0)
    last = pl.num_programs(0) - 1

    @pl.when((r == 0) & (pl.program_id(1) == 0))
    def _():
        for acc in acc_bufs:
            acc[...] = jnp.zeros_like(acc)

    main = (r > 0) & (r < last)
    for p in range(2):
        roles = (xn_bufs[p], acc_bufs[p], xn_bufs[1 - p], acc_bufs[1 - p])
        pl.when(main & (r % 2 == p))(functools.partial(body, *roles, True))
        pl.when(jnp.logical_not(main) & (r % 2 == p))(functools.partial(body, *roles, False))


def _pipeline_maps(n_tiles, n_slices, n_steps):
    def lead(r, f):
        return (jnp.minimum(r, n_tiles - 1) * n_slices + _slice_index(f, n_slices), 0)

    def lag(r, f):
        return (jnp.where(r < 2, 0, (r - 2) * n_slices + _slice_index(f, n_slices)), 0)

    def step(r, f):
        return jnp.where(r == 0, 0, jnp.where(r == n_tiles + 1, n_steps - 1, f))

    def tile(r, f):
        return (jnp.clip(r - 1, 0, n_tiles - 1), 0)

    return lead, lag, step, tile


def _ffn_kernel(xa_ref, xb_ref, pre_g_ref, wg_ref, wu_ref, wd_ref, post_g_ref, o_ref,
                xn0, xn1, acc0, acc1):
    f = pl.program_id(1)

    n_slices = FFN_TM // SLICE_ROWS

    def body(xn_next, acc_prev, xn_cur, acc_cur, do_main):
        _pre_norm_slice(n_slices, xa_ref, pre_g_ref, xn_next)
        _post_norm_slice(n_slices, 0.5, xb_ref, post_g_ref, o_ref, acc_prev)
        if do_main:
            xn = xn_cur[...]
            g = jnp.dot(xn, wg_ref[...], preferred_element_type=F32)
            u = jnp.dot(xn, wu_ref[...], preferred_element_type=F32)
            hid = (g * jax.nn.sigmoid(g) * u).astype(BF16)
            acc_cur[...] = (jnp.where(f > 0, acc_cur[...], 0.0)
                            + jnp.dot(hid, wd_ref[...], preferred_element_type=F32))

    _for_row_roles((xn0, xn1), (acc0, acc1), body)


def _ffn(h, pre_g, wg, wu, wd, post_g):
    n = h.shape[0]
    tm, tf = FFN_TM, FFN_TF
    n_tiles, n_steps, n_slices = n // tm, D_FF // tf, tm // SLICE_ROWS
    assert n_steps >= n_slices
    lead, lag, step, _ = _pipeline_maps(n_tiles, n_slices, n_steps)
    return pl.pallas_call(
        _ffn_kernel,
        name="ffn",
        grid=(n_tiles + 2, n_steps),
        in_specs=[
            pl.BlockSpec((SLICE_ROWS, D_MODEL), lead),
            pl.BlockSpec((SLICE_ROWS, D_MODEL), lag),
            pl.BlockSpec((1, D_MODEL), lambda r, f: (0, 0)),
            pl.BlockSpec((D_MODEL, tf), lambda r, f: (0, step(r, f))),
            pl.BlockSpec((D_MODEL, tf), lambda r, f: (0, step(r, f))),
            pl.BlockSpec((tf, D_MODEL), lambda r, f: (step(r, f), 0)),
            pl.BlockSpec((1, D_MODEL), lambda r, f: (0, 0)),
        ],
        out_specs=pl.BlockSpec((SLICE_ROWS, D_MODEL), lag),
        out_shape=jax.ShapeDtypeStruct((n, D_MODEL), F32),
        scratch_shapes=[pltpu.VMEM((tm, D_MODEL), BF16), pltpu.VMEM((tm, D_MODEL), BF16),
                        pltpu.VMEM((tm, D_MODEL), F32), pltpu.VMEM((tm, D_MODEL), F32)],
        compiler_params=pltpu.CompilerParams(
            dimension_semantics=("arbitrary", "arbitrary"), vmem_limit_bytes=VMEM_LIMIT),
    )(h, h, pre_g, wg, wu, wd, post_g)


_OFF_AK = ATT_Q
_OFF_MQ = _OFF_AK + 2 * ATT_KV
_OFF_MV = _OFF_MQ + 2 * M_QK
_OFF_MO = _OFF_MV + M_V
W_IN_MAIN = _OFF_MO + M_V
_PROJ_SEGMENTS = (
    (0, 0, 0, ATT_Q),
    (0, ATT_Q, _OFF_MV, M_V),
    (0, ATT_Q + M_V, _OFF_AK, 2 * ATT_KV),
    (1, 0, _OFF_MQ, 2 * M_QK),
    (1, 2 * M_QK, _OFF_MO, M_V),
)


def _inproj_kernel(h_ref, g_ref, w_ref, wgate_ref, zb_ref, zf_ref, gt_ref, xn_ref):
    xn_ref[...] = _rms(h_ref[...], g_ref[...]).astype(BF16)
    xn = xn_ref[...]
    outs = (zb_ref, zf_ref)
    for which, out0, w0, width in _PROJ_SEGMENTS:
        for c in range(0, width, PROJ_TN):
            z = jnp.dot(xn, w_ref[:, w0 + c:w0 + c + PROJ_TN], preferred_element_type=F32)
            outs[which][:, out0 + c:out0 + c + PROJ_TN] = z.astype(outs[which].dtype)
    zg = jnp.dot(xn, wgate_ref[...], preferred_element_type=F32)
    zf_ref[:, 2 * M_QK + M_V:] = zg
    gt_ref[...] = zg.T[0:GATE_ROWS, :]


def _inproj(h, g, w_main, w_gate):
    n = h.shape[0]
    tm = PROJ_TM
    return pl.pallas_call(
        _inproj_kernel,
        name="inproj",
        grid=(n // tm,),
        in_specs=[
            pl.BlockSpec((tm, D_MODEL), lambda i: (i, 0)),
            pl.BlockSpec((1, D_MODEL), lambda i: (0, 0)),
            pl.BlockSpec((D_MODEL, W_IN_MAIN), lambda i: (0, 0), pipeline_mode=pl.Buffered(1)),
            pl.BlockSpec((D_MODEL, GATE_PAD), lambda i: (0, 0), pipeline_mode=pl.Buffered(1)),
        ],
        out_specs=[
            pl.BlockSpec((tm, ZB_COLS), lambda i: (i, 0)),
            pl.BlockSpec((tm, ZF_COLS), lambda i: (i, 0)),
            pl.BlockSpec((GATE_ROWS, tm), lambda i: (0, i)),
        ],
        out_shape=[jax.ShapeDtypeStruct((n, ZB_COLS), BF16),
                   jax.ShapeDtypeStruct((n, ZF_COLS), F32),
                   jax.ShapeDtypeStruct((GATE_ROWS, n), F32)],
        scratch_shapes=[pltpu.VMEM((tm, D_MODEL), BF16)],
        compiler_params=pltpu.CompilerParams(
            dimension_semantics=("parallel",), vmem_limit_bytes=VMEM_LIMIT),
    )(h, g, w_main, w_gate)


def _attn_kernel(scal_ref, q_ref, kp_ref, kc_ref, vp_ref, vc_ref, o_ref, bias_ref):
    t = pl.program_id(1)
    blk = ATT_BLOCK
    half = ATT_HEAD_DIM
    scale = 1.0 / math.sqrt(ATT_HEAD_DIM)
    neg_inf = jnp.float32(-jnp.inf)

    qi = lax.broadcasted_iota(jnp.int32, (blk, 2 * blk), 0)
    kj = lax.broadcasted_iota(jnp.int32, (blk, 2 * blk), 1)

    @pl.when(t == 0)
    def _():
        dist = qi - kj + blk
        band = (dist >= 0) & (dist < WINDOW)
        distf = dist.astype(F32)
        for h in range(ATT_HEADS):
            bias_ref[h] = jnp.where(band, -(scal_ref[1, h] * distf), neg_inf)

    kk = jnp.concatenate([kp_ref[...], kc_ref[...]], axis=0).astype(F32) * scale
    vv = jnp.concatenate([vp_ref[...], vc_ref[...]], axis=0).astype(F32)
    rows = kk.shape[0]
    lane = lax.broadcasted_iota(jnp.int32, (rows, LANES), 1)
    low = lane < half

    def split_pair(x):
        xr = pltpu.roll(x, half, axis=1)
        zero = jnp.zeros_like(x)
        first = (jnp.where(low, x, zero).astype(BF16), jnp.where(low, zero, xr).astype(BF16))
        second = (jnp.where(low, xr, zero).astype(BF16), jnp.where(low, zero, x).astype(BF16))
        return first, second

    k_lo_hi, v_lo_hi = [], []
    for pair in range(ATT_KV // LANES):
        ka, kb = split_pair(kk[:, pair * LANES:(pair + 1) * LANES])
        va, vb = split_pair(vv[:, pair * LANES:(pair + 1) * LANES])
        k_lo_hi += [ka, kb]
        v_lo_hi += [va, vb]

    lane_q = lax.broadcasted_iota(jnp.int32, (blk, LANES), 1)
    has_prev = (kj >= blk) | (t > 0)

    low_q = lane_q < half
    pairs = [(g, pr) for g in range(ATT_KV_HEADS) for pr in range(ATT_GROUP // 2)]
    for j in range(ATT_TQ // blk):
        r0 = j * blk
        k2 = [jnp.concatenate([lo[r0:r0 + 2 * blk], hi[r0:r0 + 2 * blk]], axis=0)
              for lo, hi in k_lo_hi]
        v2 = [jnp.concatenate([lo[r0:r0 + 2 * blk], hi[r0:r0 + 2 * blk]], axis=0)
              for lo, hi in v_lo_hi]
        s2 = []
        for g, pr in pairs:
            c0 = (g * ATT_GROUP + 2 * pr) * half
            s2.append(lax.dot_general(q_ref[r0:r0 + blk, c0:c0 + LANES], k2[g],
                                      (((1,), (1,)), ((), ())), preferred_element_type=F32))
        p2, inv2 = [], []
        for idx, (g, pr) in enumerate(pairs):
            h0 = g * ATT_GROUP + 2 * pr
            ps, ms, sums = [], [], []
            for e in range(2):
                s = s2[idx][:, e * 2 * blk:(e + 1) * 2 * blk] + bias_ref[h0 + e]
                if j == 0:
                    s = jnp.where(has_prev, s, neg_inf)
                m = jnp.maximum(jnp.max(s, axis=-1, keepdims=True), scal_ref[0, h0 + e])
                p = jnp.exp(s - m)
                ps.append(p.astype(BF16))
                ms.append(scal_ref[0, h0 + e] - m)
                sums.append(jnp.sum(p, axis=-1, keepdims=True))
            den = jnp.where(low_q, sums[0], sums[1]) + jnp.exp(jnp.where(low_q, ms[0], ms[1]))
            inv2.append(1.0 / den)
            p2.append(jnp.concatenate(ps, axis=1))
        for idx, (g, pr) in enumerate(pairs):
            c0 = (g * ATT_GROUP + 2 * pr) * half
            o2 = jnp.dot(p2[idx], v2[g], preferred_element_type=F32) * inv2[idx]
            o_ref[r0:r0 + blk, c0:c0 + LANES] = o2.astype(BF16)


def _attention(zb, scal, batch, seq):
    n = zb.shape[0]
    tq = ATT_TQ
    nt = seq // tq
    per = tq // ATT_BLOCK
    kcol = (ATT_Q + M_V) // ATT_KV
    vcol = kcol + 1

    def prev_map(col):
        return lambda b, t: (b * (seq // ATT_BLOCK) + jnp.maximum(t * per - 1, 0), col)

    def cur_map(col):
        return lambda b, t: (b * nt + t, col)

    return pl.pallas_call(
        _attn_kernel,
        name="attn",
        grid=(batch, nt),
        in_specs=[
            pl.BlockSpec(memory_space=pltpu.SMEM),
            pl.BlockSpec((tq, ATT_Q), cur_map(0)),
            pl.BlockSpec((ATT_BLOCK, ATT_KV), prev_map(kcol)),
            pl.BlockSpec((tq, ATT_KV), cur_map(kcol)),
            pl.BlockSpec((ATT_BLOCK, ATT_KV), prev_map(vcol)),
            pl.BlockSpec((tq, ATT_KV), cur_map(vcol)),
        ],
        out_specs=pl.BlockSpec((tq, ATT_Q), cur_map(0)),
        out_shape=jax.ShapeDtypeStruct((n, ATT_Q), BF16),
        scratch_shapes=[pltpu.VMEM((ATT_HEADS, ATT_BLOCK, 2 * ATT_BLOCK), F32)],
        compiler_params=pltpu.CompilerParams(
            dimension_semantics=("parallel", "arbitrary"), vmem_limit_bytes=VMEM_LIMIT),
    )(scal, zb, zb, zb, zb, zb)


def _mlstm_kernel(qk_ref, v_ref, og_ref, gt_ref, gtt_ref, convw_ref, convb_ref, gbias_ref,
                  gbias_col_ref, headg_ref, o_ref, xbuf_ref, c_ref, n_ref, m_ref):
    L = M_CHUNK
    pad = 8

    @pl.when(pl.program_id(1) == 0)
    def _():
        xbuf_ref[0:pad, :] = jnp.zeros((pad, 2 * M_QK), F32)
        c_ref[...] = jnp.zeros_like(c_ref)
        n_ref[...] = jnp.zeros_like(n_ref)
        m_ref[...] = jnp.zeros_like(m_ref)

    xbuf_ref[pad:pad + L, :] = qk_ref[...]
    w = convw_ref[...]
    base = pad - (CONV_WIDTH - 1)
    y = xbuf_ref[pad:pad + L, :] * w[CONV_WIDTH - 1:CONV_WIDTH, :] + convb_ref[...]
    for j in range(CONV_WIDTH - 1):
        y = y + xbuf_ref[pl.ds(base + j, L), :] * w[j:j + 1, :]
    xbuf_ref[0:pad, :] = xbuf_ref[L:L + pad, :]
    qk = y * jax.nn.sigmoid(y)

    gl = gt_ref[...] + gbias_ref[...]
    lf = _log_sigmoid(gl)
    gl_t = gtt_ref[...] + gbias_col_ref[...]
    lf_t = _log_sigmoid(gl_t)

    ti = lax.broadcasted_iota(jnp.int32, (L, L), 0)
    si = lax.broadcasted_iota(jnp.int32, (L, L), 1)
    tril = si <= ti
    triu = ti <= si
    neg_inf = jnp.float32(-jnp.inf)
    kscale = 1.0 / math.sqrt(M_QK_DIM)

    heads = range(M_HEADS)
    q_f = [qk[:, h * M_QK_DIM:(h + 1) * M_QK_DIM] for h in heads]
    k_f = [qk[:, M_QK + h * M_QK_DIM:M_QK + (h + 1) * M_QK_DIM] * kscale for h in heads]
    v_b = [v_ref[:, h * M_V_DIM:(h + 1) * M_V_DIM] for h in heads]
    q_b = [q.astype(BF16) for q in q_f]
    k_b = [k.astype(BF16) for k in k_f]
    m_in = [m_ref[h, 0:1, 0:1] for h in heads]
    c_in = [c_ref[h] for h in heads]
    n_in = [n_ref[h] for h in heads]

    b_col, dmat, m_t, inter_w, pexp, kwt, decay, m_new = [], [], [], [], [], [], [], []
    for h in heads:
        li_col = gl[:, h:h + 1]
        lf_col = lf[:, M_HEADS + h:M_HEADS + h + 1]
        li_row = gl_t[h:h + 1, :]
        lf_row = lf_t[M_HEADS + h:M_HEADS + h + 1, :]
        bc = jnp.sum(jnp.where(tril, lf_row, 0.0), axis=1, keepdims=True)
        br = jnp.sum(jnp.where(triu, lf_col, 0.0), axis=0, keepdims=True)
        b_last = bc[L - 1:L, :]
        dm = jnp.where(tril, bc - br + li_row, neg_inf)
        a_col = bc + m_in[h]
        mt = jnp.maximum(a_col, jnp.max(dm, axis=1, keepdims=True))
        inter_w.append(jnp.exp(a_col - mt))
        pexp.append(jnp.exp(dm - mt))
        m_t.append(mt)
        g_col = b_last - bc + li_col
        mn = jnp.maximum(b_last + m_in[h], jnp.max(g_col, axis=0, keepdims=True))
        decay.append(jnp.exp(b_last + m_in[h] - mn))
        kwt.append(jnp.exp(g_col - mn))
        m_new.append(mn)

    p = [lax.dot_general(q_b[h], k_b[h], (((1,), (1,)), ((), ())), preferred_element_type=F32)
         * pexp[h] for h in heads]
    hh = []
    for h in heads:
        num = (jnp.dot(p[h].astype(BF16), v_b[h], preferred_element_type=F32)
               + inter_w[h] * jnp.dot(q_b[h], c_in[h].astype(BF16), preferred_element_type=F32))
        den = (jnp.sum(p[h], axis=1, keepdims=True)
               + inter_w[h] * jnp.sum(q_f[h] * n_in[h], axis=1, keepdims=True))
        hh.append(num / jnp.maximum(jnp.abs(den), jnp.exp(-m_t[h])))
    for h in heads:
        kw = k_f[h] * kwt[h]
        c_ref[h] = decay[h] * c_in[h] + lax.dot_general(
            kw.astype(BF16), v_b[h], (((0,), (0,)), ((), ())), preferred_element_type=F32)
        n_ref[h] = decay[h] * n_in[h] + jnp.sum(kw, axis=0, keepdims=True)
        m_ref[h] = jnp.broadcast_to(m_new[h], m_ref.shape[1:])
    for h in heads:
        cs = slice(h * M_V_DIM, (h + 1) * M_V_DIM)
        hn = _rms(hh[h], headg_ref[:, cs])
        o_ref[:, cs] = (jax.nn.sigmoid(og_ref[:, cs]) * hn).astype(BF16)


def _mlstm(zb, zf, gt, conv_w, conv_b, gbias, head_g, batch, seq):
    n = zb.shape[0]
    L = M_CHUNK
    nc = seq // L
    row = lambda b, c: b * nc + c
    return pl.pallas_call(
        _mlstm_kernel,
        name="mlstm",
        grid=(batch, nc),
        in_specs=[
            pl.BlockSpec((L, 2 * M_QK), lambda b, c: (row(b, c), 0)),
            pl.BlockSpec((L, M_V), lambda b, c: (row(b, c), ATT_Q // M_V)),
            pl.BlockSpec((L, M_V), lambda b, c: (row(b, c), 2 * M_QK // M_V)),
            pl.BlockSpec((L, GATE_PAD), lambda b, c: (row(b, c), (2 * M_QK + M_V) // GATE_PAD)),
            pl.BlockSpec((GATE_ROWS, L), lambda b, c: (0, row(b, c))),
            pl.BlockSpec((CONV_WIDTH, 2 * M_QK), lambda b, c: (0, 0)),
            pl.BlockSpec((1, 2 * M_QK), lambda b, c: (0, 0)),
            pl.BlockSpec((1, GATE_PAD), lambda b, c: (0, 0)),
            pl.BlockSpec((GATE_ROWS, 1), lambda b, c: (0, 0)),
            pl.BlockSpec((1, M_V), lambda b, c: (0, 0)),
        ],
        out_specs=pl.BlockSpec((L, M_V), lambda b, c: (row(b, c), 0)),
        out_shape=jax.ShapeDtypeStruct((n, M_V), BF16),
        scratch_shapes=[
            pltpu.VMEM((L + 8, 2 * M_QK), F32),
            pltpu.VMEM((M_HEADS, M_QK_DIM, M_V_DIM), F32),
            pltpu.VMEM((M_HEADS, 1, M_QK_DIM), F32),
            pltpu.VMEM((M_HEADS, 8, LANES), F32),
        ],
        compiler_params=pltpu.CompilerParams(
            dimension_semantics=("parallel", "arbitrary"), vmem_limit_bytes=VMEM_LIMIT),
    )(zf, zb, zf, zf, gt, conv_w, conv_b, gbias, gbias[0, :GATE_ROWS, None], head_g)


def _merge_kernel(ha_ref, hb_ref, att_ref, hm_ref, pre_g_ref, wga_ref, wgm_ref, wa_ref, wm_ref, wo_ref,
                  post_g_ref, o_ref, u0, u1, acc0, acc1):
    j = pl.program_id(1)

    n_slices = MERGE_TM // SLICE_ROWS

    def body(u_next, acc_prev, u_cur, acc_cur, do_main):
        _pre_norm_slice(n_slices, ha_ref, pre_g_ref, u_next)
        _post_norm_slice(n_slices, 1.0, hb_ref, post_g_ref, o_ref, acc_prev)
        if do_main:
            u = u_cur[...]
            ga = jnp.dot(u, wga_ref[...], preferred_element_type=F32)
            gm = jnp.dot(u, wgm_ref[...], preferred_element_type=F32)
            a = jnp.dot(att_ref[...], wa_ref[...], preferred_element_type=F32)
            m = jnp.dot(hm_ref[...], wm_ref[...], preferred_element_type=F32)
            y = (jax.nn.sigmoid(ga) * a + jax.nn.sigmoid(gm) * m).astype(BF16)
            acc_cur[...] = (jnp.where(j > 0, acc_cur[...], 0.0)
                            + jnp.dot(y, wo_ref[...], preferred_element_type=F32))

    _for_row_roles((u0, u1), (acc0, acc1), body)


def _merge(h, att, hm, pre_g, wga, wgm, wa, wm, wo, post_g):
    n = h.shape[0]
    tm, tn = MERGE_TM, MERGE_TN
    n_tiles, n_steps, n_slices = n // tm, D_MODEL // tn, tm // SLICE_ROWS
    assert n_steps >= n_slices
    lead, lag, step, tile = _pipeline_maps(n_tiles, n_slices, n_steps)
    return pl.pallas_call(
        _merge_kernel,
        name="merge",
        grid=(n_tiles + 2, n_steps),
        in_specs=[
            pl.BlockSpec((SLICE_ROWS, D_MODEL), lead),
            pl.BlockSpec((SLICE_ROWS, D_MODEL), lag),
            pl.BlockSpec((tm, ATT_Q), tile),
            pl.BlockSpec((tm, M_V), tile),
            pl.BlockSpec((1, D_MODEL), lambda r, j: (0, 0)),
            pl.BlockSpec((D_MODEL, tn), lambda r, j: (0, step(r, j))),
            pl.BlockSpec((D_MODEL, tn), lambda r, j: (0, step(r, j))),
            pl.BlockSpec((ATT_Q, tn), lambda r, j: (0, step(r, j))),
            pl.BlockSpec((M_V, tn), lambda r, j: (0, step(r, j))),
            pl.BlockSpec((tn, D_MODEL), lambda r, j: (step(r, j), 0)),
            pl.BlockSpec((1, D_MODEL), lambda r, j: (0, 0)),
        ],
        out_specs=pl.BlockSpec((SLICE_ROWS, D_MODEL), lag),
        out_shape=jax.ShapeDtypeStruct((n, D_MODEL), F32),
        scratch_shapes=[pltpu.VMEM((tm, D_MODEL), BF16), pltpu.VMEM((tm, D_MODEL), BF16),
                        pltpu.VMEM((tm, D_MODEL), F32), pltpu.VMEM((tm, D_MODEL), F32)],
        compiler_params=pltpu.CompilerParams(
            dimension_semantics=("arbitrary", "arbitrary"), vmem_limit_bytes=VMEM_LIMIT),
    )(h, h, att, hm, pre_g, wga, wgm, wa, wm, wo, post_g)


def kernel(x, ffn1_pre_g, ffn1_wg, ffn1_wu, ffn1_wd, ffn1_post_g, mix_pre_g, w_in, conv_w, conv_b,
           attn_sinks, m_igate_b, m_fgate_b, m_head_g, w_attn_up, w_mlstm_up, w_out, mix_post_g,
           ffn2_pre_g, ffn2_wg, ffn2_wu, ffn2_wd, ffn2_post_g):
    batch, seq, _ = x.shape
    h = x.reshape(batch * seq, D_MODEL)
    for l in range(ffn1_wg.shape[0]):
        bf = lambda w: w.astype(BF16)
        gate0 = W_IN_MAIN + GATE_ROWS
        w_main = bf(w_in[l][:, :W_IN_MAIN])
        w_gate = bf(jnp.pad(w_in[l][:, W_IN_MAIN:gate0], ((0, 0), (0, GATE_PAD - GATE_ROWS))))
        w_ga = w_in[l][:, gate0:gate0 + D_MODEL]
        w_gm = w_in[l][:, gate0 + D_MODEL:gate0 + 2 * D_MODEL]
        gbias = jnp.concatenate(
            [m_igate_b[l], m_fgate_b[l], jnp.zeros((GATE_PAD - 2 * M_HEADS,), F32)])[None, :]
        slopes = jnp.exp2(-8.0 * jnp.arange(1, ATT_HEADS + 1, dtype=F32) / ATT_HEADS)
        scal = jnp.stack([attn_sinks[l].astype(F32), slopes])

        h = _ffn(h, ffn1_pre_g[l][None, :], bf(ffn1_wg[l]), bf(ffn1_wu[l]), bf(ffn1_wd[l]),
                 ffn1_post_g[l][None, :])
        zb, zf, gt = _inproj(h, mix_pre_g[l][None, :], w_main, w_gate)
        att = _attention(zb, scal, batch, seq)
        hm = _mlstm(zb, zf, gt, conv_w[l], conv_b[l][None, :], gbias, m_head_g[l][None, :],
                    batch, seq)
        h = _merge(h, att, hm, mix_pre_g[l][None, :], bf(w_ga), bf(w_gm), bf(w_attn_up[l]),
                   bf(w_mlstm_up[l]), bf(w_out[l]), mix_post_g[l][None, :])
        h = _ffn(h, ffn2_pre_g[l][None, :], bf(ffn2_wg[l]), bf(ffn2_wu[l]), bf(ffn2_wd[l]),
                 ffn2_post_g[l][None, :])
    return h.reshape(batch, seq, D_MODEL)
```

```python
import functools
import math

import jax
import jax.numpy as jnp
from jax import lax
from jax.experimental import pallas as pl
from jax.experimental.pallas import tpu as pltpu

F32 = jnp.float32
BF16 = jnp.bfloat16

D_MODEL = 2048
ATT_HEADS = 16
ATT_KV_HEADS = 4
ATT_HEAD_DIM = 64
ATT_GROUP = ATT_HEADS // ATT_KV_HEADS
WINDOW = 128
ATT_BLOCK = 128
M_HEADS = 4
M_QK_DIM = 128
M_V_DIM = 256
CONV_WIDTH = 4
D_FF = 5632
EPS = 1e-6

ATT_Q = ATT_HEADS * ATT_HEAD_DIM
ATT_KV = ATT_KV_HEADS * ATT_HEAD_DIM
M_QK = M_HEADS * M_QK_DIM
M_V = M_HEADS * M_V_DIM

LANES = 128
GATE_PAD = LANES
GATE_ROWS = 2 * M_HEADS
ZB_COLS = ATT_Q + M_V + 2 * ATT_KV
ZF_COLS = 2 * M_QK + M_V + GATE_PAD

VMEM_LIMIT = 62 * 1024 * 1024

FFN_TM = 1024
FFN_TF = 512
PROJ_TM = 512
PROJ_TN = 512
ATT_TQ = 512
M_CHUNK = 256
MERGE_TM = 512
MERGE_TN = 512


def _rms(x, g):
    return x * lax.rsqrt(jnp.mean(x * x, axis=-1, keepdims=True) + EPS) * g


def _log_sigmoid(x):
    return jnp.minimum(x, 0.0) - jnp.log1p(jnp.exp(-jnp.abs(x)))


SLICE_ROWS = 128


def _slice_index(f, n_slices):
    return jnp.minimum(f, n_slices - 1)


def _slice_rows(n_slices):
    s = _slice_index(pl.program_id(1), n_slices)
    return pl.ds(pl.multiple_of(s * SLICE_ROWS, SLICE_ROWS), SLICE_ROWS)


def _pre_norm_slice(n_slices, xa_ref, pre_g_ref, xn_next):
    xn_next[_slice_rows(n_slices), :] = _rms(xa_ref[...], pre_g_ref[...]).astype(BF16)


def _post_norm_slice(n_slices, res_scale, xb_ref, post_g_ref, o_ref, acc_prev):
    o_ref[...] = xb_ref[...] + res_scale * _rms(acc_prev[_slice_rows(n_slices), :], post_g_ref[...])


def _for_row_roles(xn_bufs, acc_bufs, body):
    r = pl.program_id(0)
    last = pl.num_programs(0) - 1

    @pl.when((r == 0) & (pl.program_id(1) == 0))
    def _():
        for acc in acc_bufs:
            acc[...] = jnp.zeros_like(acc)

    main = (r > 0) & (r < last)
    for p in range(2):
        roles = (xn_bufs[p], acc_bufs[p], xn_bufs[1 - p], acc_bufs[1 - p])
        pl.when(main & (r % 2 == p))(functools.partial(body, *roles, True))
        pl.when(jnp.logical_not(main) & (r % 2 == p))(functools.partial(body, *roles, False))


def _pipeline_maps(n_tiles, n_slices, n_steps):
    def lead(r, f):
        return (jnp.minimum(r, n_tiles - 1) * n_slices + _slice_index(f, n_slices), 0)

    def lag(r, f):
        return (jnp.where(r < 2, 0, (r - 2) * n_slices + _slice_index(f, n_slices)), 0)

    def step(r, f):
        return jnp.where(r == 0, 0, jnp.where(r == n_tiles + 1, n_steps - 1, f))

    def tile(r, f):
        return (jnp.clip(r - 1, 0, n_tiles - 1), 0)

    return lead, lag, step, tile


def _ffn_kernel(xa_ref, xb_ref, pre_g_ref, wg_ref, wu_ref, wd_ref, post_g_ref, o_ref,
                xn0, xn1, acc0, acc1):
    f = pl.program_id(1)

    n_slices = FFN_TM // SLICE_ROWS

    def body(xn_next, acc_prev, xn_cur, acc_cur, do_main):
        _pre_norm_slice(n_slices, xa_ref, pre_g_ref, xn_next)
        _post_norm_slice(n_slices, 0.5, xb_ref, post_g_ref, o_ref, acc_prev)
        if do_main:
            xn = xn_cur[...]
            g = jnp.dot(xn, wg_ref[...].astype(BF16), preferred_element_type=F32)
            u = jnp.dot(xn, wu_ref[...].astype(BF16), preferred_element_type=F32)
            hid = (g * jax.nn.sigmoid(g) * u).astype(BF16)
            acc_cur[...] = (jnp.where(f > 0, acc_cur[...], 0.0)
                            + jnp.dot(hid, wd_ref[...].astype(BF16), preferred_element_type=F32))

    _for_row_roles((xn0, xn1), (acc0, acc1), body)


def _ffn(h, pre_g, wg, wu, wd, post_g):
    n = h.shape[0]
    tm, tf = FFN_TM, FFN_TF
    n_tiles, n_steps, n_slices = n // tm, D_FF // tf, tm // SLICE_ROWS
    assert n_steps >= n_slices
    lead, lag, step, _ = _pipeline_maps(n_tiles, n_slices, n_steps)
    return pl.pallas_call(
        _ffn_kernel,
        name="ffn",
        grid=(n_tiles + 2, n_steps),
        in_specs=[
            pl.BlockSpec((SLICE_ROWS, D_MODEL), lead),
            pl.BlockSpec((SLICE_ROWS, D_MODEL), lag),
            pl.BlockSpec((1, D_MODEL), lambda r, f: (0, 0)),
            pl.BlockSpec((D_MODEL, tf), lambda r, f: (0, step(r, f))),
            pl.BlockSpec((D_MODEL, tf), lambda r, f: (0, step(r, f))),
            pl.BlockSpec((tf, D_MODEL), lambda r, f: (step(r, f), 0)),
            pl.BlockSpec((1, D_MODEL), lambda r, f: (0, 0)),
        ],
        out_specs=pl.BlockSpec((SLICE_ROWS, D_MODEL), lag),
        out_shape=jax.ShapeDtypeStruct((n, D_MODEL), F32),
        scratch_shapes=[pltpu.VMEM((tm, D_MODEL), BF16), pltpu.VMEM((tm, D_MODEL), BF16),
                        pltpu.VMEM((tm, D_MODEL), F32), pltpu.VMEM((tm, D_MODEL), F32)],
        compiler_params=pltpu.CompilerParams(
            dimension_semantics=("arbitrary", "arbitrary"), vmem_limit_bytes=VMEM_LIMIT),
    )(h, h, pre_g, wg, wu, wd, post_g)


_OFF_AK = ATT_Q
_OFF_MQ = _OFF_AK + 2 * ATT_KV
_OFF_MV = _OFF_MQ + 2 * M_QK
_OFF_MO = _OFF_MV + M_V
W_IN_MAIN = _OFF_MO + M_V
_PROJ_SEGMENTS = (
    (0, 0, 0, ATT_Q),
    (0, ATT_Q, _OFF_MV, M_V),
    (0, ATT_Q + M_V, _OFF_AK, 2 * ATT_KV),
    (1, 0, _OFF_MQ, 2 * M_QK),
    (1, 2 * M_QK, _OFF_MO, M_V),
)


def _inproj_kernel(h_ref, g_ref, w_ref, wgate_ref, zb_ref, zf_ref, gt_ref, xn_ref):
    xn_ref[...] = _rms(h_ref[...], g_ref[...]).astype(BF16)
    xn = xn_ref[...]
    outs = (zb_ref, zf_ref)
    for which, out0, w0, width in _PROJ_SEGMENTS:
        for c in range(0, width, PROJ_TN):
            z = jnp.dot(xn, w_ref[:, w0 + c:w0 + c + PROJ_TN], preferred_element_type=F32)
            outs[which][:, out0 + c:out0 + c + PROJ_TN] = z.astype(outs[which].dtype)
    zg = jnp.dot(xn, wgate_ref[...], preferred_element_type=F32)
    zf_ref[:, 2 * M_QK + M_V:] = zg
    gt_ref[...] = zg.T[0:GATE_ROWS, :]


def _inproj(h, g, w_main, w_gate):
    n = h.shape[0]
    tm = PROJ_TM
    return pl.pallas_call(
        _inproj_kernel,
        name="inproj",
        grid=(n // tm,),
        in_specs=[
            pl.BlockSpec((tm, D_MODEL), lambda i: (i, 0)),
            pl.BlockSpec((1, D_MODEL), lambda i: (0, 0)),
            pl.BlockSpec((D_MODEL, W_IN_MAIN), lambda i: (0, 0), pipeline_mode=pl.Buffered(1)),
            pl.BlockSpec((D_MODEL, GATE_PAD), lambda i: (0, 0), pipeline_mode=pl.Buffered(1)),
        ],
        out_specs=[
            pl.BlockSpec((tm, ZB_COLS), lambda i: (i, 0)),
            pl.BlockSpec((tm, ZF_COLS), lambda i: (i, 0)),
            pl.BlockSpec((GATE_ROWS, tm), lambda i: (0, i)),
        ],
        out_shape=[jax.ShapeDtypeStruct((n, ZB_COLS), BF16),
                   jax.ShapeDtypeStruct((n, ZF_COLS), F32),
                   jax.ShapeDtypeStruct((GATE_ROWS, n), F32)],
        scratch_shapes=[pltpu.VMEM((tm, D_MODEL), BF16)],
        compiler_params=pltpu.CompilerParams(
            dimension_semantics=("parallel",), vmem_limit_bytes=VMEM_LIMIT),
    )(h, g, w_main, w_gate)


def _attn_kernel(scal_ref, q_ref, kp_ref, kc_ref, vp_ref, vc_ref, o_ref, bias_ref):
    t = pl.program_id(1)
    blk = ATT_BLOCK
    half = ATT_HEAD_DIM
    scale = 1.0 / math.sqrt(ATT_HEAD_DIM)
    neg_inf = jnp.float32(-jnp.inf)

    qi = lax.broadcasted_iota(jnp.int32, (blk, 2 * blk), 0)
    kj = lax.broadcasted_iota(jnp.int32, (blk, 2 * blk), 1)

    @pl.when(t == 0)
    def _():
        dist = qi - kj + blk
        band = (dist >= 0) & (dist < WINDOW)
        distf = dist.astype(F32)
        for h in range(ATT_HEADS):
            bias_ref[h] = jnp.where(band, -(scal_ref[1, h] * distf), neg_inf)

    kk = jnp.concatenate([kp_ref[...], kc_ref[...]], axis=0).astype(F32) * scale
    vv = jnp.concatenate([vp_ref[...], vc_ref[...]], axis=0).astype(F32)
    rows = kk.shape[0]
    lane = lax.broadcasted_iota(jnp.int32, (rows, LANES), 1)
    low = lane < half

    def split_pair(x):
        xr = pltpu.roll(x, half, axis=1)
        zero = jnp.zeros_like(x)
        first = (jnp.where(low, x, zero).astype(BF16), jnp.where(low, zero, xr).astype(BF16))
        second = (jnp.where(low, xr, zero).astype(BF16), jnp.where(low, zero, x).astype(BF16))
        return first, second

    k_lo_hi, v_lo_hi = [], []
    for pair in range(ATT_KV // LANES):
        ka, kb = split_pair(kk[:, pair * LANES:(pair + 1) * LANES])
        va, vb = split_pair(vv[:, pair * LANES:(pair + 1) * LANES])
        k_lo_hi += [ka, kb]
        v_lo_hi += [va, vb]

    lane_q = lax.broadcasted_iota(jnp.int32, (blk, LANES), 1)
    has_prev = (kj >= blk) | (t > 0)

    low_q = lane_q < half
    pairs = [(g, pr) for g in range(ATT_KV_HEADS) for pr in range(ATT_GROUP // 2)]
    for j in range(ATT_TQ // blk):
        r0 = j * blk
        k2 = [jnp.concatenate([lo[r0:r0 + 2 * blk], hi[r0:r0 + 2 * blk]], axis=0)
              for lo, hi in k_lo_hi]
        v2 = [jnp.concatenate([lo[r0:r0 + 2 * blk], hi[r0:r0 + 2 * blk]], axis=0)
              for lo, hi in v_lo_hi]
        s2 = []
        for g, pr in pairs:
            c0 = (g * ATT_GROUP + 2 * pr) * half
            s2.append(lax.dot_general(q_ref[r0:r0 + blk, c0:c0 + LANES], k2[g],
                                      (((1,), (1,)), ((), ())), preferred_element_type=F32))
        p2, inv2 = [], []
        for idx, (g, pr) in enumerate(pairs):
            h0 = g * ATT_GROUP + 2 * pr
            ps, ms, sums = [], [], []
            for e in range(2):
                s = s2[idx][:, e * 2 * blk:(e + 1) * 2 * blk] + bias_ref[h0 + e]
                if j == 0:
                    s = jnp.where(has_prev, s, neg_inf)
                m = jnp.maximum(jnp.max(s, axis=-1, keepdims=True), scal_ref[0, h0 + e])
                p = jnp.exp(s - m)
                ps.append(p.astype(BF16))
                ms.append(scal_ref[0, h0 + e] - m)
                sums.append(jnp.sum(p, axis=-1, keepdims=True))
            den = jnp.where(low_q, sums[0], sums[1]) + jnp.exp(jnp.where(low_q, ms[0], ms[1]))
            inv2.append(1.0 / den)
            p2.append(jnp.concatenate(ps, axis=1))
        for idx, (g, pr) in enumerate(pairs):
            c0 = (g * ATT_GROUP + 2 * pr) * half
            o2 = jnp.dot(p2[idx], v2[g], preferred_element_type=F32) * inv2[idx]
            o_ref[r0:r0 + blk, c0:c0 + LANES] = o2.astype(BF16)


def _attention(zb, scal, batch, seq):
    n = zb.shape[0]
    tq = ATT_TQ
    nt = seq // tq
    per = tq // ATT_BLOCK
    kcol = (ATT_Q + M_V) // ATT_KV
    vcol = kcol + 1

    def prev_map(col):
        return lambda b, t: (b * (seq // ATT_BLOCK) + jnp.maximum(t * per - 1, 0), col)

    def cur_map(col):
        return lambda b, t: (b * nt + t, col)

    return pl.pallas_call(
        _attn_kernel,
        name="attn",
        grid=(batch, nt),
        in_specs=[
            pl.BlockSpec(memory_space=pltpu.SMEM),
            pl.BlockSpec((tq, ATT_Q), cur_map(0)),
            pl.BlockSpec((ATT_BLOCK, ATT_KV), prev_map(kcol)),
            pl.BlockSpec((tq, ATT_KV), cur_map(kcol)),
            pl.BlockSpec((ATT_BLOCK, ATT_KV), prev_map(vcol)),
            pl.BlockSpec((tq, ATT_KV), cur_map(vcol)),
        ],
        out_specs=pl.BlockSpec((tq, ATT_Q), cur_map(0)),
        out_shape=jax.ShapeDtypeStruct((n, ATT_Q), BF16),
        scratch_shapes=[pltpu.VMEM((ATT_HEADS, ATT_BLOCK, 2 * ATT_BLOCK), F32)],
        compiler_params=pltpu.CompilerParams(
            dimension_semantics=("parallel", "arbitrary"), vmem_limit_bytes=VMEM_LIMIT),
    )(scal, zb, zb, zb, zb, zb)


def _mlstm_kernel(qk_ref, v_ref, og_ref, gt_ref, gtt_ref, convw_ref, convb_ref, gbias_ref,
                  gbias_col_ref, headg_ref, o_ref, xbuf_ref, c_ref, n_ref, m_ref):
    L = M_CHUNK
    pad = 8

    @pl.when(pl.program_id(1) == 0)
    def _():
        xbuf_ref[0:pad, :] = jnp.zeros((pad, 2 * M_QK), F32)
        c_ref[...] = jnp.zeros_like(c_ref)
        n_ref[...] = jnp.zeros_like(n_ref)
        m_ref[...] = jnp.zeros_like(m_ref)

    xbuf_ref[pad:pad + L, :] = qk_ref[...]
    w = convw_ref[...]
    base = pad - (CONV_WIDTH - 1)
    y = xbuf_ref[pad:pad + L, :] * w[CONV_WIDTH - 1:CONV_WIDTH, :] + convb_ref[...]
    for j in range(CONV_WIDTH - 1):
        y = y + xbuf_ref[pl.ds(base + j, L), :] * w[j:j + 1, :]
    xbuf_ref[0:pad, :] = xbuf_ref[L:L + pad, :]
    qk = y * jax.nn.sigmoid(y)

    gl = gt_ref[...] + gbias_ref[...]
    lf = _log_sigmoid(gl)
    gl_t = gtt_ref[...] + gbias_col_ref[...]
    lf_t = _log_sigmoid(gl_t)

    ti = lax.broadcasted_iota(jnp.int32, (L, L), 0)
    si = lax.broadcasted_iota(jnp.int32, (L, L), 1)
    tril = si <= ti
    triu = ti <= si
    neg_inf = jnp.float32(-jnp.inf)
    kscale = 1.0 / math.sqrt(M_QK_DIM)

    heads = range(M_HEADS)
    q_f = [qk[:, h * M_QK_DIM:(h + 1) * M_QK_DIM] for h in heads]
    k_f = [qk[:, M_QK + h * M_QK_DIM:M_QK + (h + 1) * M_QK_DIM] * kscale for h in heads]
    v_b = [v_ref[:, h * M_V_DIM:(h + 1) * M_V_DIM] for h in heads]
    q_b = [q.astype(BF16) for q in q_f]
    k_b = [k.astype(BF16) for k in k_f]
    m_in = [m_ref[h, 0:1, 0:1] for h in heads]
    c_in = [c_ref[h] for h in heads]
    n_in = [n_ref[h] for h in heads]

    b_col, dmat, m_t, inter_w, pexp, kwt, decay, m_new = [], [], [], [], [], [], [], []
    for h in heads:
        li_col = gl[:, h:h + 1]
        lf_col = lf[:, M_HEADS + h:M_HEADS + h + 1]
        li_row = gl_t[h:h + 1, :]
        lf_row = lf_t[M_HEADS + h:M_HEADS + h + 1, :]
        bc = jnp.sum(jnp.where(tril, lf_row, 0.0), axis=1, keepdims=True)
        br = jnp.sum(jnp.where(triu, lf_col, 0.0), axis=0, keepdims=True)
        b_last = bc[L - 1:L, :]
        dm = jnp.where(tril, bc - br + li_row, neg_inf)
        a_col = bc + m_in[h]
        mt = jnp.maximum(a_col, jnp.max(dm, axis=1, keepdims=True))
        inter_w.append(jnp.exp(a_col - mt))
        pexp.append(jnp.exp(dm - mt))
        m_t.append(mt)
        g_col = b_last - bc + li_col
        mn = jnp.maximum(b_last + m_in[h], jnp.max(g_col, axis=0, keepdims=True))
        decay.append(jnp.exp(b_last + m_in[h] - mn))
        kwt.append(jnp.exp(g_col - mn))
        m_new.append(mn)

    p = [lax.dot_general(q_b[h], k_b[h], (((1,), (1,)), ((), ())), preferred_element_type=F32)
         * pexp[h] for h in heads]
    hh = []
    for h in heads:
        num = (jnp.dot(p[h].astype(BF16), v_b[h], preferred_element_type=F32)
               + inter_w[h] * jnp.dot(q_b[h], c_in[h].astype(BF16), preferred_element_type=F32))
        den = (jnp.sum(p[h], axis=1, keepdims=True)
               + inter_w[h] * jnp.sum(q_f[h] * n_in[h], axis=1, keepdims=True))
        hh.append(num / jnp.maximum(jnp.abs(den), jnp.exp(-m_t[h])))
    for h in heads:
        kw = k_f[h] * kwt[h]
        c_ref[h] = decay[h] * c_in[h] + lax.dot_general(
            kw.astype(BF16), v_b[h], (((0,), (0,)), ((), ())), preferred_element_type=F32)
        n_ref[h] = decay[h] * n_in[h] + jnp.sum(kw, axis=0, keepdims=True)
        m_ref[h] = jnp.broadcast_to(m_new[h], m_ref.shape[1:])
    for h in heads:
        cs = slice(h * M_V_DIM, (h + 1) * M_V_DIM)
        hn = _rms(hh[h], headg_ref[:, cs])
        o_ref[:, cs] = (jax.nn.sigmoid(og_ref[:, cs]) * hn).astype(BF16)


def _mlstm(zb, zf, gt, conv_w, conv_b, gbias, head_g, batch, seq):
    n = zb.shape[0]
    L = M_CHUNK
    nc = seq // L
    row = lambda b, c: b * nc + c
    return pl.pallas_call(
        _mlstm_kernel,
        name="mlstm",
        grid=(batch, nc),
        in_specs=[
            pl.BlockSpec((L, 2 * M_QK), lambda b, c: (row(b, c), 0)),
            pl.BlockSpec((L, M_V), lambda b, c: (row(b, c), ATT_Q // M_V)),
            pl.BlockSpec((L, M_V), lambda b, c: (row(b, c), 2 * M_QK // M_V)),
            pl.BlockSpec((L, GATE_PAD), lambda b, c: (row(b, c), (2 * M_QK + M_V) // GATE_PAD)),
            pl.BlockSpec((GATE_ROWS, L), lambda b, c: (0, row(b, c))),
            pl.BlockSpec((CONV_WIDTH, 2 * M_QK), lambda b, c: (0, 0)),
            pl.BlockSpec((1, 2 * M_QK), lambda b, c: (0, 0)),
            pl.BlockSpec((1, GATE_PAD), lambda b, c: (0, 0)),
            pl.BlockSpec((GATE_ROWS, 1), lambda b, c: (0, 0)),
            pl.BlockSpec((1, M_V), lambda b, c: (0, 0)),
        ],
        out_specs=pl.BlockSpec((L, M_V), lambda b, c: (row(b, c), 0)),
        out_shape=jax.ShapeDtypeStruct((n, M_V), BF16),
        scratch_shapes=[
            pltpu.VMEM((L + 8, 2 * M_QK), F32),
            pltpu.VMEM((M_HEADS, M_QK_DIM, M_V_DIM), F32),
            pltpu.VMEM((M_HEADS, 1, M_QK_DIM), F32),
            pltpu.VMEM((M_HEADS, 8, LANES), F32),
        ],
        compiler_params=pltpu.CompilerParams(
            dimension_semantics=("parallel", "arbitrary"), vmem_limit_bytes=VMEM_LIMIT),
    )(zf, zb, zf, zf, gt, conv_w, conv_b, gbias, gbias[0, :GATE_ROWS, None], head_g)


def _merge_kernel(ha_ref, hb_ref, att_ref, hm_ref, pre_g_ref, wga_ref, wgm_ref, wa_ref, wm_ref, wo_ref,
                  post_g_ref, o_ref, u0, u1, acc0, acc1):
    j = pl.program_id(1)

    n_slices = MERGE_TM // SLICE_ROWS

    def body(u_next, acc_prev, u_cur, acc_cur, do_main):
        _pre_norm_slice(n_slices, ha_ref, pre_g_ref, u_next)
        _post_norm_slice(n_slices, 1.0, hb_ref, post_g_ref, o_ref, acc_prev)
        if do_main:
            u = u_cur[...]
            ga = jnp.dot(u, wga_ref[...], preferred_element_type=F32)
            gm = jnp.dot(u, wgm_ref[...], preferred_element_type=F32)
            a = jnp.dot(att_ref[...], wa_ref[...], preferred_element_type=F32)
            m = jnp.dot(hm_ref[...], wm_ref[...], preferred_element_type=F32)
            y = (jax.nn.sigmoid(ga) * a + jax.nn.sigmoid(gm) * m).astype(BF16)
            acc_cur[...] = (jnp.where(j > 0, acc_cur[...], 0.0)
                            + jnp.dot(y, wo_ref[...], preferred_element_type=F32))

    _for_row_roles((u0, u1), (acc0, acc1), body)


def _merge(h, att, hm, pre_g, wga, wgm, wa, wm, wo, post_g):
    n = h.shape[0]
    tm, tn = MERGE_TM, MERGE_TN
    n_tiles, n_steps, n_slices = n // tm, D_MODEL // tn, tm // SLICE_ROWS
    assert n_steps >= n_slices
    lead, lag, step, tile = _pipeline_maps(n_tiles, n_slices, n_steps)
    return pl.pallas_call(
        _merge_kernel,
        name="merge",
        grid=(n_tiles + 2, n_steps),
        in_specs=[
            pl.BlockSpec((SLICE_ROWS, D_MODEL), lead),
            pl.BlockSpec((SLICE_ROWS, D_MODEL), lag),
            pl.BlockSpec((tm, ATT_Q), tile),
            pl.BlockSpec((tm, M_V), tile),
            pl.BlockSpec((1, D_MODEL), lambda r, j: (0, 0)),
            pl.BlockSpec((D_MODEL, tn), lambda r, j: (0, step(r, j))),
            pl.BlockSpec((D_MODEL, tn), lambda r, j: (0, step(r, j))),
            pl.BlockSpec((ATT_Q, tn), lambda r, j: (0, step(r, j))),
            pl.BlockSpec((M_V, tn), lambda r, j: (0, step(r, j))),
            pl.BlockSpec((tn, D_MODEL), lambda r, j: (step(r, j), 0)),
            pl.BlockSpec((1, D_MODEL), lambda r, j: (0, 0)),
        ],
        out_specs=pl.BlockSpec((SLICE_ROWS, D_MODEL), lag),
        out_shape=jax.ShapeDtypeStruct((n, D_MODEL), F32),
        scratch_shapes=[pltpu.VMEM((tm, D_MODEL), BF16), pltpu.VMEM((tm, D_MODEL), BF16),
                        pltpu.VMEM((tm, D_MODEL), F32), pltpu.VMEM((tm, D_MODEL), F32)],
        compiler_params=pltpu.CompilerParams(
            dimension_semantics=("arbitrary", "arbitrary"), vmem_limit_bytes=VMEM_LIMIT),
    )(h, h, att, hm, pre_g, wga, wgm, wa, wm, wo, post_g)


def kernel(x, ffn1_pre_g, ffn1_wg, ffn1_wu, ffn1_wd, ffn1_post_g, mix_pre_g, w_in, conv_w, conv_b,
           attn_sinks, m_igate_b, m_fgate_b, m_head_g, w_attn_up, w_mlstm_up, w_out, mix_post_g,
           ffn2_pre_g, ffn2_wg, ffn2_wu, ffn2_wd, ffn2_post_g):
    batch, seq, _ = x.shape
    h = x.reshape(batch * seq, D_MODEL)
    for l in range(ffn1_wg.shape[0]):
        bf = lambda w: w.astype(BF16)
        gate0 = W_IN_MAIN + GATE_ROWS
        w_main = bf(w_in[l][:, :W_IN_MAIN])
        w_gate = bf(jnp.pad(w_in[l][:, W_IN_MAIN:gate0], ((0, 0), (0, GATE_PAD - GATE_ROWS))))
        w_tail = bf(w_in[l][:, W_IN_MAIN:])
        w_ga = w_tail[:, GATE_ROWS:GATE_ROWS + D_MODEL]
        w_gm = w_tail[:, GATE_ROWS + D_MODEL:GATE_ROWS + 2 * D_MODEL]
        gbias = jnp.concatenate(
            [m_igate_b[l], m_fgate_b[l], jnp.zeros((GATE_PAD - 2 * M_HEADS,), F32)])[None, :]
        slopes = jnp.exp2(-8.0 * jnp.arange(1, ATT_HEADS + 1, dtype=F32) / ATT_HEADS)
        scal = jnp.stack([attn_sinks[l].astype(F32), slopes])

        h = _ffn(h, ffn1_pre_g[l][None, :], ffn1_wg[l], ffn1_wu[l], ffn1_wd[l],
                 ffn1_post_g[l][None, :])
        zb, zf, gt = _inproj(h, mix_pre_g[l][None, :], w_main, w_gate)
        att = _attention(zb, scal, batch, seq)
        hm = _mlstm(zb, zf, gt, conv_w[l], conv_b[l][None, :], gbias, m_head_g[l][None, :],
                    batch, seq)
        h = _merge(h, att, hm, mix_pre_g[l][None, :], bf(w_ga), bf(w_gm), bf(w_attn_up[l]),
                   bf(w_mlstm_up[l]), bf(w_out[l]), mix_post_g[l][None, :])
        h = _ffn(h, ffn2_pre_g[l][None, :], ffn2_wg[l], ffn2_wu[l], ffn2_wd[l],
                 ffn2_post_g[l][None, :])
    return h.reshape(batch, seq, D_MODEL)
```

```python
import functools
import math

import jax
import jax.numpy as jnp
from jax import lax
from jax.experimental import pallas as pl
from jax.experimental.pallas import tpu as pltpu

F32 = jnp.float32
BF16 = jnp.bfloat16

D_MODEL = 2048
ATT_HEADS = 16
ATT_KV_HEADS = 4
ATT_HEAD_DIM = 64
ATT_GROUP = ATT_HEADS // ATT_KV_HEADS
WINDOW = 128
ATT_BLOCK = 128
M_HEADS = 4
M_QK_DIM = 128
M_V_DIM = 256
CONV_WIDTH = 4
D_FF = 5632
EPS = 1e-6

ATT_Q = ATT_HEADS * ATT_HEAD_DIM
ATT_KV = ATT_KV_HEADS * ATT_HEAD_DIM
M_QK = M_HEADS * M_QK_DIM
M_V = M_HEADS * M_V_DIM

LANES = 128
GATE_PAD = LANES
GATE_ROWS = 2 * M_HEADS
ZB_COLS = ATT_Q + M_V + 2 * ATT_KV
ZF_COLS = 2 * M_QK + M_V + GATE_PAD

VMEM_LIMIT = 62 * 1024 * 1024

FFN_TM = 1024
FFN_TF = 512
PROJ_TM = 512
PROJ_TN = 512
ATT_TQ = 512
M_CHUNK = 256
MERGE_TM = 512
MERGE_TN = 512


def _rms(x, g):
    return x * lax.rsqrt(jnp.mean(x * x, axis=-1, keepdims=True) + EPS) * g


def _log_sigmoid(x):
    return jnp.minimum(x, 0.0) - jnp.log1p(jnp.exp(-jnp.abs(x)))


SLICE_ROWS = 128


def _slice_index(f, n_slices):
    return jnp.minimum(f, n_slices - 1)


def _norm_piece(n_slices, piece, n_pieces, res_scale, xa_ref, xb_ref, pre_g_ref, post_g_ref, o_ref,
                xn_next, acc_prev):
    rows = SLICE_ROWS // n_pieces
    s = _slice_index(pl.program_id(1), n_slices)
    src = pl.ds(piece * rows, rows)
    dst = pl.ds(pl.multiple_of(s * SLICE_ROWS + piece * rows, rows), rows)
    xn = _rms(xa_ref[src, :], pre_g_ref[...])
    xn_next[dst, :] = xn.astype(BF16)
    out = xb_ref[src, :] + res_scale * _rms(acc_prev[dst, :], post_g_ref[...])
    o_ref[src, :] = out
    return xn, out


def _zero_after(*values):
    acc = None
    for v in values:
        bits = pltpu.bitcast(v, jnp.uint32)
        for r0 in range(0, bits.shape[0], 8):
            for c0 in range(0, bits.shape[1], LANES):
                tile = bits[r0:r0 + 8, c0:c0 + LANES]
                acc = tile if acc is None else acc | tile
    return jnp.sum(((acc >> 16) >> 16).astype(F32), keepdims=True)


NORM_PIECES = 4


def _for_row_roles(xn_bufs, acc_bufs, body):
    r = pl.program_id(0)
    last = pl.num_programs(0) - 1

    @pl.when((r == 0) & (pl.program_id(1) == 0))
    def _():
        for acc in acc_bufs:
            acc[...] = jnp.zeros_like(acc)

    main = (r > 0) & (r < last)
    for p in range(2):
        roles = (xn_bufs[p], acc_bufs[p], xn_bufs[1 - p], acc_bufs[1 - p])
        pl.when(main & (r % 2 == p))(functools.partial(body, *roles, True))
        pl.when(jnp.logical_not(main) & (r % 2 == p))(functools.partial(body, *roles, False))


def _pipeline_maps(n_tiles, n_slices, n_steps):
    def lead(r, f):
        return (jnp.minimum(r, n_tiles - 1) * n_slices + _slice_index(f, n_slices), 0)

    def lag(r, f):
        return (jnp.where(r < 2, 0, (r - 2) * n_slices + _slice_index(f, n_slices)), 0)

    def step(r, f):
        return jnp.where(r == 0, 0, jnp.where(r == n_tiles + 1, n_steps - 1, f))

    def tile(r, f):
        return (jnp.clip(r - 1, 0, n_tiles - 1), 0)

    return lead, lag, step, tile


def _ffn_kernel(xa_ref, xb_ref, pre_g_ref, wg_ref, wu_ref, wd_ref, post_g_ref, o_ref,
                xn0, xn1, acc0, acc1):
    f = pl.program_id(1)

    n_slices = FFN_TM // SLICE_ROWS

    def body(xn_next, acc_prev, xn_cur, acc_cur, do_main):
        norm_refs = (xa_ref, xb_ref, pre_g_ref, post_g_ref, o_ref, xn_next, acc_prev)
        if not do_main:
            _norm_piece(n_slices, 0, 1, 0.5, *norm_refs)
            return
        z = [_zero_after(*_norm_piece(n_slices, c, NORM_PIECES, 0.5, *norm_refs))
             for c in range(NORM_PIECES)]
        xn = xn_cur[...]
        half = FFN_TF // 2
        hid = []
        for c in range(2):
            cols = slice(c * half, (c + 1) * half)
            g = jnp.dot(xn, wg_ref[:, cols].astype(BF16), preferred_element_type=F32) + z[2 * c]
            u = jnp.dot(xn, wu_ref[:, cols].astype(BF16), preferred_element_type=F32) + z[2 * c + 1]
            hid.append((g * jax.nn.sigmoid(g) * u).astype(BF16))
        hid = jnp.concatenate(hid, axis=1)
        acc_cur[...] = (jnp.where(f == 0, 0.0, acc_cur[...])
                        + jnp.dot(hid, wd_ref[...].astype(BF16), preferred_element_type=F32))

    _for_row_roles((xn0, xn1), (acc0, acc1), body)


def _ffn(h, pre_g, wg, wu, wd, post_g):
    n = h.shape[0]
    tm, tf = FFN_TM, FFN_TF
    n_tiles, n_steps, n_slices = n // tm, D_FF // tf, tm // SLICE_ROWS
    assert n_steps >= n_slices
    lead, lag, step, _ = _pipeline_maps(n_tiles, n_slices, n_steps)
    return pl.pallas_call(
        _ffn_kernel,
        name="ffn",
        grid=(n_tiles + 2, n_steps),
        in_specs=[
            pl.BlockSpec((SLICE_ROWS, D_MODEL), lead),
            pl.BlockSpec((SLICE_ROWS, D_MODEL), lag),
            pl.BlockSpec((1, D_MODEL), lambda r, f: (0, 0)),
            pl.BlockSpec((D_MODEL, tf), lambda r, f: (0, step(r, f))),
            pl.BlockSpec((D_MODEL, tf), lambda r, f: (0, step(r, f))),
            pl.BlockSpec((tf, D_MODEL), lambda r, f: (step(r, f), 0)),
            pl.BlockSpec((1, D_MODEL), lambda r, f: (0, 0)),
        ],
        out_specs=pl.BlockSpec((SLICE_ROWS, D_MODEL), lag),
        out_shape=jax.ShapeDtypeStruct((n, D_MODEL), F32),
        scratch_shapes=[pltpu.VMEM((tm, D_MODEL), BF16), pltpu.VMEM((tm, D_MODEL), BF16),
                        pltpu.VMEM((tm, D_MODEL), F32), pltpu.VMEM((tm, D_MODEL), F32)],
        compiler_params=pltpu.CompilerParams(
            dimension_semantics=("arbitrary", "arbitrary"), vmem_limit_bytes=VMEM_LIMIT),
    )(h, h, pre_g, wg, wu, wd, post_g)


_OFF_AK = ATT_Q
_OFF_MQ = _OFF_AK + 2 * ATT_KV
_OFF_MV = _OFF_MQ + 2 * M_QK
_OFF_MO = _OFF_MV + M_V
W_IN_MAIN = _OFF_MO + M_V
_PROJ_SEGMENTS = (
    (0, 0, 0, ATT_Q),
    (0, ATT_Q, _OFF_MV, M_V),
    (0, ATT_Q + M_V, _OFF_AK, 2 * ATT_KV),
    (1, 0, _OFF_MQ, 2 * M_QK),
    (1, 2 * M_QK, _OFF_MO, M_V),
)


def _inproj_kernel(h_ref, g_ref, w_ref, wgate_ref, zb_ref, zf_ref, gt_ref, xn_ref):
    xn_ref[...] = _rms(h_ref[...], g_ref[...]).astype(BF16)
    xn = xn_ref[...]
    outs = (zb_ref, zf_ref)
    for which, out0, w0, width in _PROJ_SEGMENTS:
        for c in range(0, width, PROJ_TN):
            z = jnp.dot(xn, w_ref[:, w0 + c:w0 + c + PROJ_TN], preferred_element_type=F32)
            outs[which][:, out0 + c:out0 + c + PROJ_TN] = z.astype(outs[which].dtype)
    zg = jnp.dot(xn, wgate_ref[...], preferred_element_type=F32)
    zf_ref[:, 2 * M_QK + M_V:] = zg
    gt_ref[...] = zg.T[0:GATE_ROWS, :]


def _inproj(h, g, w_all):
    n = h.shape[0]
    tm = PROJ_TM
    return pl.pallas_call(
        _inproj_kernel,
        name="inproj",
        grid=(n // tm,),
        in_specs=[
            pl.BlockSpec((tm, D_MODEL), lambda i: (i, 0)),
            pl.BlockSpec((1, D_MODEL), lambda i: (0, 0)),
            pl.BlockSpec((D_MODEL, W_IN_MAIN), lambda i: (0, 0), pipeline_mode=pl.Buffered(1)),
            pl.BlockSpec((D_MODEL, GATE_PAD), lambda i: (0, W_IN_MAIN // GATE_PAD),
                         pipeline_mode=pl.Buffered(1)),
        ],
        out_specs=[
            pl.BlockSpec((tm, ZB_COLS), lambda i: (i, 0)),
            pl.BlockSpec((tm, ZF_COLS), lambda i: (i, 0)),
            pl.BlockSpec((GATE_ROWS, tm), lambda i: (0, i)),
        ],
        out_shape=[jax.ShapeDtypeStruct((n, ZB_COLS), BF16),
                   jax.ShapeDtypeStruct((n, ZF_COLS), F32),
                   jax.ShapeDtypeStruct((GATE_ROWS, n), F32)],
        scratch_shapes=[pltpu.VMEM((tm, D_MODEL), BF16)],
        compiler_params=pltpu.CompilerParams(
            dimension_semantics=("parallel",), vmem_limit_bytes=VMEM_LIMIT),
    )(h, g, w_all, w_all)


def _attn_kernel(scal_ref, q_ref, kp_ref, kc_ref, vp_ref, vc_ref, o_ref, bias_ref):
    t = pl.program_id(1)
    blk = ATT_BLOCK
    half = ATT_HEAD_DIM
    scale = 1.0 / math.sqrt(ATT_HEAD_DIM)
    neg_inf = jnp.float32(-jnp.inf)

    qi = lax.broadcasted_iota(jnp.int32, (blk, 2 * blk), 0)
    kj = lax.broadcasted_iota(jnp.int32, (blk, 2 * blk), 1)

    @pl.when(t == 0)
    def _():
        dist = qi - kj + blk
        band = (dist >= 0) & (dist < WINDOW)
        distf = dist.astype(F32)
        for h in range(ATT_HEADS):
            bias_ref[h] = jnp.where(band, -(scal_ref[1, h] * distf), neg_inf)

    kk = jnp.concatenate([kp_ref[...], kc_ref[...]], axis=0).astype(F32) * scale
    vv = jnp.concatenate([vp_ref[...], vc_ref[...]], axis=0).astype(F32)
    rows = kk.shape[0]
    lane = lax.broadcasted_iota(jnp.int32, (rows, LANES), 1)
    low = lane < half

    def split_pair(x):
        xr = pltpu.roll(x, half, axis=1)
        zero = jnp.zeros_like(x)
        first = (jnp.where(low, x, zero).astype(BF16), jnp.where(low, zero, xr).astype(BF16))
        second = (jnp.where(low, xr, zero).astype(BF16), jnp.where(low, zero, x).astype(BF16))
        return first, second

    k_lo_hi, v_lo_hi = [], []
    for pair in range(ATT_KV // LANES):
        ka, kb = split_pair(kk[:, pair * LANES:(pair + 1) * LANES])
        va, vb = split_pair(vv[:, pair * LANES:(pair + 1) * LANES])
        k_lo_hi += [ka, kb]
        v_lo_hi += [va, vb]

    lane_q = lax.broadcasted_iota(jnp.int32, (blk, LANES), 1)
    has_prev = (kj >= blk) | (t > 0)

    low_q = lane_q < half
    pairs = [(g, pr) for g in range(ATT_KV_HEADS) for pr in range(ATT_GROUP // 2)]
    for j in range(ATT_TQ // blk):
        r0 = j * blk
        k2 = [jnp.concatenate([lo[r0:r0 + 2 * blk], hi[r0:r0 + 2 * blk]], axis=0)
              for lo, hi in k_lo_hi]
        v2 = [jnp.concatenate([lo[r0:r0 + 2 * blk], hi[r0:r0 + 2 * blk]], axis=0)
              for lo, hi in v_lo_hi]
        s2 = []
        for g, pr in pairs:
            c0 = (g * ATT_GROUP + 2 * pr) * half
            s2.append(lax.dot_general(q_ref[r0:r0 + blk, c0:c0 + LANES], k2[g],
                                      (((1,), (1,)), ((), ())), preferred_element_type=F32))
        p2, inv2 = [], []
        for idx, (g, pr) in enumerate(pairs):
            h0 = g * ATT_GROUP + 2 * pr
            ps, ms, sums = [], [], []
            for e in range(2):
                s = s2[idx][:, e * 2 * blk:(e + 1) * 2 * blk] + bias_ref[h0 + e]
                if j == 0:
                    s = jnp.where(has_prev, s, neg_inf)
                m = jnp.maximum(jnp.max(s, axis=-1, keepdims=True), scal_ref[0, h0 + e])
                p = jnp.exp(s - m)
                ps.append(p.astype(BF16))
                ms.append(scal_ref[0, h0 + e] - m)
                sums.append(jnp.sum(p, axis=-1, keepdims=True))
            den = jnp.where(low_q, sums[0], sums[1]) + jnp.exp(jnp.where(low_q, ms[0], ms[1]))
            inv2.append(1.0 / den)
            p2.append(jnp.concatenate(ps, axis=1))
        for idx, (g, pr) in enumerate(pairs):
            c0 = (g * ATT_GROUP + 2 * pr) * half
            o2 = jnp.dot(p2[idx], v2[g], preferred_element_type=F32) * inv2[idx]
            o_ref[r0:r0 + blk, c0:c0 + LANES] = o2.astype(BF16)


def _attention(zb, scal, batch, seq):
    n = zb.shape[0]
    tq = ATT_TQ
    nt = seq // tq
    per = tq // ATT_BLOCK
    kcol = (ATT_Q + M_V) // ATT_KV
    vcol = kcol + 1

    def prev_map(col):
        return lambda b, t: (b * (seq // ATT_BLOCK) + jnp.maximum(t * per - 1, 0), col)

    def cur_map(col):
        return lambda b, t: (b * nt + t, col)

    return pl.pallas_call(
        _attn_kernel,
        name="attn",
        grid=(batch, nt),
        in_specs=[
            pl.BlockSpec(memory_space=pltpu.SMEM),
            pl.BlockSpec((tq, ATT_Q), cur_map(0)),
            pl.BlockSpec((ATT_BLOCK, ATT_KV), prev_map(kcol)),
            pl.BlockSpec((tq, ATT_KV), cur_map(kcol)),
            pl.BlockSpec((ATT_BLOCK, ATT_KV), prev_map(vcol)),
            pl.BlockSpec((tq, ATT_KV), cur_map(vcol)),
        ],
        out_specs=pl.BlockSpec((tq, ATT_Q), cur_map(0)),
        out_shape=jax.ShapeDtypeStruct((n, ATT_Q), BF16),
        scratch_shapes=[pltpu.VMEM((ATT_HEADS, ATT_BLOCK, 2 * ATT_BLOCK), F32)],
        compiler_params=pltpu.CompilerParams(
            dimension_semantics=("parallel", "arbitrary"), vmem_limit_bytes=VMEM_LIMIT),
    )(scal, zb, zb, zb, zb, zb)


def _mlstm_kernel(qk_ref, v_ref, og_ref, gt_ref, gtt_ref, convw_ref, convb_ref, gbias_ref,
                  gbias_col_ref, headg_ref, o_ref, xbuf_ref, c_ref, n_ref, m_ref):
    L = M_CHUNK
    pad = 8

    @pl.when(pl.program_id(1) == 0)
    def _():
        xbuf_ref[0:pad, :] = jnp.zeros((pad, 2 * M_QK), F32)
        c_ref[...] = jnp.zeros_like(c_ref)
        n_ref[...] = jnp.zeros_like(n_ref)
        m_ref[...] = jnp.zeros_like(m_ref)

    xbuf_ref[pad:pad + L, :] = qk_ref[...]
    w = convw_ref[...]
    base = pad - (CONV_WIDTH - 1)
    y = xbuf_ref[pad:pad + L, :] * w[CONV_WIDTH - 1:CONV_WIDTH, :] + convb_ref[...]
    for j in range(CONV_WIDTH - 1):
        y = y + xbuf_ref[pl.ds(base + j, L), :] * w[j:j + 1, :]
    xbuf_ref[0:pad, :] = xbuf_ref[L:L + pad, :]
    qk = y * jax.nn.sigmoid(y)

    gl = gt_ref[...] + gbias_ref[...]
    lf = _log_sigmoid(gl)
    gl_t = gtt_ref[...] + gbias_col_ref[...]
    lf_t = _log_sigmoid(gl_t)

    ti = lax.broadcasted_iota(jnp.int32, (L, L), 0)
    si = lax.broadcasted_iota(jnp.int32, (L, L), 1)
    tril = si <= ti
    triu = ti <= si
    neg_inf = jnp.float32(-jnp.inf)
    kscale = 1.0 / math.sqrt(M_QK_DIM)

    heads = range(M_HEADS)
    q_f = [qk[:, h * M_QK_DIM:(h + 1) * M_QK_DIM] for h in heads]
    k_f = [qk[:, M_QK + h * M_QK_DIM:M_QK + (h + 1) * M_QK_DIM] * kscale for h in heads]
    v_b = [v_ref[:, h * M_V_DIM:(h + 1) * M_V_DIM] for h in heads]
    q_b = [q.astype(BF16) for q in q_f]
    k_b = [k.astype(BF16) for k in k_f]
    m_in = [m_ref[h, 0:1, 0:1] for h in heads]
    c_in = [c_ref[h] for h in heads]
    n_in = [n_ref[h] for h in heads]

    b_col, dmat, m_t, inter_w, pexp, kwt, decay, m_new = [], [], [], [], [], [], [], []
    for h in heads:
        li_col = gl[:, h:h + 1]
        lf_col = lf[:, M_HEADS + h:M_HEADS + h + 1]
        li_row = gl_t[h:h + 1, :]
        lf_row = lf_t[M_HEADS + h:M_HEADS + h + 1, :]
        bc = jnp.sum(jnp.where(tril, lf_row, 0.0), axis=1, keepdims=True)
        br = jnp.sum(jnp.where(triu, lf_col, 0.0), axis=0, keepdims=True)
        b_last = bc[L - 1:L, :]
        dm = jnp.where(tril, bc - br + li_row, neg_inf)
        a_col = bc + m_in[h]
        mt = jnp.maximum(a_col, jnp.max(dm, axis=1, keepdims=True))
        inter_w.append(jnp.exp(a_col - mt))
        pexp.append(jnp.exp(dm - mt))
        m_t.append(mt)
        g_col = b_last - bc + li_col
        mn = jnp.maximum(b_last + m_in[h], jnp.max(g_col, axis=0, keepdims=True))
        decay.append(jnp.exp(b_last + m_in[h] - mn))
        kwt.append(jnp.exp(g_col - mn))
        m_new.append(mn)

    p = [lax.dot_general(q_b[h], k_b[h], (((1,), (1,)), ((), ())), preferred_element_type=F32)
         * pexp[h] for h in heads]
    hh = []
    for h in heads:
        num = (jnp.dot(p[h].astype(BF16), v_b[h], preferred_element_type=F32)
               + inter_w[h] * jnp.dot(q_b[h], c_in[h].astype(BF16), preferred_element_type=F32))
        den = (jnp.sum(p[h], axis=1, keepdims=True)
               + inter_w[h] * jnp.sum(q_f[h] * n_in[h], axis=1, keepdims=True))
        hh.append(num / jnp.maximum(jnp.abs(den), jnp.exp(-m_t[h])))
    for h in heads:
        kw = k_f[h] * kwt[h]
        c_ref[h] = decay[h] * c_in[h] + lax.dot_general(
            kw.astype(BF16), v_b[h], (((0,), (0,)), ((), ())), preferred_element_type=F32)
        n_ref[h] = decay[h] * n_in[h] + jnp.sum(kw, axis=0, keepdims=True)
        m_ref[h] = jnp.broadcast_to(m_new[h], m_ref.shape[1:])
    for h in heads:
        cs = slice(h * M_V_DIM, (h + 1) * M_V_DIM)
        hn = _rms(hh[h], headg_ref[:, cs])
        o_ref[:, cs] = (jax.nn.sigmoid(og_ref[:, cs]) * hn).astype(BF16)


def _mlstm(zb, zf, gt, conv_w, conv_b, gbias, head_g, batch, seq):
    n = zb.shape[0]
    L = M_CHUNK
    nc = seq // L
    row = lambda b, c: b * nc + c
    return pl.pallas_call(
        _mlstm_kernel,
        name="mlstm",
        grid=(batch, nc),
        in_specs=[
            pl.BlockSpec((L, 2 * M_QK), lambda b, c: (row(b, c), 0)),
            pl.BlockSpec((L, M_V), lambda b, c: (row(b, c), ATT_Q // M_V)),
            pl.BlockSpec((L, M_V), lambda b, c: (row(b, c), 2 * M_QK // M_V)),
            pl.BlockSpec((L, GATE_PAD), lambda b, c: (row(b, c), (2 * M_QK + M_V) // GATE_PAD)),
            pl.BlockSpec((GATE_ROWS, L), lambda b, c: (0, row(b, c))),
            pl.BlockSpec((CONV_WIDTH, 2 * M_QK), lambda b, c: (0, 0)),
            pl.BlockSpec((1, 2 * M_QK), lambda b, c: (0, 0)),
            pl.BlockSpec((1, GATE_PAD), lambda b, c: (0, 0)),
            pl.BlockSpec((GATE_ROWS, 1), lambda b, c: (0, 0)),
            pl.BlockSpec((1, M_V), lambda b, c: (0, 0)),
        ],
        out_specs=pl.BlockSpec((L, M_V), lambda b, c: (row(b, c), 0)),
        out_shape=jax.ShapeDtypeStruct((n, M_V), BF16),
        scratch_shapes=[
            pltpu.VMEM((L + 8, 2 * M_QK), F32),
            pltpu.VMEM((M_HEADS, M_QK_DIM, M_V_DIM), F32),
            pltpu.VMEM((M_HEADS, 1, M_QK_DIM), F32),
            pltpu.VMEM((M_HEADS, 8, LANES), F32),
        ],
        compiler_params=pltpu.CompilerParams(
            dimension_semantics=("parallel", "arbitrary"), vmem_limit_bytes=VMEM_LIMIT),
    )(zf, zb, zf, zf, gt, conv_w, conv_b, gbias, gbias[0, :GATE_ROWS, None], head_g)


def _merge_kernel(ha_ref, hb_ref, att_ref, hm_ref, pre_g_ref, wga_ref, wgm_ref, wa_ref, wm_ref, wo_ref,
                  post_g_ref, o_ref, u0, u1, acc0, acc1):
    j = pl.program_id(1)

    n_slices = MERGE_TM // SLICE_ROWS

    def body(u_next, acc_prev, u_cur, acc_cur, do_main):
        norm_refs = (ha_ref, hb_ref, pre_g_ref, post_g_ref, o_ref, u_next, acc_prev)
        if not do_main:
            _norm_piece(n_slices, 0, 1, 1.0, *norm_refs)
            return
        z = [_zero_after(*_norm_piece(n_slices, c, NORM_PIECES, 1.0, *norm_refs))
             for c in range(NORM_PIECES)]
        u = u_cur[...]
        ga = jnp.dot(u, wga_ref[...], preferred_element_type=F32) + z[0]
        gm = jnp.dot(u, wgm_ref[...], preferred_element_type=F32) + z[1]
        a = jnp.dot(att_ref[...], wa_ref[...].astype(BF16), preferred_element_type=F32) + z[2]
        m = jnp.dot(hm_ref[...], wm_ref[...].astype(BF16), preferred_element_type=F32) + z[3]
        y = (jax.nn.sigmoid(ga) * a + jax.nn.sigmoid(gm) * m).astype(BF16)
        acc_cur[...] = (jnp.where(j == 0, 0.0, acc_cur[...])
                        + jnp.dot(y, wo_ref[...].astype(BF16), preferred_element_type=F32))

    _for_row_roles((u0, u1), (acc0, acc1), body)


def _merge(h, att, hm, pre_g, wga, wgm, wa, wm, wo, post_g):
    n = h.shape[0]
    tm, tn = MERGE_TM, MERGE_TN
    n_tiles, n_steps, n_slices = n // tm, D_MODEL // tn, tm // SLICE_ROWS
    assert n_steps >= n_slices
    lead, lag, step, tile = _pipeline_maps(n_tiles, n_slices, n_steps)
    return pl.pallas_call(
        _merge_kernel,
        name="merge",
        grid=(n_tiles + 2, n_steps),
        in_specs=[
            pl.BlockSpec((SLICE_ROWS, D_MODEL), lead),
            pl.BlockSpec((SLICE_ROWS, D_MODEL), lag),
            pl.BlockSpec((tm, ATT_Q), tile),
            pl.BlockSpec((tm, M_V), tile),
            pl.BlockSpec((1, D_MODEL), lambda r, j: (0, 0)),
            pl.BlockSpec((D_MODEL, tn), lambda r, j: (0, step(r, j))),
            pl.BlockSpec((D_MODEL, tn), lambda r, j: (0, step(r, j))),
            pl.BlockSpec((ATT_Q, tn), lambda r, j: (0, step(r, j))),
            pl.BlockSpec((M_V, tn), lambda r, j: (0, step(r, j))),
            pl.BlockSpec((tn, D_MODEL), lambda r, j: (step(r, j), 0)),
            pl.BlockSpec((1, D_MODEL), lambda r, j: (0, 0)),
        ],
        out_specs=pl.BlockSpec((SLICE_ROWS, D_MODEL), lag),
        out_shape=jax.ShapeDtypeStruct((n, D_MODEL), F32),
        scratch_shapes=[pltpu.VMEM((tm, D_MODEL), BF16), pltpu.VMEM((tm, D_MODEL), BF16),
                        pltpu.VMEM((tm, D_MODEL), F32), pltpu.VMEM((tm, D_MODEL), F32)],
        compiler_params=pltpu.CompilerParams(
            dimension_semantics=("arbitrary", "arbitrary"), vmem_limit_bytes=VMEM_LIMIT),
    )(h, h, att, hm, pre_g, wga, wgm, wa, wm, wo, post_g)


def kernel(x, ffn1_pre_g, ffn1_wg, ffn1_wu, ffn1_wd, ffn1_post_g, mix_pre_g, w_in, conv_w, conv_b,
           attn_sinks, m_igate_b, m_fgate_b, m_head_g, w_attn_up, w_mlstm_up, w_out, mix_post_g,
           ffn2_pre_g, ffn2_wg, ffn2_wu, ffn2_wd, ffn2_post_g):
    batch, seq, _ = x.shape
    h = x.reshape(batch * seq, D_MODEL)
    for l in range(ffn1_wg.shape[0]):
        bf = lambda w: w.astype(BF16)
        gate0 = W_IN_MAIN + GATE_ROWS
        w_all = bf(w_in[l])
        w_ga = w_all[:, gate0:gate0 + D_MODEL]
        w_gm = w_all[:, gate0 + D_MODEL:gate0 + 2 * D_MODEL]
        gbias = jnp.concatenate(
            [m_igate_b[l], m_fgate_b[l], jnp.zeros((GATE_PAD - 2 * M_HEADS,), F32)])[None, :]
        slopes = jnp.exp2(-8.0 * jnp.arange(1, ATT_HEADS + 1, dtype=F32) / ATT_HEADS)
        scal = jnp.stack([attn_sinks[l].astype(F32), slopes])

        h = _ffn(h, ffn1_pre_g[l][None, :], ffn1_wg[l], ffn1_wu[l], ffn1_wd[l],
                 ffn1_post_g[l][None, :])
        zb, zf, gt = _inproj(h, mix_pre_g[l][None, :], w_all)
        att = _attention(zb, scal, batch, seq)
        hm = _mlstm(zb, zf, gt, conv_w[l], conv_b[l][None, :], gbias, m_head_g[l][None, :],
                    batch, seq)
        h = _merge(h, att, hm, mix_pre_g[l][None, :], w_ga, w_gm, w_attn_up[l],
                   w_mlstm_up[l], w_out[l], mix_post_g[l][None, :])
        h = _ffn(h, ffn2_pre_g[l][None, :], ffn2_wg[l], ffn2_wu[l], ffn2_wd[l],
                 ffn2_post_g[l][None, :])
    return h.reshape(batch, seq, D_MODEL)
```

```python
import functools
import math

import jax
import jax.numpy as jnp
from jax import lax
from jax.experimental import pallas as pl
from jax.experimental.pallas import tpu as pltpu

F32 = jnp.float32
BF16 = jnp.bfloat16

D_MODEL = 2048
ATT_HEADS = 16
ATT_KV_HEADS = 4
ATT_HEAD_DIM = 64
ATT_GROUP = ATT_HEADS // ATT_KV_HEADS
WINDOW = 128
ATT_BLOCK = 128
M_HEADS = 4
M_QK_DIM = 128
M_V_DIM = 256
CONV_WIDTH = 4
D_FF = 5632
EPS = 1e-6

ATT_Q = ATT_HEADS * ATT_HEAD_DIM
ATT_KV = ATT_KV_HEADS * ATT_HEAD_DIM
M_QK = M_HEADS * M_QK_DIM
M_V = M_HEADS * M_V_DIM

LANES = 128
GATE_PAD = LANES
GATE_ROWS = 2 * M_HEADS
ZB_COLS = ATT_Q + M_V + 2 * ATT_KV
ZF_COLS = 2 * M_QK + M_V + GATE_PAD

VMEM_LIMIT = 62 * 1024 * 1024

FFN_TM = 1024
FFN_TF = 512
PROJ_TM = 512
PROJ_TN = 512
ATT_TQ = 512
M_CHUNK = 256
MERGE_TM = 512
MERGE_TN = 512


def _rms(x, g):
    return x * lax.rsqrt(jnp.mean(x * x, axis=-1, keepdims=True) + EPS) * g


def _log_sigmoid(x):
    return jnp.minimum(x, 0.0) - jnp.log1p(jnp.exp(-jnp.abs(x)))


SLICE_ROWS = 128


def _slice_index(f, n_slices):
    return jnp.minimum(f, n_slices - 1)


def _norm_piece(n_slices, piece, n_pieces, res_scale, xa_ref, xb_ref, pre_g_ref, post_g_ref, o_ref,
                xn_next, acc_prev):
    rows = SLICE_ROWS // n_pieces
    s = _slice_index(pl.program_id(1), n_slices)
    src = pl.ds(piece * rows, rows)
    dst = pl.ds(pl.multiple_of(s * SLICE_ROWS + piece * rows, rows), rows)
    xn = _rms(xa_ref[src, :], pre_g_ref[...])
    xn_next[dst, :] = xn.astype(BF16)
    out = xb_ref[src, :] + res_scale * _rms(acc_prev[dst, :], post_g_ref[...])
    o_ref[src, :] = out
    return xn, out


def _zero_after(*values):
    acc = None
    for v in values:
        bits = pltpu.bitcast(v, jnp.uint32)
        for r0 in range(0, bits.shape[0], 8):
            for c0 in range(0, bits.shape[1], LANES):
                tile = bits[r0:r0 + 8, c0:c0 + LANES]
                acc = tile if acc is None else acc | tile
    return jnp.sum(((acc >> 16) >> 16).astype(F32), keepdims=True)


NORM_PIECES = 4


def _for_row_roles(xn_bufs, acc_bufs, body):
    r = pl.program_id(0)
    last = pl.num_programs(0) - 1

    @pl.when((r == 0) & (pl.program_id(1) == 0))
    def _():
        for acc in acc_bufs:
            acc[...] = jnp.zeros_like(acc)

    main = (r > 0) & (r < last)
    for p in range(2):
        roles = (xn_bufs[p], acc_bufs[p], xn_bufs[1 - p], acc_bufs[1 - p])
        pl.when(main & (r % 2 == p))(functools.partial(body, *roles, True))
        pl.when(jnp.logical_not(main) & (r % 2 == p))(functools.partial(body, *roles, False))


def _pipeline_maps(n_tiles, n_slices, n_steps):
    def lead(r, f):
        return (jnp.minimum(r, n_tiles - 1) * n_slices + _slice_index(f, n_slices), 0)

    def lag(r, f):
        return (jnp.where(r < 2, 0, (r - 2) * n_slices + _slice_index(f, n_slices)), 0)

    def step(r, f):
        return jnp.where(r == 0, 0, jnp.where(r == n_tiles + 1, n_steps - 1, f))

    def tile(r, f):
        return (jnp.clip(r - 1, 0, n_tiles - 1), 0)

    return lead, lag, step, tile


def _ffn_kernel(xa_ref, xb_ref, pre_g_ref, wg_ref, wu_ref, wd_ref, post_g_ref, o_ref,
                xn0, xn1, acc0, acc1):
    f = pl.program_id(1)

    n_slices = FFN_TM // SLICE_ROWS

    def body(xn_next, acc_prev, xn_cur, acc_cur, do_main):
        norm_refs = (xa_ref, xb_ref, pre_g_ref, post_g_ref, o_ref, xn_next, acc_prev)
        if not do_main:
            _norm_piece(n_slices, 0, 1, 0.5, *norm_refs)
            return
        z = [_zero_after(*_norm_piece(n_slices, c, NORM_PIECES, 0.5, *norm_refs))
             for c in range(NORM_PIECES)]
        xn = xn_cur[...]
        half = FFN_TF // 2
        hid = []
        for c in range(2):
            cols = slice(c * half, (c + 1) * half)
            g = jnp.dot(xn, wg_ref[:, cols], preferred_element_type=F32) + z[2 * c]
            u = jnp.dot(xn, wu_ref[:, cols].astype(BF16), preferred_element_type=F32) + z[2 * c + 1]
            hid.append((g * jax.nn.sigmoid(g) * u).astype(BF16))
        hid = jnp.concatenate(hid, axis=1)
        acc_cur[...] = (jnp.where(f == 0, 0.0, acc_cur[...])
                        + jnp.dot(hid, wd_ref[...].astype(BF16), preferred_element_type=F32))

    _for_row_roles((xn0, xn1), (acc0, acc1), body)


def _ffn(h, pre_g, wg, wu, wd, post_g):
    n = h.shape[0]
    tm, tf = FFN_TM, FFN_TF
    n_tiles, n_steps, n_slices = n // tm, D_FF // tf, tm // SLICE_ROWS
    assert n_steps >= n_slices
    lead, lag, step, _ = _pipeline_maps(n_tiles, n_slices, n_steps)
    return pl.pallas_call(
        _ffn_kernel,
        name="ffn",
        grid=(n_tiles + 2, n_steps),
        in_specs=[
            pl.BlockSpec((SLICE_ROWS, D_MODEL), lead),
            pl.BlockSpec((SLICE_ROWS, D_MODEL), lag),
            pl.BlockSpec((1, D_MODEL), lambda r, f: (0, 0)),
            pl.BlockSpec((D_MODEL, tf), lambda r, f: (0, step(r, f))),
            pl.BlockSpec((D_MODEL, tf), lambda r, f: (0, step(r, f))),
            pl.BlockSpec((tf, D_MODEL), lambda r, f: (step(r, f), 0)),
            pl.BlockSpec((1, D_MODEL), lambda r, f: (0, 0)),
        ],
        out_specs=pl.BlockSpec((SLICE_ROWS, D_MODEL), lag),
        out_shape=jax.ShapeDtypeStruct((n, D_MODEL), F32),
        scratch_shapes=[pltpu.VMEM((tm, D_MODEL), BF16), pltpu.VMEM((tm, D_MODEL), BF16),
                        pltpu.VMEM((tm, D_MODEL), F32), pltpu.VMEM((tm, D_MODEL), F32)],
        compiler_params=pltpu.CompilerParams(
            dimension_semantics=("arbitrary", "arbitrary"), vmem_limit_bytes=VMEM_LIMIT),
    )(h, h, pre_g, wg, wu, wd, post_g)


_OFF_AK = ATT_Q
_OFF_MQ = _OFF_AK + 2 * ATT_KV
_OFF_MV = _OFF_MQ + 2 * M_QK
_OFF_MO = _OFF_MV + M_V
W_IN_MAIN = _OFF_MO + M_V
_PROJ_SEGMENTS = (
    (0, 0, 0, ATT_Q),
    (0, ATT_Q, _OFF_MV, M_V),
    (0, ATT_Q + M_V, _OFF_AK, 2 * ATT_KV),
    (1, 0, _OFF_MQ, 2 * M_QK),
    (1, 2 * M_QK, _OFF_MO, M_V),
)


def _inproj_kernel(h_ref, g_ref, w_ref, wgate_ref, zb_ref, zf_ref, gt_ref, xn_ref):
    xn_ref[...] = _rms(h_ref[...], g_ref[...]).astype(BF16)
    xn = xn_ref[...]
    outs = (zb_ref, zf_ref)
    for which, out0, w0, width in _PROJ_SEGMENTS:
        for c in range(0, width, PROJ_TN):
            z = jnp.dot(xn, w_ref[:, w0 + c:w0 + c + PROJ_TN], preferred_element_type=F32)
            outs[which][:, out0 + c:out0 + c + PROJ_TN] = z.astype(outs[which].dtype)
    zg = jnp.dot(xn, wgate_ref[...], preferred_element_type=F32)
    zf_ref[:, 2 * M_QK + M_V:] = zg
    gt_ref[...] = zg.T[0:GATE_ROWS, :]


def _inproj(h, g, w_all):
    n = h.shape[0]
    tm = PROJ_TM
    return pl.pallas_call(
        _inproj_kernel,
        name="inproj",
        grid=(n // tm,),
        in_specs=[
            pl.BlockSpec((tm, D_MODEL), lambda i: (i, 0)),
            pl.BlockSpec((1, D_MODEL), lambda i: (0, 0)),
            pl.BlockSpec((D_MODEL, W_IN_MAIN), lambda i: (0, 0), pipeline_mode=pl.Buffered(1)),
            pl.BlockSpec((D_MODEL, GATE_PAD), lambda i: (0, W_IN_MAIN // GATE_PAD),
                         pipeline_mode=pl.Buffered(1)),
        ],
        out_specs=[
            pl.BlockSpec((tm, ZB_COLS), lambda i: (i, 0)),
            pl.BlockSpec((tm, ZF_COLS), lambda i: (i, 0)),
            pl.BlockSpec((GATE_ROWS, tm), lambda i: (0, i)),
        ],
        out_shape=[jax.ShapeDtypeStruct((n, ZB_COLS), BF16),
                   jax.ShapeDtypeStruct((n, ZF_COLS), F32),
                   jax.ShapeDtypeStruct((GATE_ROWS, n), F32)],
        scratch_shapes=[pltpu.VMEM((tm, D_MODEL), BF16)],
        compiler_params=pltpu.CompilerParams(
            dimension_semantics=("parallel",), vmem_limit_bytes=VMEM_LIMIT),
    )(h, g, w_all, w_all)


def _attn_kernel(scal_ref, q_ref, kp_ref, kc_ref, vp_ref, vc_ref, o_ref, bias_ref):
    t = pl.program_id(1)
    blk = ATT_BLOCK
    half = ATT_HEAD_DIM
    scale = 1.0 / math.sqrt(ATT_HEAD_DIM)
    neg_inf = jnp.float32(-jnp.inf)

    qi = lax.broadcasted_iota(jnp.int32, (blk, 2 * blk), 0)
    kj = lax.broadcasted_iota(jnp.int32, (blk, 2 * blk), 1)

    @pl.when(t == 0)
    def _():
        dist = qi - kj + blk
        band = (dist >= 0) & (dist < WINDOW)
        distf = dist.astype(F32)
        for h in range(ATT_HEADS):
            bias_ref[h] = jnp.where(band, -(scal_ref[1, h] * distf), neg_inf)

    kk = jnp.concatenate([kp_ref[...], kc_ref[...]], axis=0).astype(F32) * scale
    vv = jnp.concatenate([vp_ref[...], vc_ref[...]], axis=0).astype(F32)
    rows = kk.shape[0]
    lane = lax.broadcasted_iota(jnp.int32, (rows, LANES), 1)
    low = lane < half

    def split_pair(x):
        xr = pltpu.roll(x, half, axis=1)
        zero = jnp.zeros_like(x)
        first = (jnp.where(low, x, zero).astype(BF16), jnp.where(low, zero, xr).astype(BF16))
        second = (jnp.where(low, xr, zero).astype(BF16), jnp.where(low, zero, x).astype(BF16))
        return first, second

    k_lo_hi, v_lo_hi = [], []
    for pair in range(ATT_KV // LANES):
        ka, kb = split_pair(kk[:, pair * LANES:(pair + 1) * LANES])
        va, vb = split_pair(vv[:, pair * LANES:(pair + 1) * LANES])
        k_lo_hi += [ka, kb]
        v_lo_hi += [va, vb]

    lane_q = lax.broadcasted_iota(jnp.int32, (blk, LANES), 1)
    has_prev = (kj >= blk) | (t > 0)

    low_q = lane_q < half
    pairs = [(g, pr) for g in range(ATT_KV_HEADS) for pr in range(ATT_GROUP // 2)]
    for j in range(ATT_TQ // blk):
        r0 = j * blk
        k2 = [jnp.concatenate([lo[r0:r0 + 2 * blk], hi[r0:r0 + 2 * blk]], axis=0)
              for lo, hi in k_lo_hi]
        v2 = [jnp.concatenate([lo[r0:r0 + 2 * blk], hi[r0:r0 + 2 * blk]], axis=0)
              for lo, hi in v_lo_hi]
        s2 = []
        for g, pr in pairs:
            c0 = (g * ATT_GROUP + 2 * pr) * half
            s2.append(lax.dot_general(q_ref[r0:r0 + blk, c0:c0 + LANES], k2[g],
                                      (((1,), (1,)), ((), ())), preferred_element_type=F32))
        p2, inv2 = [], []
        for idx, (g, pr) in enumerate(pairs):
            h0 = g * ATT_GROUP + 2 * pr
            ps, ms, sums = [], [], []
            for e in range(2):
                s = s2[idx][:, e * 2 * blk:(e + 1) * 2 * blk] + bias_ref[h0 + e]
                if j == 0:
                    s = jnp.where(has_prev, s, neg_inf)
                m = jnp.maximum(jnp.max(s, axis=-1, keepdims=True), scal_ref[0, h0 + e])
                p = jnp.exp(s - m)
                ps.append(p.astype(BF16))
                ms.append(scal_ref[0, h0 + e] - m)
                sums.append(jnp.sum(p, axis=-1, keepdims=True))
            den = jnp.where(low_q, sums[0], sums[1]) + jnp.exp(jnp.where(low_q, ms[0], ms[1]))
            inv2.append(1.0 / den)
            p2.append(jnp.concatenate(ps, axis=1))
        for idx, (g, pr) in enumerate(pairs):
            c0 = (g * ATT_GROUP + 2 * pr) * half
            o2 = jnp.dot(p2[idx], v2[g], preferred_element_type=F32) * inv2[idx]
            o_ref[r0:r0 + blk, c0:c0 + LANES] = o2.astype(BF16)


def _attention(zb, scal, batch, seq):
    n = zb.shape[0]
    tq = ATT_TQ
    nt = seq // tq
    per = tq // ATT_BLOCK
    kcol = (ATT_Q + M_V) // ATT_KV
    vcol = kcol + 1

    def prev_map(col):
        return lambda b, t: (b * (seq // ATT_BLOCK) + jnp.maximum(t * per - 1, 0), col)

    def cur_map(col):
        return lambda b, t: (b * nt + t, col)

    return pl.pallas_call(
        _attn_kernel,
        name="attn",
        grid=(batch, nt),
        in_specs=[
            pl.BlockSpec(memory_space=pltpu.SMEM),
            pl.BlockSpec((tq, ATT_Q), cur_map(0)),
            pl.BlockSpec((ATT_BLOCK, ATT_KV), prev_map(kcol)),
            pl.BlockSpec((tq, ATT_KV), cur_map(kcol)),
            pl.BlockSpec((ATT_BLOCK, ATT_KV), prev_map(vcol)),
            pl.BlockSpec((tq, ATT_KV), cur_map(vcol)),
        ],
        out_specs=pl.BlockSpec((tq, ATT_Q), cur_map(0)),
        out_shape=jax.ShapeDtypeStruct((n, ATT_Q), BF16),
        scratch_shapes=[pltpu.VMEM((ATT_HEADS, ATT_BLOCK, 2 * ATT_BLOCK), F32)],
        compiler_params=pltpu.CompilerParams(
            dimension_semantics=("parallel", "arbitrary"), vmem_limit_bytes=VMEM_LIMIT),
    )(scal, zb, zb, zb, zb, zb)


def _mlstm_kernel(qk_ref, v_ref, og_ref, gt_ref, gtt_ref, convw_ref, convb_ref, gbias_ref,
                  gbias_col_ref, headg_ref, o_ref, xbuf_ref, c_ref, n_ref, m_ref):
    L = M_CHUNK
    pad = 8

    @pl.when(pl.program_id(1) == 0)
    def _():
        xbuf_ref[0:pad, :] = jnp.zeros((pad, 2 * M_QK), F32)
        c_ref[...] = jnp.zeros_like(c_ref)
        n_ref[...] = jnp.zeros_like(n_ref)
        m_ref[...] = jnp.zeros_like(m_ref)

    xbuf_ref[pad:pad + L, :] = qk_ref[...]
    w = convw_ref[...]
    base = pad - (CONV_WIDTH - 1)
    y = xbuf_ref[pad:pad + L, :] * w[CONV_WIDTH - 1:CONV_WIDTH, :] + convb_ref[...]
    for j in range(CONV_WIDTH - 1):
        y = y + xbuf_ref[pl.ds(base + j, L), :] * w[j:j + 1, :]
    xbuf_ref[0:pad, :] = xbuf_ref[L:L + pad, :]
    qk = y * jax.nn.sigmoid(y)

    gl = gt_ref[...] + gbias_ref[...]
    lf = _log_sigmoid(gl)
    gl_t = gtt_ref[...] + gbias_col_ref[...]
    lf_t = _log_sigmoid(gl_t)

    ti = lax.broadcasted_iota(jnp.int32, (L, L), 0)
    si = lax.broadcasted_iota(jnp.int32, (L, L), 1)
    tril = si <= ti
    triu = ti <= si
    neg_inf = jnp.float32(-jnp.inf)
    kscale = 1.0 / math.sqrt(M_QK_DIM)

    heads = range(M_HEADS)
    q_f = [qk[:, h * M_QK_DIM:(h + 1) * M_QK_DIM] for h in heads]
    k_f = [qk[:, M_QK + h * M_QK_DIM:M_QK + (h + 1) * M_QK_DIM] * kscale for h in heads]
    v_b = [v_ref[:, h * M_V_DIM:(h + 1) * M_V_DIM] for h in heads]
    q_b = [q.astype(BF16) for q in q_f]
    k_b = [k.astype(BF16) for k in k_f]
    m_in = [m_ref[h, 0:1, 0:1] for h in heads]
    c_in = [c_ref[h] for h in heads]
    n_in = [n_ref[h] for h in heads]

    b_col, dmat, m_t, inter_w, pexp, kwt, decay, m_new = [], [], [], [], [], [], [], []
    for h in heads:
        li_col = gl[:, h:h + 1]
        lf_col = lf[:, M_HEADS + h:M_HEADS + h + 1]
        li_row = gl_t[h:h + 1, :]
        lf_row = lf_t[M_HEADS + h:M_HEADS + h + 1, :]
        bc = jnp.sum(jnp.where(tril, lf_row, 0.0), axis=1, keepdims=True)
        br = jnp.sum(jnp.where(triu, lf_col, 0.0), axis=0, keepdims=True)
        b_last = bc[L - 1:L, :]
        dm = jnp.where(tril, bc - br + li_row, neg_inf)
        a_col = bc + m_in[h]
        mt = jnp.maximum(a_col, jnp.max(dm, axis=1, keepdims=True))
        inter_w.append(jnp.exp(a_col - mt))
        pexp.append(jnp.exp(dm - mt))
        m_t.append(mt)
        g_col = b_last - bc + li_col
        mn = jnp.maximum(b_last + m_in[h], jnp.max(g_col, axis=0, keepdims=True))
        decay.append(jnp.exp(b_last + m_in[h] - mn))
        kwt.append(jnp.exp(g_col - mn))
        m_new.append(mn)

    p = [lax.dot_general(q_b[h], k_b[h], (((1,), (1,)), ((), ())), preferred_element_type=F32)
         * pexp[h] for h in heads]
    hh = []
    for h in heads:
        num = (jnp.dot(p[h].astype(BF16), v_b[h], preferred_element_type=F32)
               + inter_w[h] * jnp.dot(q_b[h], c_in[h].astype(BF16), preferred_element_type=F32))
        den = (jnp.sum(p[h], axis=1, keepdims=True)
               + inter_w[h] * jnp.sum(q_f[h] * n_in[h], axis=1, keepdims=True))
        hh.append(num / jnp.maximum(jnp.abs(den), jnp.exp(-m_t[h])))
    for h in heads:
        kw = k_f[h] * kwt[h]
        c_ref[h] = decay[h] * c_in[h] + lax.dot_general(
            kw.astype(BF16), v_b[h], (((0,), (0,)), ((), ())), preferred_element_type=F32)
        n_ref[h] = decay[h] * n_in[h] + jnp.sum(kw, axis=0, keepdims=True)
        m_ref[h] = jnp.broadcast_to(m_new[h], m_ref.shape[1:])
    for h in heads:
        cs = slice(h * M_V_DIM, (h + 1) * M_V_DIM)
        hn = _rms(hh[h], headg_ref[:, cs])
        o_ref[:, cs] = (jax.nn.sigmoid(og_ref[:, cs]) * hn).astype(BF16)


def _mlstm(zb, zf, gt, conv_w, conv_b, gbias, head_g, batch, seq):
    n = zb.shape[0]
    L = M_CHUNK
    nc = seq // L
    row = lambda b, c: b * nc + c
    return pl.pallas_call(
        _mlstm_kernel,
        name="mlstm",
        grid=(batch, nc),
        in_specs=[
            pl.BlockSpec((L, 2 * M_QK), lambda b, c: (row(b, c), 0)),
            pl.BlockSpec((L, M_V), lambda b, c: (row(b, c), ATT_Q // M_V)),
            pl.BlockSpec((L, M_V), lambda b, c: (row(b, c), 2 * M_QK // M_V)),
            pl.BlockSpec((L, GATE_PAD), lambda b, c: (row(b, c), (2 * M_QK + M_V) // GATE_PAD)),
            pl.BlockSpec((GATE_ROWS, L), lambda b, c: (0, row(b, c))),
            pl.BlockSpec((CONV_WIDTH, 2 * M_QK), lambda b, c: (0, 0)),
            pl.BlockSpec((1, 2 * M_QK), lambda b, c: (0, 0)),
            pl.BlockSpec((1, GATE_PAD), lambda b, c: (0, 0)),
            pl.BlockSpec((GATE_ROWS, 1), lambda b, c: (0, 0)),
            pl.BlockSpec((1, M_V), lambda b, c: (0, 0)),
        ],
        out_specs=pl.BlockSpec((L, M_V), lambda b, c: (row(b, c), 0)),
        out_shape=jax.ShapeDtypeStruct((n, M_V), BF16),
        scratch_shapes=[
            pltpu.VMEM((L + 8, 2 * M_QK), F32),
            pltpu.VMEM((M_HEADS, M_QK_DIM, M_V_DIM), F32),
            pltpu.VMEM((M_HEADS, 1, M_QK_DIM), F32),
            pltpu.VMEM((M_HEADS, 8, LANES), F32),
        ],
        compiler_params=pltpu.CompilerParams(
            dimension_semantics=("parallel", "arbitrary"), vmem_limit_bytes=VMEM_LIMIT),
    )(zf, zb, zf, zf, gt, conv_w, conv_b, gbias, gbias[0, :GATE_ROWS, None], head_g)


def _merge_kernel(ha_ref, hb_ref, att_ref, hm_ref, pre_g_ref, wga_ref, wgm_ref, wa_ref, wm_ref, wo_ref,
                  post_g_ref, o_ref, u0, u1, acc0, acc1):
    j = pl.program_id(1)

    n_slices = MERGE_TM // SLICE_ROWS

    def body(u_next, acc_prev, u_cur, acc_cur, do_main):
        norm_refs = (ha_ref, hb_ref, pre_g_ref, post_g_ref, o_ref, u_next, acc_prev)
        if not do_main:
            _norm_piece(n_slices, 0, 1, 1.0, *norm_refs)
            return
        z = [_zero_after(*_norm_piece(n_slices, c, NORM_PIECES, 1.0, *norm_refs))
             for c in range(NORM_PIECES)]
        u = u_cur[...]
        ga = jnp.dot(u, wga_ref[...], preferred_element_type=F32) + z[0]
        gm = jnp.dot(u, wgm_ref[...], preferred_element_type=F32) + z[1]
        a = jnp.dot(att_ref[...], wa_ref[...], preferred_element_type=F32) + z[2]
        m = jnp.dot(hm_ref[...], wm_ref[...], preferred_element_type=F32) + z[3]
        y = (jax.nn.sigmoid(ga) * a + jax.nn.sigmoid(gm) * m).astype(BF16)
        acc_cur[...] = (jnp.where(j == 0, 0.0, acc_cur[...])
                        + jnp.dot(y, wo_ref[...], preferred_element_type=F32))

    _for_row_roles((u0, u1), (acc0, acc1), body)


def _merge(h, att, hm, pre_g, wga, wgm, wa, wm, wo, post_g):
    n = h.shape[0]
    tm, tn = MERGE_TM, MERGE_TN
    n_tiles, n_steps, n_slices = n // tm, D_MODEL // tn, tm // SLICE_ROWS
    assert n_steps >= n_slices
    lead, lag, step, tile = _pipeline_maps(n_tiles, n_slices, n_steps)
    return pl.pallas_call(
        _merge_kernel,
        name="merge",
        grid=(n_tiles + 2, n_steps),
        in_specs=[
            pl.BlockSpec((SLICE_ROWS, D_MODEL), lead),
            pl.BlockSpec((SLICE_ROWS, D_MODEL), lag),
            pl.BlockSpec((tm, ATT_Q), tile),
            pl.BlockSpec((tm, M_V), tile),
            pl.BlockSpec((1, D_MODEL), lambda r, j: (0, 0)),
            pl.BlockSpec((D_MODEL, tn), lambda r, j: (0, step(r, j))),
            pl.BlockSpec((D_MODEL, tn), lambda r, j: (0, step(r, j))),
            pl.BlockSpec((ATT_Q, tn), lambda r, j: (0, step(r, j))),
            pl.BlockSpec((M_V, tn), lambda r, j: (0, step(r, j))),
            pl.BlockSpec((tn, D_MODEL), lambda r, j: (step(r, j), 0)),
            pl.BlockSpec((1, D_MODEL), lambda r, j: (0, 0)),
        ],
        out_specs=pl.BlockSpec((SLICE_ROWS, D_MODEL), lag),
        out_shape=jax.ShapeDtypeStruct((n, D_MODEL), F32),
        scratch_shapes=[pltpu.VMEM((tm, D_MODEL), BF16), pltpu.VMEM((tm, D_MODEL), BF16),
                        pltpu.VMEM((tm, D_MODEL), F32), pltpu.VMEM((tm, D_MODEL), F32)],
        compiler_params=pltpu.CompilerParams(
            dimension_semantics=("arbitrary", "arbitrary"), vmem_limit_bytes=VMEM_LIMIT),
    )(h, h, att, hm, pre_g, wga, wgm, wa, wm, wo, post_g)


def kernel(x, ffn1_pre_g, ffn1_wg, ffn1_wu, ffn1_wd, ffn1_post_g, mix_pre_g, w_in, conv_w, conv_b,
           attn_sinks, m_igate_b, m_fgate_b, m_head_g, w_attn_up, w_mlstm_up, w_out, mix_post_g,
           ffn2_pre_g, ffn2_wg, ffn2_wu, ffn2_wd, ffn2_post_g):
    batch, seq, _ = x.shape
    h = x.reshape(batch * seq, D_MODEL)
    for l in range(ffn1_wg.shape[0]):
        bf = lambda w: w.astype(BF16)
        gate0 = W_IN_MAIN + GATE_ROWS
        w_all = bf(w_in[l])
        w_ga = w_all[:, gate0:gate0 + D_MODEL]
        w_gm = w_all[:, gate0 + D_MODEL:gate0 + 2 * D_MODEL]
        gbias = jnp.concatenate(
            [m_igate_b[l], m_fgate_b[l], jnp.zeros((GATE_PAD - 2 * M_HEADS,), F32)])[None, :]
        slopes = jnp.exp2(-8.0 * jnp.arange(1, ATT_HEADS + 1, dtype=F32) / ATT_HEADS)
        scal = jnp.stack([attn_sinks[l].astype(F32), slopes])

        h = _ffn(h, ffn1_pre_g[l][None, :], bf(ffn1_wg[l]), ffn1_wu[l], ffn1_wd[l],
                 ffn1_post_g[l][None, :])
        zb, zf, gt = _inproj(h, mix_pre_g[l][None, :], w_all)
        att = _attention(zb, scal, batch, seq)
        hm = _mlstm(zb, zf, gt, conv_w[l], conv_b[l][None, :], gbias, m_head_g[l][None, :],
                    batch, seq)
        h = _merge(h, att, hm, mix_pre_g[l][None, :], w_ga, w_gm, bf(w_attn_up[l]),
                   bf(w_mlstm_up[l]), bf(w_out[l]), mix_post_g[l][None, :])
        h = _ffn(h, ffn2_pre_g[l][None, :], bf(ffn2_wg[l]), ffn2_wu[l], ffn2_wd[l],
                 ffn2_post_g[l][None, :])
    return h.reshape(batch, seq, D_MODEL)
```

```python
import functools
import math

import jax
import jax.numpy as jnp
from jax import lax
from jax.experimental import pallas as pl
from jax.experimental.pallas import tpu as pltpu

F32 = jnp.float32
BF16 = jnp.bfloat16

D_MODEL = 2048
ATT_HEADS = 16
ATT_KV_HEADS = 4
ATT_HEAD_DIM = 64
ATT_GROUP = ATT_HEADS // ATT_KV_HEADS
WINDOW = 128
ATT_BLOCK = 128
M_HEADS = 4
M_QK_DIM = 128
M_V_DIM = 256
CONV_WIDTH = 4
D_FF = 5632
EPS = 1e-6

ATT_Q = ATT_HEADS * ATT_HEAD_DIM
ATT_KV = ATT_KV_HEADS * ATT_HEAD_DIM
M_QK = M_HEADS * M_QK_DIM
M_V = M_HEADS * M_V_DIM

LANES = 128
GATE_PAD = LANES
GATE_ROWS = 2 * M_HEADS
ZB_COLS = ATT_Q + M_V + 2 * ATT_KV
ZF_COLS = 2 * M_QK + M_V + GATE_PAD

VMEM_LIMIT = 62 * 1024 * 1024

FFN_TM = 1024
FFN_TF = 512
PROJ_TM = 512
PROJ_TN = 512
ATT_TQ = 512
M_CHUNK = 256
MERGE_TM = 512
MERGE_TN = 512


def _rms(x, g):
    return x * lax.rsqrt(jnp.mean(x * x, axis=-1, keepdims=True) + EPS) * g


def _log_sigmoid(x):
    return jnp.minimum(x, 0.0) - jnp.log1p(jnp.exp(-jnp.abs(x)))


SLICE_ROWS = 128


def _slice_index(f, n_slices):
    return jnp.minimum(f, n_slices - 1)


def _norm_slices(n_slices, res_scale, xa_ref, xb_ref, pre_g_ref, post_g_ref, o_ref, xn_next, acc_prev):
    s = _slice_index(pl.program_id(1), n_slices)
    rows = pl.ds(pl.multiple_of(s * SLICE_ROWS, SLICE_ROWS), SLICE_ROWS)
    xn_next[rows, :] = _rms(xa_ref[...], pre_g_ref[...]).astype(BF16)
    o_ref[...] = xb_ref[...] + res_scale * _rms(acc_prev[rows, :], post_g_ref[...])


def _for_row_roles(xn_bufs, acc_bufs, body):
    r = pl.program_id(0)
    last = pl.num_programs(0) - 1

    @pl.when((r == 0) & (pl.program_id(1) == 0))
    def _():
        for acc in acc_bufs:
            acc[...] = jnp.zeros_like(acc)

    main = (r > 0) & (r < last)
    for p in range(2):
        roles = (xn_bufs[p], acc_bufs[p], xn_bufs[1 - p], acc_bufs[1 - p])
        pl.when(main & (r % 2 == p))(functools.partial(body, *roles, True))
        pl.when(jnp.logical_not(main) & (r % 2 == p))(functools.partial(body, *roles, False))


def _pipeline_maps(n_tiles, n_slices, n_steps):
    def lead(r, f):
        return (jnp.minimum(r, n_tiles - 1) * n_slices + _slice_index(f, n_slices), 0)

    def lag(r, f):
        return (jnp.where(r < 2, 0, (r - 2) * n_slices + _slice_index(f, n_slices)), 0)

    def step(r, f):
        return jnp.where(r == 0, 0, jnp.where(r == n_tiles + 1, n_steps - 1, f))

    def tile(r, f):
        return (jnp.clip(r - 1, 0, n_tiles - 1), 0)

    return lead, lag, step, tile


def _ffn_kernel(xa_ref, xb_ref, pre_g_ref, wg_ref, wu_ref, wd_ref, post_g_ref, o_ref,
                xn0, xn1, acc0, acc1):
    f = pl.program_id(1)

    n_slices = FFN_TM // SLICE_ROWS

    def body(xn_next, acc_prev, xn_cur, acc_cur, do_main):
        _norm_slices(n_slices, 0.5, xa_ref, xb_ref, pre_g_ref, post_g_ref, o_ref, xn_next, acc_prev)
        if do_main:
            xn = xn_cur[...]
            g = jnp.dot(xn, wg_ref[...].astype(BF16), preferred_element_type=F32)
            u = jnp.dot(xn, wu_ref[...].astype(BF16), preferred_element_type=F32)
            hid = (g * jax.nn.sigmoid(g) * u).astype(BF16)
            acc_cur[...] = (jnp.where(f == 0, 0.0, acc_cur[...])
                            + jnp.dot(hid, wd_ref[...].astype(BF16), preferred_element_type=F32))

    _for_row_roles((xn0, xn1), (acc0, acc1), body)


def _ffn(h, pre_g, wg, wu, wd, post_g):
    n = h.shape[0]
    tm, tf = FFN_TM, FFN_TF
    n_tiles, n_steps, n_slices = n // tm, D_FF // tf, tm // SLICE_ROWS
    assert n_steps >= n_slices
    lead, lag, step, _ = _pipeline_maps(n_tiles, n_slices, n_steps)
    return pl.pallas_call(
        _ffn_kernel,
        name="ffn",
        grid=(n_tiles + 2, n_steps),
        in_specs=[
            pl.BlockSpec((SLICE_ROWS, D_MODEL), lead),
            pl.BlockSpec((SLICE_ROWS, D_MODEL), lag),
            pl.BlockSpec((1, D_MODEL), lambda r, f: (0, 0)),
            pl.BlockSpec((D_MODEL, tf), lambda r, f: (0, step(r, f))),
            pl.BlockSpec((D_MODEL, tf), lambda r, f: (0, step(r, f))),
            pl.BlockSpec((tf, D_MODEL), lambda r, f: (step(r, f), 0)),
            pl.BlockSpec((1, D_MODEL), lambda r, f: (0, 0)),
        ],
        out_specs=pl.BlockSpec((SLICE_ROWS, D_MODEL), lag),
        out_shape=jax.ShapeDtypeStruct((n, D_MODEL), F32),
        scratch_shapes=[pltpu.VMEM((tm, D_MODEL), BF16), pltpu.VMEM((tm, D_MODEL), BF16),
                        pltpu.VMEM((tm, D_MODEL), F32), pltpu.VMEM((tm, D_MODEL), F32)],
        compiler_params=pltpu.CompilerParams(
            dimension_semantics=("arbitrary", "arbitrary"), vmem_limit_bytes=VMEM_LIMIT),
    )(h, h, pre_g, wg, wu, wd, post_g)


_OFF_AK = ATT_Q
_OFF_MQ = _OFF_AK + 2 * ATT_KV
_OFF_MV = _OFF_MQ + 2 * M_QK
_OFF_MO = _OFF_MV + M_V
W_IN_MAIN = _OFF_MO + M_V
_PROJ_SEGMENTS = (
    (0, 0, 0, ATT_Q),
    (0, ATT_Q, _OFF_MV, M_V),
    (0, ATT_Q + M_V, _OFF_AK, 2 * ATT_KV),
    (1, 0, _OFF_MQ, 2 * M_QK),
    (1, 2 * M_QK, _OFF_MO, M_V),
)


def _inproj_kernel(h_ref, g_ref, w_ref, wgate_ref, zb_ref, zf_ref, gt_ref, xn_ref):
    xn_ref[...] = _rms(h_ref[...], g_ref[...]).astype(BF16)
    xn = xn_ref[...]
    outs = (zb_ref, zf_ref)
    for which, out0, w0, width in _PROJ_SEGMENTS:
        for c in range(0, width, PROJ_TN):
            z = jnp.dot(xn, w_ref[:, w0 + c:w0 + c + PROJ_TN], preferred_element_type=F32)
            outs[which][:, out0 + c:out0 + c + PROJ_TN] = z.astype(outs[which].dtype)
    zg = jnp.dot(xn, wgate_ref[...], preferred_element_type=F32)
    zf_ref[:, 2 * M_QK + M_V:] = zg
    gt_ref[...] = zg.T[0:GATE_ROWS, :]


def _inproj(h, g, w_all):
    n = h.shape[0]
    tm = PROJ_TM
    return pl.pallas_call(
        _inproj_kernel,
        name="inproj",
        grid=(n // tm,),
        in_specs=[
            pl.BlockSpec((tm, D_MODEL), lambda i: (i, 0)),
            pl.BlockSpec((1, D_MODEL), lambda i: (0, 0)),
            pl.BlockSpec((D_MODEL, W_IN_MAIN), lambda i: (0, 0), pipeline_mode=pl.Buffered(1)),
            pl.BlockSpec((D_MODEL, GATE_PAD), lambda i: (0, W_IN_MAIN // GATE_PAD),
                         pipeline_mode=pl.Buffered(1)),
        ],
        out_specs=[
            pl.BlockSpec((tm, ZB_COLS), lambda i: (i, 0)),
            pl.BlockSpec((tm, ZF_COLS), lambda i: (i, 0)),
            pl.BlockSpec((GATE_ROWS, tm), lambda i: (0, i)),
        ],
        out_shape=[jax.ShapeDtypeStruct((n, ZB_COLS), BF16),
                   jax.ShapeDtypeStruct((n, ZF_COLS), F32),
                   jax.ShapeDtypeStruct((GATE_ROWS, n), F32)],
        scratch_shapes=[pltpu.VMEM((tm, D_MODEL), BF16)],
        compiler_params=pltpu.CompilerParams(
            dimension_semantics=("parallel",), vmem_limit_bytes=VMEM_LIMIT),
    )(h, g, w_all, w_all)


def _attn_kernel(scal_ref, q_ref, kp_ref, kc_ref, vp_ref, vc_ref, o_ref, bias_ref):
    t = pl.program_id(1)
    blk = ATT_BLOCK
    half = ATT_HEAD_DIM
    scale = 1.0 / math.sqrt(ATT_HEAD_DIM)
    neg_inf = jnp.float32(-jnp.inf)

    qi = lax.broadcasted_iota(jnp.int32, (blk, 2 * blk), 0)
    kj = lax.broadcasted_iota(jnp.int32, (blk, 2 * blk), 1)

    @pl.when(t == 0)
    def _():
        dist = qi - kj + blk
        band = (dist >= 0) & (dist < WINDOW)
        distf = dist.astype(F32)
        for h in range(ATT_HEADS):
            bias_ref[h] = jnp.where(band, -(scal_ref[1, h] * distf), neg_inf)

    kk = jnp.concatenate([kp_ref[...], kc_ref[...]], axis=0).astype(F32) * scale
    vv = jnp.concatenate([vp_ref[...], vc_ref[...]], axis=0).astype(F32)
    rows = kk.shape[0]
    lane = lax.broadcasted_iota(jnp.int32, (rows, LANES), 1)
    low = lane < half

    def split_pair(x):
        xr = pltpu.roll(x, half, axis=1)
        zero = jnp.zeros_like(x)
        first = (jnp.where(low, x, zero).astype(BF16), jnp.where(low, zero, xr).astype(BF16))
        second = (jnp.where(low, xr, zero).astype(BF16), jnp.where(low, zero, x).astype(BF16))
        return first, second

    k_lo_hi, v_lo_hi = [], []
    for pair in range(ATT_KV // LANES):
        ka, kb = split_pair(kk[:, pair * LANES:(pair + 1) * LANES])
        va, vb = split_pair(vv[:, pair * LANES:(pair + 1) * LANES])
        k_lo_hi += [ka, kb]
        v_lo_hi += [va, vb]

    lane_q = lax.broadcasted_iota(jnp.int32, (blk, LANES), 1)
    has_prev = (kj >= blk) | (t > 0)

    low_q = lane_q < half
    pairs = [(g, pr) for g in range(ATT_KV_HEADS) for pr in range(ATT_GROUP // 2)]
    for j in range(ATT_TQ // blk):
        r0 = j * blk
        k2 = [jnp.concatenate([lo[r0:r0 + 2 * blk], hi[r0:r0 + 2 * blk]], axis=0)
              for lo, hi in k_lo_hi]
        v2 = [jnp.concatenate([lo[r0:r0 + 2 * blk], hi[r0:r0 + 2 * blk]], axis=0)
              for lo, hi in v_lo_hi]
        s2 = []
        for g, pr in pairs:
            c0 = (g * ATT_GROUP + 2 * pr) * half
            s2.append(lax.dot_general(q_ref[r0:r0 + blk, c0:c0 + LANES], k2[g],
                                      (((1,), (1,)), ((), ())), preferred_element_type=F32))
        p2, inv2 = [], []
        for idx, (g, pr) in enumerate(pairs):
            h0 = g * ATT_GROUP + 2 * pr
            ps, ms, sums = [], [], []
            for e in range(2):
                s = s2[idx][:, e * 2 * blk:(e + 1) * 2 * blk] + bias_ref[h0 + e]
                if j == 0:
                    s = jnp.where(has_prev, s, neg_inf)
                m = jnp.maximum(jnp.max(s, axis=-1, keepdims=True), scal_ref[0, h0 + e])
                p = jnp.exp(s - m)
                ps.append(p.astype(BF16))
                ms.append(scal_ref[0, h0 + e] - m)
                sums.append(jnp.sum(p, axis=-1, keepdims=True))
            den = jnp.where(low_q, sums[0], sums[1]) + jnp.exp(jnp.where(low_q, ms[0], ms[1]))
            inv2.append(1.0 / den)
            p2.append(jnp.concatenate(ps, axis=1))
        for idx, (g, pr) in enumerate(pairs):
            c0 = (g * ATT_GROUP + 2 * pr) * half
            o2 = jnp.dot(p2[idx], v2[g], preferred_element_type=F32) * inv2[idx]
            o_ref[r0:r0 + blk, c0:c0 + LANES] = o2.astype(BF16)


def _attention(zb, scal, batch, seq):
    n = zb.shape[0]
    tq = ATT_TQ
    nt = seq // tq
    per = tq // ATT_BLOCK
    kcol = (ATT_Q + M_V) // ATT_KV
    vcol = kcol + 1

    def prev_map(col):
        return lambda b, t: (b * (seq // ATT_BLOCK) + jnp.maximum(t * per - 1, 0), col)

    def cur_map(col):
        return lambda b, t: (b * nt + t, col)

    return pl.pallas_call(
        _attn_kernel,
        name="attn",
        grid=(batch, nt),
        in_specs=[
            pl.BlockSpec(memory_space=pltpu.SMEM),
            pl.BlockSpec((tq, ATT_Q), cur_map(0)),
            pl.BlockSpec((ATT_BLOCK, ATT_KV), prev_map(kcol)),
            pl.BlockSpec((tq, ATT_KV), cur_map(kcol)),
            pl.BlockSpec((ATT_BLOCK, ATT_KV), prev_map(vcol)),
            pl.BlockSpec((tq, ATT_KV), cur_map(vcol)),
        ],
        out_specs=pl.BlockSpec((tq, ATT_Q), cur_map(0)),
        out_shape=jax.ShapeDtypeStruct((n, ATT_Q), BF16),
        scratch_shapes=[pltpu.VMEM((ATT_HEADS, ATT_BLOCK, 2 * ATT_BLOCK), F32)],
        compiler_params=pltpu.CompilerParams(
            dimension_semantics=("parallel", "arbitrary"), vmem_limit_bytes=VMEM_LIMIT),
    )(scal, zb, zb, zb, zb, zb)


def _mlstm_kernel(qk_ref, v_ref, og_ref, gt_ref, gtt_ref, convw_ref, convb_ref, gbias_ref,
                  gbias_col_ref, headg_ref, o_ref, xbuf_ref, c_ref, n_ref, m_ref):
    L = M_CHUNK
    pad = 8

    @pl.when(pl.program_id(1) == 0)
    def _():
        xbuf_ref[0:pad, :] = jnp.zeros((pad, 2 * M_QK), F32)
        c_ref[...] = jnp.zeros_like(c_ref)
        n_ref[...] = jnp.zeros_like(n_ref)
        m_ref[...] = jnp.zeros_like(m_ref)

    xbuf_ref[pad:pad + L, :] = qk_ref[...]
    w = convw_ref[...]
    base = pad - (CONV_WIDTH - 1)
    y = xbuf_ref[pad:pad + L, :] * w[CONV_WIDTH - 1:CONV_WIDTH, :] + convb_ref[...]
    for j in range(CONV_WIDTH - 1):
        y = y + xbuf_ref[pl.ds(base + j, L), :] * w[j:j + 1, :]
    xbuf_ref[0:pad, :] = xbuf_ref[L:L + pad, :]
    qk = y * jax.nn.sigmoid(y)

    gl = gt_ref[...] + gbias_ref[...]
    lf = _log_sigmoid(gl)
    gl_t = gtt_ref[...] + gbias_col_ref[...]
    lf_t = _log_sigmoid(gl_t)

    ti = lax.broadcasted_iota(jnp.int32, (L, L), 0)
    si = lax.broadcasted_iota(jnp.int32, (L, L), 1)
    tril = si <= ti
    triu = ti <= si
    neg_inf = jnp.float32(-jnp.inf)
    kscale = 1.0 / math.sqrt(M_QK_DIM)

    heads = range(M_HEADS)
    q_f = [qk[:, h * M_QK_DIM:(h + 1) * M_QK_DIM] for h in heads]
    k_f = [qk[:, M_QK + h * M_QK_DIM:M_QK + (h + 1) * M_QK_DIM] * kscale for h in heads]
    v_b = [v_ref[:, h * M_V_DIM:(h + 1) * M_V_DIM] for h in heads]
    q_b = [q.astype(BF16) for q in q_f]
    k_b = [k.astype(BF16) for k in k_f]
    m_in = [m_ref[h, 0:1, 0:1] for h in heads]
    c_in = [c_ref[h] for h in heads]
    n_in = [n_ref[h] for h in heads]

    b_col, dmat, m_t, inter_w, pexp, kwt, decay, m_new = [], [], [], [], [], [], [], []
    for h in heads:
        li_col = gl[:, h:h + 1]
        lf_col = lf[:, M_HEADS + h:M_HEADS + h + 1]
        li_row = gl_t[h:h + 1, :]
        lf_row = lf_t[M_HEADS + h:M_HEADS + h + 1, :]
        bc = jnp.sum(jnp.where(tril, lf_row, 0.0), axis=1, keepdims=True)
        br = jnp.sum(jnp.where(triu, lf_col, 0.0), axis=0, keepdims=True)
        b_last = bc[L - 1:L, :]
        dm = jnp.where(tril, bc - br + li_row, neg_inf)
        a_col = bc + m_in[h]
        mt = jnp.maximum(a_col, jnp.max(dm, axis=1, keepdims=True))
        inter_w.append(jnp.exp(a_col - mt))
        pexp.append(jnp.exp(dm - mt))
        m_t.append(mt)
        g_col = b_last - bc + li_col
        mn = jnp.maximum(b_last + m_in[h], jnp.max(g_col, axis=0, keepdims=True))
        decay.append(jnp.exp(b_last + m_in[h] - mn))
        kwt.append(jnp.exp(g_col - mn))
        m_new.append(mn)

    p = [lax.dot_general(q_b[h], k_b[h], (((1,), (1,)), ((), ())), preferred_element_type=F32)
         * pexp[h] for h in heads]
    hh = []
    for h in heads:
        num = (jnp.dot(p[h].astype(BF16), v_b[h], preferred_element_type=F32)
               + inter_w[h] * jnp.dot(q_b[h], c_in[h].astype(BF16), preferred_element_type=F32))
        den = (jnp.sum(p[h], axis=1, keepdims=True)
               + inter_w[h] * jnp.sum(q_f[h] * n_in[h], axis=1, keepdims=True))
        hh.append(num / jnp.maximum(jnp.abs(den), jnp.exp(-m_t[h])))
    for h in heads:
        kw = k_f[h] * kwt[h]
        c_ref[h] = decay[h] * c_in[h] + lax.dot_general(
            kw.astype(BF16), v_b[h], (((0,), (0,)), ((), ())), preferred_element_type=F32)
        n_ref[h] = decay[h] * n_in[h] + jnp.sum(kw, axis=0, keepdims=True)
        m_ref[h] = jnp.broadcast_to(m_new[h], m_ref.shape[1:])
    for h in heads:
        cs = slice(h * M_V_DIM, (h + 1) * M_V_DIM)
        hn = _rms(hh[h], headg_ref[:, cs])
        o_ref[:, cs] = (jax.nn.sigmoid(og_ref[:, cs]) * hn).astype(BF16)


def _mlstm(zb, zf, gt, conv_w, conv_b, gbias, head_g, batch, seq):
    n = zb.shape[0]
    L = M_CHUNK
    nc = seq // L
    row = lambda b, c: b * nc + c
    return pl.pallas_call(
        _mlstm_kernel,
        name="mlstm",
        grid=(batch, nc),
        in_specs=[
            pl.BlockSpec((L, 2 * M_QK), lambda b, c: (row(b, c), 0)),
            pl.BlockSpec((L, M_V), lambda b, c: (row(b, c), ATT_Q // M_V)),
            pl.BlockSpec((L, M_V), lambda b, c: (row(b, c), 2 * M_QK // M_V)),
            pl.BlockSpec((L, GATE_PAD), lambda b, c: (row(b, c), (2 * M_QK + M_V) // GATE_PAD)),
            pl.BlockSpec((GATE_ROWS, L), lambda b, c: (0, row(b, c))),
            pl.BlockSpec((CONV_WIDTH, 2 * M_QK), lambda b, c: (0, 0)),
            pl.BlockSpec((1, 2 * M_QK), lambda b, c: (0, 0)),
            pl.BlockSpec((1, GATE_PAD), lambda b, c: (0, 0)),
            pl.BlockSpec((GATE_ROWS, 1), lambda b, c: (0, 0)),
            pl.BlockSpec((1, M_V), lambda b, c: (0, 0)),
        ],
        out_specs=pl.BlockSpec((L, M_V), lambda b, c: (row(b, c), 0)),
        out_shape=jax.ShapeDtypeStruct((n, M_V), BF16),
        scratch_shapes=[
            pltpu.VMEM((L + 8, 2 * M_QK), F32),
            pltpu.VMEM((M_HEADS, M_QK_DIM, M_V_DIM), F32),
            pltpu.VMEM((M_HEADS, 1, M_QK_DIM), F32),
            pltpu.VMEM((M_HEADS, 8, LANES), F32),
        ],
        compiler_params=pltpu.CompilerParams(
            dimension_semantics=("parallel", "arbitrary"), vmem_limit_bytes=VMEM_LIMIT),
    )(zf, zb, zf, zf, gt, conv_w, conv_b, gbias, gbias[0, :GATE_ROWS, None], head_g)


def _merge_kernel(ha_ref, hb_ref, att_ref, hm_ref, pre_g_ref, wga_ref, wgm_ref, wa_ref, wm_ref, wo_ref,
                  post_g_ref, o_ref, u0, u1, acc0, acc1):
    j = pl.program_id(1)

    n_slices = MERGE_TM // SLICE_ROWS

    def body(u_next, acc_prev, u_cur, acc_cur, do_main):
        _norm_slices(n_slices, 1.0, ha_ref, hb_ref, pre_g_ref, post_g_ref, o_ref, u_next, acc_prev)
        if do_main:
            u = u_cur[...]
            ga = jnp.dot(u, wga_ref[...], preferred_element_type=F32)
            gm = jnp.dot(u, wgm_ref[...], preferred_element_type=F32)
            a = jnp.dot(att_ref[...], wa_ref[...], preferred_element_type=F32)
            m = jnp.dot(hm_ref[...], wm_ref[...], preferred_element_type=F32)
            y = (jax.nn.sigmoid(ga) * a + jax.nn.sigmoid(gm) * m).astype(BF16)
            acc_cur[...] = (jnp.where(j == 0, 0.0, acc_cur[...])
                            + jnp.dot(y, wo_ref[...], preferred_element_type=F32))

    _for_row_roles((u0, u1), (acc0, acc1), body)


def _merge(h, att, hm, pre_g, wga, wgm, wa, wm, wo, post_g):
    n = h.shape[0]
    tm, tn = MERGE_TM, MERGE_TN
    n_tiles, n_steps, n_slices = n // tm, D_MODEL // tn, tm // SLICE_ROWS
    assert n_steps >= n_slices
    lead, lag, step, tile = _pipeline_maps(n_tiles, n_slices, n_steps)
    return pl.pallas_call(
        _merge_kernel,
        name="merge",
        grid=(n_tiles + 2, n_steps),
        in_specs=[
            pl.BlockSpec((SLICE_ROWS, D_MODEL), lead),
            pl.BlockSpec((SLICE_ROWS, D_MODEL), lag),
            pl.BlockSpec((tm, ATT_Q), tile),
            pl.BlockSpec((tm, M_V), tile),
            pl.BlockSpec((1, D_MODEL), lambda r, j: (0, 0)),
            pl.BlockSpec((D_MODEL, tn), lambda r, j: (0, step(r, j))),
            pl.BlockSpec((D_MODEL, tn), lambda r, j: (0, step(r, j))),
            pl.BlockSpec((ATT_Q, tn), lambda r, j: (0, step(r, j))),
            pl.BlockSpec((M_V, tn), lambda r, j: (0, step(r, j))),
            pl.BlockSpec((tn, D_MODEL), lambda r, j: (step(r, j), 0)),
            pl.BlockSpec((1, D_MODEL), lambda r, j: (0, 0)),
        ],
        out_specs=pl.BlockSpec((SLICE_ROWS, D_MODEL), lag),
        out_shape=jax.ShapeDtypeStruct((n, D_MODEL), F32),
        scratch_shapes=[pltpu.VMEM((tm, D_MODEL), BF16), pltpu.VMEM((tm, D_MODEL), BF16),
                        pltpu.VMEM((tm, D_MODEL), F32), pltpu.VMEM((tm, D_MODEL), F32)],
        compiler_params=pltpu.CompilerParams(
            dimension_semantics=("arbitrary", "arbitrary"), vmem_limit_bytes=VMEM_LIMIT),
    )(h, h, att, hm, pre_g, wga, wgm, wa, wm, wo, post_g)


def kernel(x, ffn1_pre_g, ffn1_wg, ffn1_wu, ffn1_wd, ffn1_post_g, mix_pre_g, w_in, conv_w, conv_b,
           attn_sinks, m_igate_b, m_fgate_b, m_head_g, w_attn_up, w_mlstm_up, w_out, mix_post_g,
           ffn2_pre_g, ffn2_wg, ffn2_wu, ffn2_wd, ffn2_post_g):
    batch, seq, _ = x.shape
    h = x.reshape(batch * seq, D_MODEL)
    for l in range(ffn1_wg.shape[0]):
        bf = lambda w: w.astype(BF16)
        gate0 = W_IN_MAIN + GATE_ROWS
        w_all = bf(w_in[l])
        w_ga = w_all[:, gate0:gate0 + D_MODEL]
        w_gm = w_all[:, gate0 + D_MODEL:gate0 + 2 * D_MODEL]
        gbias = jnp.concatenate(
            [m_igate_b[l], m_fgate_b[l], jnp.zeros((GATE_PAD - 2 * M_HEADS,), F32)])[None, :]
        slopes = jnp.exp2(-8.0 * jnp.arange(1, ATT_HEADS + 1, dtype=F32) / ATT_HEADS)
        scal = jnp.stack([attn_sinks[l].astype(F32), slopes])

        h = _ffn(h, ffn1_pre_g[l][None, :], ffn1_wg[l], ffn1_wu[l], ffn1_wd[l],
                 ffn1_post_g[l][None, :])
        zb, zf, gt = _inproj(h, mix_pre_g[l][None, :], w_all)
        att = _attention(zb, scal, batch, seq)
        hm = _mlstm(zb, zf, gt, conv_w[l], conv_b[l][None, :], gbias, m_head_g[l][None, :],
                    batch, seq)
        h = _merge(h, att, hm, mix_pre_g[l][None, :], w_ga, w_gm, bf(w_attn_up[l]),
                   bf(w_mlstm_up[l]), bf(w_out[l]), mix_post_g[l][None, :])
        h = _ffn(h, ffn2_pre_g[l][None, :], ffn2_wg[l], ffn2_wu[l], ffn2_wd[l],
                 ffn2_post_g[l][None, :])
    return h.reshape(batch, seq, D_MODEL)
```

```python
import functools
import math

import jax
import jax.numpy as jnp
from jax import lax
from jax.experimental import pallas as pl
from jax.experimental.pallas import tpu as pltpu

F32 = jnp.float32
BF16 = jnp.bfloat16

D_MODEL = 2048
ATT_HEADS = 16
ATT_KV_HEADS = 4
ATT_HEAD_DIM = 64
ATT_GROUP = ATT_HEADS // ATT_KV_HEADS
WINDOW = 128
ATT_BLOCK = 128
M_HEADS = 4
M_QK_DIM = 128
M_V_DIM = 256
CONV_WIDTH = 4
D_FF = 5632
EPS = 1e-6

ATT_Q = ATT_HEADS * ATT_HEAD_DIM
ATT_KV = ATT_KV_HEADS * ATT_HEAD_DIM
M_QK = M_HEADS * M_QK_DIM
M_V = M_HEADS * M_V_DIM

LANES = 128
GATE_PAD = LANES
GATE_ROWS = 2 * M_HEADS
ZB_COLS = ATT_Q + M_V + 2 * ATT_KV
ZF_COLS = 2 * M_QK + M_V + GATE_PAD

VMEM_LIMIT = 62 * 1024 * 1024

FFN_TM = 1024
FFN_TF = 512
PROJ_TM = 512
PROJ_TN = 512
ATT_TQ = 512
M_CHUNK = 256
M_SEQS = 4
MERGE_TM = 512
MERGE_TN = 512


def _rms(x, g):
    return x * lax.rsqrt(jnp.mean(x * x, axis=-1, keepdims=True) + EPS) * g


def _log_sigmoid(x):
    return jnp.minimum(x, 0.0) - jnp.log1p(jnp.exp(-jnp.abs(x)))


SLICE_ROWS = 128


def _slice_index(f, n_slices):
    return jnp.minimum(f, n_slices - 1)


def _norm_slices(n_slices, res_scale, xa_ref, xb_ref, pre_g_ref, post_g_ref, o_ref, xn_next, acc_prev):
    s = _slice_index(pl.program_id(1), n_slices)
    rows = pl.ds(pl.multiple_of(s * SLICE_ROWS, SLICE_ROWS), SLICE_ROWS)
    xn_next[rows, :] = _rms(xa_ref[...], pre_g_ref[...]).astype(BF16)
    o_ref[...] = xb_ref[...] + res_scale * _rms(acc_prev[rows, :], post_g_ref[...])


def _for_row_roles(xn_bufs, acc_bufs, body):
    r = pl.program_id(0)
    last = pl.num_programs(0) - 1

    @pl.when((r == 0) & (pl.program_id(1) == 0))
    def _():
        for acc in acc_bufs:
            acc[...] = jnp.zeros_like(acc)

    main = (r > 0) & (r < last)
    for p in range(2):
        roles = (xn_bufs[p], acc_bufs[p], xn_bufs[1 - p], acc_bufs[1 - p])
        pl.when(main & (r % 2 == p))(functools.partial(body, *roles, True))
        pl.when(jnp.logical_not(main) & (r % 2 == p))(functools.partial(body, *roles, False))


def _pipeline_maps(n_tiles, n_slices, n_steps):
    def lead(r, f):
        return (jnp.minimum(r, n_tiles - 1) * n_slices + _slice_index(f, n_slices), 0)

    def lag(r, f):
        return (jnp.where(r < 2, 0, (r - 2) * n_slices + _slice_index(f, n_slices)), 0)

    def step(r, f):
        return jnp.where(r == 0, 0, jnp.where(r == n_tiles + 1, n_steps - 1, f))

    def tile(r, f):
        return (jnp.clip(r - 1, 0, n_tiles - 1), 0)

    return lead, lag, step, tile


CAST_ROWS = 16


def _ffn_kernel(has_side, xa_ref, xb_ref, pre_g_ref, wg_ref, wu_ref, wd_ref, post_g_ref, *rest):
    if has_side:
        side_in_ref, o_ref, side_out_ref, xn0, xn1, acc0, acc1 = rest
        side_out_ref[...] = side_in_ref[...].astype(BF16)
    else:
        o_ref, xn0, xn1, acc0, acc1 = rest
    f = pl.program_id(1)

    n_slices = FFN_TM // SLICE_ROWS

    def body(xn_next, acc_prev, xn_cur, acc_cur, do_main):
        _norm_slices(n_slices, 0.5, xa_ref, xb_ref, pre_g_ref, post_g_ref, o_ref, xn_next, acc_prev)
        if do_main:
            xn = xn_cur[...]
            g = jnp.dot(xn, wg_ref[...].astype(BF16), preferred_element_type=F32)
            u = jnp.dot(xn, wu_ref[...].astype(BF16), preferred_element_type=F32)
            hid = (g * jax.nn.sigmoid(g) * u).astype(BF16)
            acc_cur[...] = (jnp.where(f == 0, 0.0, acc_cur[...])
                            + jnp.dot(hid, wd_ref[...].astype(BF16), preferred_element_type=F32))

    _for_row_roles((xn0, xn1), (acc0, acc1), body)


def _ffn(h, pre_g, wg, wu, wd, post_g, side=None):
    n = h.shape[0]
    tm, tf = FFN_TM, FFN_TF
    n_tiles, n_steps, n_slices = n // tm, D_FF // tf, tm // SLICE_ROWS
    assert n_steps >= n_slices
    lead, lag, step, _ = _pipeline_maps(n_tiles, n_slices, n_steps)
    in_specs = [
        pl.BlockSpec((SLICE_ROWS, D_MODEL), lead),
        pl.BlockSpec((SLICE_ROWS, D_MODEL), lag),
        pl.BlockSpec((1, D_MODEL), lambda r, f: (0, 0)),
        pl.BlockSpec((D_MODEL, tf), lambda r, f: (0, step(r, f))),
        pl.BlockSpec((D_MODEL, tf), lambda r, f: (0, step(r, f))),
        pl.BlockSpec((tf, D_MODEL), lambda r, f: (step(r, f), 0)),
        pl.BlockSpec((1, D_MODEL), lambda r, f: (0, 0)),
    ]
    out_specs = [pl.BlockSpec((SLICE_ROWS, D_MODEL), lag)]
    out_shape = [jax.ShapeDtypeStruct((n, D_MODEL), F32)]
    operands = [h, h, pre_g, wg, wu, wd, post_g]
    if side is not None:
        side, cols = side
        rows = side.shape[0]
        n_blocks = rows // CAST_ROWS
        assert rows % CAST_ROWS == 0 and n_blocks <= n_tiles * n_steps

        def cast_map(r, f):
            return (jnp.clip((r - 1) * n_steps + f, 0, n_blocks - 1), 0)

        in_specs.append(pl.BlockSpec((CAST_ROWS, cols), cast_map))
        out_specs.append(pl.BlockSpec((CAST_ROWS, cols), cast_map))
        out_shape.append(jax.ShapeDtypeStruct((rows, cols), BF16))
        operands.append(side)
    outs = pl.pallas_call(
        functools.partial(_ffn_kernel, side is not None),
        name="ffn",
        grid=(n_tiles + 2, n_steps),
        in_specs=in_specs,
        out_specs=out_specs,
        out_shape=out_shape,
        scratch_shapes=[pltpu.VMEM((tm, D_MODEL), BF16), pltpu.VMEM((tm, D_MODEL), BF16),
                        pltpu.VMEM((tm, D_MODEL), F32), pltpu.VMEM((tm, D_MODEL), F32)],
        compiler_params=pltpu.CompilerParams(
            dimension_semantics=("arbitrary", "arbitrary"), vmem_limit_bytes=VMEM_LIMIT),
    )(*operands)
    return outs if side is not None else outs[0]


_OFF_AK = ATT_Q
_OFF_MQ = _OFF_AK + 2 * ATT_KV
_OFF_MV = _OFF_MQ + 2 * M_QK
_OFF_MO = _OFF_MV + M_V
W_IN_MAIN = _OFF_MO + M_V
_PROJ_SEGMENTS = (
    (0, 0, 0, ATT_Q),
    (0, ATT_Q, _OFF_MV, M_V),
    (0, ATT_Q + M_V, _OFF_AK, 2 * ATT_KV),
    (1, 0, _OFF_MQ, 2 * M_QK),
    (1, 2 * M_QK, _OFF_MO, M_V),
)


def _inproj_kernel(h_ref, g_ref, w_ref, wgate_ref, zb_ref, zf_ref, gt_ref, xn_ref):
    xn_ref[...] = _rms(h_ref[...], g_ref[...]).astype(BF16)
    xn = xn_ref[...]
    outs = (zb_ref, zf_ref)
    for which, out0, w0, width in _PROJ_SEGMENTS:
        for c in range(0, width, PROJ_TN):
            z = jnp.dot(xn, w_ref[:, w0 + c:w0 + c + PROJ_TN], preferred_element_type=F32)
            outs[which][:, out0 + c:out0 + c + PROJ_TN] = z.astype(outs[which].dtype)
    zg = jnp.dot(xn, wgate_ref[...], preferred_element_type=F32)
    zf_ref[:, 2 * M_QK + M_V:] = zg
    gt_ref[...] = zg.T[0:GATE_ROWS, :]


def _inproj(h, g, w_all):
    n = h.shape[0]
    tm = PROJ_TM
    return pl.pallas_call(
        _inproj_kernel,
        name="inproj",
        grid=(n // tm,),
        in_specs=[
            pl.BlockSpec((tm, D_MODEL), lambda i: (i, 0)),
            pl.BlockSpec((1, D_MODEL), lambda i: (0, 0)),
            pl.BlockSpec((D_MODEL, W_IN_MAIN), lambda i: (0, 0), pipeline_mode=pl.Buffered(1)),
            pl.BlockSpec((D_MODEL, GATE_PAD), lambda i: (0, W_IN_MAIN // GATE_PAD),
                         pipeline_mode=pl.Buffered(1)),
        ],
        out_specs=[
            pl.BlockSpec((tm, ZB_COLS), lambda i: (i, 0)),
            pl.BlockSpec((tm, ZF_COLS), lambda i: (i, 0)),
            pl.BlockSpec((GATE_ROWS, tm), lambda i: (0, i)),
        ],
        out_shape=[jax.ShapeDtypeStruct((n, ZB_COLS), BF16),
                   jax.ShapeDtypeStruct((n, ZF_COLS), F32),
                   jax.ShapeDtypeStruct((GATE_ROWS, n), F32)],
        scratch_shapes=[pltpu.VMEM((tm, D_MODEL), BF16)],
        compiler_params=pltpu.CompilerParams(
            dimension_semantics=("parallel",), vmem_limit_bytes=VMEM_LIMIT),
    )(h, g, w_all, w_all)


def _attn_kernel(n_side, scal_ref, q_ref, kp_ref, kc_ref, vp_ref, vc_ref, *rest):
    side_in, o_ref, side_out, bias_ref = (rest[:n_side], rest[n_side], rest[n_side + 1:-1], rest[-1])
    for src, dst in zip(side_in, side_out):
        dst[...] = src[...].astype(BF16)
    t = pl.program_id(1)
    blk = ATT_BLOCK
    half = ATT_HEAD_DIM
    scale = 1.0 / math.sqrt(ATT_HEAD_DIM)
    neg_inf = jnp.float32(-jnp.inf)

    qi = lax.broadcasted_iota(jnp.int32, (blk, 2 * blk), 0)
    kj = lax.broadcasted_iota(jnp.int32, (blk, 2 * blk), 1)

    @pl.when(t == 0)
    def _():
        dist = qi - kj + blk
        band = (dist >= 0) & (dist < WINDOW)
        distf = dist.astype(F32)
        for h in range(ATT_HEADS):
            bias_ref[h] = jnp.where(band, -(scal_ref[1, h] * distf), neg_inf)

    kk = jnp.concatenate([kp_ref[...], kc_ref[...]], axis=0).astype(F32) * scale
    vv = jnp.concatenate([vp_ref[...], vc_ref[...]], axis=0).astype(F32)
    rows = kk.shape[0]
    lane = lax.broadcasted_iota(jnp.int32, (rows, LANES), 1)
    low = lane < half

    def split_pair(x):
        xr = pltpu.roll(x, half, axis=1)
        zero = jnp.zeros_like(x)
        first = (jnp.where(low, x, zero).astype(BF16), jnp.where(low, zero, xr).astype(BF16))
        second = (jnp.where(low, xr, zero).astype(BF16), jnp.where(low, zero, x).astype(BF16))
        return first, second

    k_lo_hi, v_lo_hi = [], []
    for pair in range(ATT_KV // LANES):
        ka, kb = split_pair(kk[:, pair * LANES:(pair + 1) * LANES])
        va, vb = split_pair(vv[:, pair * LANES:(pair + 1) * LANES])
        k_lo_hi += [ka, kb]
        v_lo_hi += [va, vb]

    lane_q = lax.broadcasted_iota(jnp.int32, (blk, LANES), 1)
    has_prev = (kj >= blk) | (t > 0)

    low_q = lane_q < half
    pairs = [(g, pr) for g in range(ATT_KV_HEADS) for pr in range(ATT_GROUP // 2)]
    for j in range(ATT_TQ // blk):
        r0 = j * blk
        k2 = [jnp.concatenate([lo[r0:r0 + 2 * blk], hi[r0:r0 + 2 * blk]], axis=0)
              for lo, hi in k_lo_hi]
        v2 = [jnp.concatenate([lo[r0:r0 + 2 * blk], hi[r0:r0 + 2 * blk]], axis=0)
              for lo, hi in v_lo_hi]
        s2 = []
        for g, pr in pairs:
            c0 = (g * ATT_GROUP + 2 * pr) * half
            s2.append(lax.dot_general(q_ref[r0:r0 + blk, c0:c0 + LANES], k2[g],
                                      (((1,), (1,)), ((), ())), preferred_element_type=F32))
        p2, inv2 = [], []
        for idx, (g, pr) in enumerate(pairs):
            h0 = g * ATT_GROUP + 2 * pr
            ps, ms, sums = [], [], []
            for e in range(2):
                s = s2[idx][:, e * 2 * blk:(e + 1) * 2 * blk] + bias_ref[h0 + e]
                if j == 0:
                    s = jnp.where(has_prev, s, neg_inf)
                m = jnp.maximum(jnp.max(s, axis=-1, keepdims=True), scal_ref[0, h0 + e])
                p = jnp.exp(s - m)
                ps.append(p.astype(BF16))
                ms.append(scal_ref[0, h0 + e] - m)
                sums.append(jnp.sum(p, axis=-1, keepdims=True))
            den = jnp.where(low_q, sums[0], sums[1]) + jnp.exp(jnp.where(low_q, ms[0], ms[1]))
            inv2.append(1.0 / den)
            p2.append(jnp.concatenate(ps, axis=1))
        for idx, (g, pr) in enumerate(pairs):
            c0 = (g * ATT_GROUP + 2 * pr) * half
            o2 = jnp.dot(p2[idx], v2[g], preferred_element_type=F32) * inv2[idx]
            o_ref[r0:r0 + blk, c0:c0 + LANES] = o2.astype(BF16)


def _attention(zb, scal, batch, seq, side=()):
    n = zb.shape[0]
    tq = ATT_TQ
    nt = seq // tq
    per = tq // ATT_BLOCK
    kcol = (ATT_Q + M_V) // ATT_KV
    vcol = kcol + 1

    def prev_map(col):
        return lambda b, t: (b * (seq // ATT_BLOCK) + jnp.maximum(t * per - 1, 0), col)

    def cur_map(col):
        return lambda b, t: (b * nt + t, col)

    side_specs = []
    for w in side:
        rows = w.shape[0] // (batch * nt)
        assert rows % CAST_ROWS == 0 and rows * batch * nt == w.shape[0]
        side_specs.append(pl.BlockSpec((rows, w.shape[1]), cur_map(0)))
    outs = pl.pallas_call(
        functools.partial(_attn_kernel, len(side)),
        name="attn",
        grid=(batch, nt),
        in_specs=[
            pl.BlockSpec(memory_space=pltpu.SMEM),
            pl.BlockSpec((tq, ATT_Q), cur_map(0)),
            pl.BlockSpec((ATT_BLOCK, ATT_KV), prev_map(kcol)),
            pl.BlockSpec((tq, ATT_KV), cur_map(kcol)),
            pl.BlockSpec((ATT_BLOCK, ATT_KV), prev_map(vcol)),
            pl.BlockSpec((tq, ATT_KV), cur_map(vcol)),
            *side_specs,
        ],
        out_specs=[pl.BlockSpec((tq, ATT_Q), cur_map(0)), *side_specs],
        out_shape=[jax.ShapeDtypeStruct((n, ATT_Q), BF16),
                   *[jax.ShapeDtypeStruct(w.shape, BF16) for w in side]],
        scratch_shapes=[pltpu.VMEM((ATT_HEADS, ATT_BLOCK, 2 * ATT_BLOCK), F32)],
        compiler_params=pltpu.CompilerParams(
            dimension_semantics=("parallel", "arbitrary"), vmem_limit_bytes=VMEM_LIMIT),
    )(scal, zb, zb, zb, zb, zb, *side)
    return outs


def _mlstm_kernel(qk_ref, v_ref, og_ref, gt_ref, *rest):
    gtt_refs = rest[:M_SEQS]
    (convw_ref, convb_ref, gbias_ref, gbias_col_ref, headg_ref,
     o_ref, xbuf_ref, c_ref, n_ref, m_ref) = rest[M_SEQS:]
    L = M_CHUNK
    pad = 8

    @pl.when(pl.program_id(1) == 0)
    def _():
        xbuf_ref[:, 0:pad, :] = jnp.zeros((M_SEQS, pad, 2 * M_QK), F32)
        c_ref[...] = jnp.zeros_like(c_ref)
        n_ref[...] = jnp.zeros_like(n_ref)
        m_ref[...] = jnp.zeros_like(m_ref)

    w = convw_ref[...]
    base = pad - (CONV_WIDTH - 1)
    qk, gl, lf, gl_t, lf_t = [], [], [], [], []
    for b in range(M_SEQS):
        xbuf_ref[b, pad:pad + L, :] = qk_ref[b]
        y = xbuf_ref[b, pad:pad + L, :] * w[CONV_WIDTH - 1:CONV_WIDTH, :] + convb_ref[...]
        for j in range(CONV_WIDTH - 1):
            y = y + xbuf_ref[b, pl.ds(base + j, L), :] * w[j:j + 1, :]
        xbuf_ref[b, 0:pad, :] = xbuf_ref[b, L:L + pad, :]
        qk.append(y * jax.nn.sigmoid(y))
        gl.append(gt_ref[b] + gbias_ref[...])
        lf.append(_log_sigmoid(gl[b]))
        gl_t.append(gtt_refs[b][...] + gbias_col_ref[...])
        lf_t.append(_log_sigmoid(gl_t[b]))

    ti = lax.broadcasted_iota(jnp.int32, (L, L), 0)
    si = lax.broadcasted_iota(jnp.int32, (L, L), 1)
    tril = si <= ti
    triu = ti <= si
    neg_inf = jnp.float32(-jnp.inf)
    kscale = 1.0 / math.sqrt(M_QK_DIM)

    units = [(b, h) for b in range(M_SEQS) for h in range(M_HEADS)]
    heads = range(len(units))
    q_f = [qk[b][:, h * M_QK_DIM:(h + 1) * M_QK_DIM] for b, h in units]
    k_f = [qk[b][:, M_QK + h * M_QK_DIM:M_QK + (h + 1) * M_QK_DIM] * kscale for b, h in units]
    v_b = [v_ref[b, :, h * M_V_DIM:(h + 1) * M_V_DIM] for b, h in units]
    q_b = [q.astype(BF16) for q in q_f]
    k_b = [k.astype(BF16) for k in k_f]
    m_in = [m_ref[u, 0:1, 0:1] for u in heads]
    c_in = [c_ref[u] for u in heads]
    n_in = [n_ref[u] for u in heads]

    b_col, dmat, m_t, inter_w, pexp, kwt, decay, m_new = [], [], [], [], [], [], [], []
    for b, h in units:
        li_col = gl[b][:, h:h + 1]
        lf_col = lf[b][:, M_HEADS + h:M_HEADS + h + 1]
        li_row = gl_t[b][h:h + 1, :]
        lf_row = lf_t[b][M_HEADS + h:M_HEADS + h + 1, :]
        h = b * M_HEADS + h
        bc = jnp.sum(jnp.where(tril, lf_row, 0.0), axis=1, keepdims=True)
        br = jnp.sum(jnp.where(triu, lf_col, 0.0), axis=0, keepdims=True)
        b_last = bc[L - 1:L, :]
        dm = jnp.where(tril, bc - br + li_row, neg_inf)
        a_col = bc + m_in[h]
        mt = jnp.maximum(a_col, jnp.max(dm, axis=1, keepdims=True))
        inter_w.append(jnp.exp(a_col - mt))
        pexp.append(jnp.exp(dm - mt))
        m_t.append(mt)
        g_col = b_last - bc + li_col
        mn = jnp.maximum(b_last + m_in[h], jnp.max(g_col, axis=0, keepdims=True))
        decay.append(jnp.exp(b_last + m_in[h] - mn))
        kwt.append(jnp.exp(g_col - mn))
        m_new.append(mn)

    p = [lax.dot_general(q_b[h], k_b[h], (((1,), (1,)), ((), ())), preferred_element_type=F32)
         * pexp[h] for h in heads]
    hh = []
    for h in heads:
        num = (jnp.dot(p[h].astype(BF16), v_b[h], preferred_element_type=F32)
               + inter_w[h] * jnp.dot(q_b[h], c_in[h].astype(BF16), preferred_element_type=F32))
        den = (jnp.sum(p[h], axis=1, keepdims=True)
               + inter_w[h] * jnp.sum(q_f[h] * n_in[h], axis=1, keepdims=True))
        hh.append(num / jnp.maximum(jnp.abs(den), jnp.exp(-m_t[h])))
    for h in heads:
        kw = k_f[h] * kwt[h]
        c_ref[h] = decay[h] * c_in[h] + lax.dot_general(
            kw.astype(BF16), v_b[h], (((0,), (0,)), ((), ())), preferred_element_type=F32)
        n_ref[h] = decay[h] * n_in[h] + jnp.sum(kw, axis=0, keepdims=True)
        m_ref[h] = jnp.broadcast_to(m_new[h], m_ref.shape[1:])
    for u, (b, h) in enumerate(units):
        cs = slice(h * M_V_DIM, (h + 1) * M_V_DIM)
        hn = _rms(hh[u], headg_ref[:, cs])
        o_ref[b, :, cs] = (jax.nn.sigmoid(og_ref[b, :, cs]) * hn).astype(BF16)


def _mlstm(zb, zf, gt, conv_w, conv_b, gbias, head_g, batch, seq):
    n = zb.shape[0]
    L = M_CHUNK
    nc = seq // L
    nb = M_SEQS
    assert batch % nb == 0
    zb3 = zb.reshape(batch, seq, ZB_COLS)
    zf3 = zf.reshape(batch, seq, ZF_COLS)
    const = lambda g, c: (0, 0)

    def gtt_spec(b):
        return pl.BlockSpec((GATE_ROWS, L), lambda g, c: (0, (g * nb + b) * nc + c))

    out = pl.pallas_call(
        _mlstm_kernel,
        name="mlstm",
        grid=(batch // nb, nc),
        in_specs=[
            pl.BlockSpec((nb, L, 2 * M_QK), lambda g, c: (g, c, 0)),
            pl.BlockSpec((nb, L, M_V), lambda g, c: (g, c, ATT_Q // M_V)),
            pl.BlockSpec((nb, L, M_V), lambda g, c: (g, c, 2 * M_QK // M_V)),
            pl.BlockSpec((nb, L, GATE_PAD), lambda g, c: (g, c, (2 * M_QK + M_V) // GATE_PAD)),
            *[gtt_spec(b) for b in range(nb)],
            pl.BlockSpec((CONV_WIDTH, 2 * M_QK), const),
            pl.BlockSpec((1, 2 * M_QK), const),
            pl.BlockSpec((1, GATE_PAD), const),
            pl.BlockSpec((GATE_ROWS, 1), const),
            pl.BlockSpec((1, M_V), const),
        ],
        out_specs=pl.BlockSpec((nb, L, M_V), lambda g, c: (g, c, 0)),
        out_shape=jax.ShapeDtypeStruct((batch, seq, M_V), BF16),
        scratch_shapes=[
            pltpu.VMEM((nb, L + 8, 2 * M_QK), F32),
            pltpu.VMEM((nb * M_HEADS, M_QK_DIM, M_V_DIM), F32),
            pltpu.VMEM((nb * M_HEADS, 1, M_QK_DIM), F32),
            pltpu.VMEM((nb * M_HEADS, 8, LANES), F32),
        ],
        compiler_params=pltpu.CompilerParams(
            dimension_semantics=("parallel", "arbitrary"), vmem_limit_bytes=VMEM_LIMIT),
    )(zf3, zb3, zf3, zf3, *([gt] * nb), conv_w, conv_b, gbias, gbias[0, :GATE_ROWS, None], head_g)
    return out.reshape(n, M_V)


def _merge_kernel(ha_ref, hb_ref, att_ref, hm_ref, pre_g_ref, wga_ref, wgm_ref, wa_ref, wm_ref, wo_ref,
                  post_g_ref, o_ref, u0, u1, acc0, acc1):
    j = pl.program_id(1)

    n_slices = MERGE_TM // SLICE_ROWS

    def body(u_next, acc_prev, u_cur, acc_cur, do_main):
        _norm_slices(n_slices, 1.0, ha_ref, hb_ref, pre_g_ref, post_g_ref, o_ref, u_next, acc_prev)
        if do_main:
            u = u_cur[...]
            ga = jnp.dot(u, wga_ref[...], preferred_element_type=F32)
            gm = jnp.dot(u, wgm_ref[...], preferred_element_type=F32)
            a = jnp.dot(att_ref[...], wa_ref[...], preferred_element_type=F32)
            m = jnp.dot(hm_ref[...], wm_ref[...], preferred_element_type=F32)
            y = (jax.nn.sigmoid(ga) * a + jax.nn.sigmoid(gm) * m).astype(BF16)
            acc_cur[...] = (jnp.where(j == 0, 0.0, acc_cur[...])
                            + jnp.dot(y, wo_ref[...], preferred_element_type=F32))

    _for_row_roles((u0, u1), (acc0, acc1), body)


def _merge(h, att, hm, pre_g, wga, wgm, wa, wm, wo, post_g):
    n = h.shape[0]
    tm, tn = MERGE_TM, MERGE_TN
    n_tiles, n_steps, n_slices = n // tm, D_MODEL // tn, tm // SLICE_ROWS
    assert n_steps >= n_slices
    lead, lag, step, tile = _pipeline_maps(n_tiles, n_slices, n_steps)
    return pl.pallas_call(
        _merge_kernel,
        name="merge",
        grid=(n_tiles + 2, n_steps),
        in_specs=[
            pl.BlockSpec((SLICE_ROWS, D_MODEL), lead),
            pl.BlockSpec((SLICE_ROWS, D_MODEL), lag),
            pl.BlockSpec((tm, ATT_Q), tile),
            pl.BlockSpec((tm, M_V), tile),
            pl.BlockSpec((1, D_MODEL), lambda r, j: (0, 0)),
            pl.BlockSpec((D_MODEL, tn), lambda r, j: (0, step(r, j))),
            pl.BlockSpec((D_MODEL, tn), lambda r, j: (0, step(r, j))),
            pl.BlockSpec((ATT_Q, tn), lambda r, j: (0, step(r, j))),
            pl.BlockSpec((M_V, tn), lambda r, j: (0, step(r, j))),
            pl.BlockSpec((tn, D_MODEL), lambda r, j: (step(r, j), 0)),
            pl.BlockSpec((1, D_MODEL), lambda r, j: (0, 0)),
        ],
        out_specs=pl.BlockSpec((SLICE_ROWS, D_MODEL), lag),
        out_shape=jax.ShapeDtypeStruct((n, D_MODEL), F32),
        scratch_shapes=[pltpu.VMEM((tm, D_MODEL), BF16), pltpu.VMEM((tm, D_MODEL), BF16),
                        pltpu.VMEM((tm, D_MODEL), F32), pltpu.VMEM((tm, D_MODEL), F32)],
        compiler_params=pltpu.CompilerParams(
            dimension_semantics=("arbitrary", "arbitrary"), vmem_limit_bytes=VMEM_LIMIT),
    )(h, h, att, hm, pre_g, wga, wgm, wa, wm, wo, post_g)


def kernel(x, ffn1_pre_g, ffn1_wg, ffn1_wu, ffn1_wd, ffn1_post_g, mix_pre_g, w_in, conv_w, conv_b,
           attn_sinks, m_igate_b, m_fgate_b, m_head_g, w_attn_up, w_mlstm_up, w_out, mix_post_g,
           ffn2_pre_g, ffn2_wg, ffn2_wu, ffn2_wd, ffn2_post_g):
    batch, seq, _ = x.shape
    h = x.reshape(batch * seq, D_MODEL)
    for l in range(ffn1_wg.shape[0]):
        bf = lambda w: w.astype(BF16)
        gate0 = W_IN_MAIN + GATE_ROWS
        w_ga = bf(w_in[l][:, gate0:gate0 + D_MODEL])
        w_gm = bf(w_in[l][:, gate0 + D_MODEL:gate0 + 2 * D_MODEL])
        gbias = jnp.concatenate(
            [m_igate_b[l], m_fgate_b[l], jnp.zeros((GATE_PAD - 2 * M_HEADS,), F32)])[None, :]
        slopes = jnp.exp2(-8.0 * jnp.arange(1, ATT_HEADS + 1, dtype=F32) / ATT_HEADS)
        scal = jnp.stack([attn_sinks[l].astype(F32), slopes])

        h, w_proj = _ffn(h, ffn1_pre_g[l][None, :], ffn1_wg[l], ffn1_wu[l], ffn1_wd[l],
                         ffn1_post_g[l][None, :], side=(w_in[l], W_IN_MAIN + GATE_PAD))
        zb, zf, gt = _inproj(h, mix_pre_g[l][None, :], w_proj)
        att, w_au, w_mu, w_o = _attention(zb, scal, batch, seq,
                                          side=(w_attn_up[l], w_mlstm_up[l], w_out[l]))
        hm = _mlstm(zb, zf, gt, conv_w[l], conv_b[l][None, :], gbias, m_head_g[l][None, :],
                    batch, seq)
        h = _merge(h, att, hm, mix_pre_g[l][None, :], w_ga, w_gm, w_au, w_mu, w_o,
                   mix_post_g[l][None, :])
        h = _ffn(h, ffn2_pre_g[l][None, :], ffn2_wg[l], ffn2_wu[l], ffn2_wd[l],
                 ffn2_post_g[l][None, :])
    return h.reshape(batch, seq, D_MODEL)
```

```python
import functools
import math

import jax
import jax.numpy as jnp
from jax import lax
from jax.experimental import pallas as pl
from jax.experimental.pallas import tpu as pltpu

F32 = jnp.float32
BF16 = jnp.bfloat16

D_MODEL = 2048
ATT_HEADS = 16
ATT_KV_HEADS = 4
ATT_HEAD_DIM = 64
ATT_GROUP = ATT_HEADS // ATT_KV_HEADS
WINDOW = 128
ATT_BLOCK = 128
M_HEADS = 4
M_QK_DIM = 128
M_V_DIM = 256
CONV_WIDTH = 4
D_FF = 5632
EPS = 1e-6

ATT_Q = ATT_HEADS * ATT_HEAD_DIM
ATT_KV = ATT_KV_HEADS * ATT_HEAD_DIM
M_QK = M_HEADS * M_QK_DIM
M_V = M_HEADS * M_V_DIM

LANES = 128
GATE_PAD = LANES
GATE_ROWS = 2 * M_HEADS
ZB_COLS = ATT_Q + M_V + 2 * ATT_KV
ZF_COLS = 2 * M_QK + M_V + GATE_PAD

VMEM_LIMIT = 62 * 1024 * 1024

FFN_TM = 1024
FFN_TF = 512
PROJ_TM = 512
PROJ_TN = 512
ATT_TQ = 512
M_CHUNK = 256
M_SEQS = 4
MERGE_TM = 512
MERGE_TN = 512


def _rms(x, g):
    return x * lax.rsqrt(jnp.mean(x * x, axis=-1, keepdims=True) + EPS) * g


def _log_sigmoid(x):
    return jnp.minimum(x, 0.0) - jnp.log1p(jnp.exp(-jnp.abs(x)))


SLICE_ROWS = 128


def _slice_index(f, n_slices):
    return jnp.minimum(f, n_slices - 1)


def _norm_slices(n_slices, res_scale, xa_ref, xb_ref, pre_g_ref, post_g_ref, o_ref, xn_next, acc_prev):
    s = _slice_index(pl.program_id(1), n_slices)
    rows = pl.ds(pl.multiple_of(s * SLICE_ROWS, SLICE_ROWS), SLICE_ROWS)
    xn_next[rows, :] = _rms(xa_ref[...], pre_g_ref[...]).astype(BF16)
    o_ref[...] = xb_ref[...] + res_scale * _rms(acc_prev[rows, :], post_g_ref[...])


def _for_row_roles(xn_bufs, acc_bufs, body):
    r = pl.program_id(0)
    last = pl.num_programs(0) - 1

    @pl.when((r == 0) & (pl.program_id(1) == 0))
    def _():
        for acc in acc_bufs:
            acc[...] = jnp.zeros_like(acc)

    main = (r > 0) & (r < last)
    for p in range(2):
        roles = (xn_bufs[p], acc_bufs[p], xn_bufs[1 - p], acc_bufs[1 - p])
        pl.when(main & (r % 2 == p))(functools.partial(body, *roles, True))
        pl.when(jnp.logical_not(main) & (r % 2 == p))(functools.partial(body, *roles, False))


def _pipeline_maps(n_tiles, n_slices, n_steps):
    def lead(r, f):
        return (jnp.minimum(r, n_tiles - 1) * n_slices + _slice_index(f, n_slices), 0)

    def lag(r, f):
        return (jnp.where(r < 2, 0, (r - 2) * n_slices + _slice_index(f, n_slices)), 0)

    def step(r, f):
        return jnp.where(r == 0, 0, jnp.where(r == n_tiles + 1, n_steps - 1, f))

    def tile(r, f):
        return (jnp.clip(r - 1, 0, n_tiles - 1), 0)

    return lead, lag, step, tile


CAST_ROWS = 16


def _ffn_kernel(has_side, xa_ref, xb_ref, pre_g_ref, wg_ref, wu_ref, wd_ref, post_g_ref, *rest):
    if has_side:
        side_in_ref, o_ref, side_out_ref, xn0, xn1, acc0, acc1 = rest
        side_out_ref[...] = side_in_ref[...].astype(BF16)
    else:
        o_ref, xn0, xn1, acc0, acc1 = rest
    f = pl.program_id(1)

    n_slices = FFN_TM // SLICE_ROWS

    def body(xn_next, acc_prev, xn_cur, acc_cur, do_main):
        _norm_slices(n_slices, 0.5, xa_ref, xb_ref, pre_g_ref, post_g_ref, o_ref, xn_next, acc_prev)
        if do_main:
            xn = xn_cur[...]
            g = jnp.dot(xn, wg_ref[...].astype(BF16), preferred_element_type=F32)
            u = jnp.dot(xn, wu_ref[...].astype(BF16), preferred_element_type=F32)
            hid = (g * jax.nn.sigmoid(g) * u).astype(BF16)
            acc_cur[...] = (jnp.where(f == 0, 0.0, acc_cur[...])
                            + jnp.dot(hid, wd_ref[...].astype(BF16), preferred_element_type=F32))

    _for_row_roles((xn0, xn1), (acc0, acc1), body)


def _ffn(h, pre_g, wg, wu, wd, post_g, side=None):
    n = h.shape[0]
    tm, tf = FFN_TM, FFN_TF
    n_tiles, n_steps, n_slices = n // tm, D_FF // tf, tm // SLICE_ROWS
    assert n_steps >= n_slices
    lead, lag, step, _ = _pipeline_maps(n_tiles, n_slices, n_steps)
    in_specs = [
        pl.BlockSpec((SLICE_ROWS, D_MODEL), lead),
        pl.BlockSpec((SLICE_ROWS, D_MODEL), lag),
        pl.BlockSpec((1, D_MODEL), lambda r, f: (0, 0)),
        pl.BlockSpec((D_MODEL, tf), lambda r, f: (0, step(r, f))),
        pl.BlockSpec((D_MODEL, tf), lambda r, f: (0, step(r, f))),
        pl.BlockSpec((tf, D_MODEL), lambda r, f: (step(r, f), 0)),
        pl.BlockSpec((1, D_MODEL), lambda r, f: (0, 0)),
    ]
    out_specs = [pl.BlockSpec((SLICE_ROWS, D_MODEL), lag)]
    out_shape = [jax.ShapeDtypeStruct((n, D_MODEL), F32)]
    operands = [h, h, pre_g, wg, wu, wd, post_g]
    if side is not None:
        side, cols = side
        rows = side.shape[0]
        n_blocks = rows // CAST_ROWS
        assert rows % CAST_ROWS == 0 and n_blocks <= n_tiles * n_steps

        def cast_map(r, f):
            return (jnp.clip((r - 1) * n_steps + f, 0, n_blocks - 1), 0)

        in_specs.append(pl.BlockSpec((CAST_ROWS, cols), cast_map))
        out_specs.append(pl.BlockSpec((CAST_ROWS, cols), cast_map))
        out_shape.append(jax.ShapeDtypeStruct((rows, cols), BF16))
        operands.append(side)
    outs = pl.pallas_call(
        functools.partial(_ffn_kernel, side is not None),
        name="ffn",
        grid=(n_tiles + 2, n_steps),
        in_specs=in_specs,
        out_specs=out_specs,
        out_shape=out_shape,
        scratch_shapes=[pltpu.VMEM((tm, D_MODEL), BF16), pltpu.VMEM((tm, D_MODEL), BF16),
                        pltpu.VMEM((tm, D_MODEL), F32), pltpu.VMEM((tm, D_MODEL), F32)],
        compiler_params=pltpu.CompilerParams(
            dimension_semantics=("arbitrary", "arbitrary"), vmem_limit_bytes=VMEM_LIMIT),
    )(*operands)
    return outs if side is not None else outs[0]


_OFF_AK = ATT_Q
_OFF_MQ = _OFF_AK + 2 * ATT_KV
_OFF_MV = _OFF_MQ + 2 * M_QK
_OFF_MO = _OFF_MV + M_V
W_IN_MAIN = _OFF_MO + M_V
_PROJ_SEGMENTS = (
    (0, 0, 0, ATT_Q),
    (0, ATT_Q, _OFF_MV, M_V),
    (0, ATT_Q + M_V, _OFF_AK, 2 * ATT_KV),
    (1, 0, _OFF_MQ, 2 * M_QK),
    (1, 2 * M_QK, _OFF_MO, M_V),
)


def _inproj_kernel(h_ref, g_ref, w_ref, wgate_ref, side_in_ref, zb_ref, zf_ref, gt_ref, side_out_ref,
                   xn_ref):
    side_out_ref[...] = side_in_ref[...].astype(BF16)
    xn_ref[...] = _rms(h_ref[...], g_ref[...]).astype(BF16)
    xn = xn_ref[...]
    outs = (zb_ref, zf_ref)
    for which, out0, w0, width in _PROJ_SEGMENTS:
        for c in range(0, width, PROJ_TN):
            z = jnp.dot(xn, w_ref[:, w0 + c:w0 + c + PROJ_TN], preferred_element_type=F32)
            outs[which][:, out0 + c:out0 + c + PROJ_TN] = z.astype(outs[which].dtype)
    zg = jnp.dot(xn, wgate_ref[...], preferred_element_type=F32)
    zf_ref[:, 2 * M_QK + M_V:] = zg
    gt_ref[...] = zg.T[0:GATE_ROWS, :]


def _inproj(h, g, w_all, side):
    n = h.shape[0]
    tm = PROJ_TM
    steps = n // tm
    side_rows = side.shape[0] // steps
    assert side_rows % CAST_ROWS == 0 and side_rows * steps == side.shape[0]
    side_spec = pl.BlockSpec((side_rows, side.shape[1]), lambda i: (i, 0))
    return pl.pallas_call(
        _inproj_kernel,
        name="inproj",
        grid=(steps,),
        in_specs=[
            pl.BlockSpec((tm, D_MODEL), lambda i: (i, 0)),
            pl.BlockSpec((1, D_MODEL), lambda i: (0, 0)),
            pl.BlockSpec((D_MODEL, W_IN_MAIN), lambda i: (0, 0), pipeline_mode=pl.Buffered(1)),
            pl.BlockSpec((D_MODEL, GATE_PAD), lambda i: (0, W_IN_MAIN // GATE_PAD),
                         pipeline_mode=pl.Buffered(1)),
            side_spec,
        ],
        out_specs=[
            pl.BlockSpec((tm, ZB_COLS), lambda i: (i, 0)),
            pl.BlockSpec((tm, ZF_COLS), lambda i: (i, 0)),
            pl.BlockSpec((GATE_ROWS, tm), lambda i: (0, i)),
            side_spec,
        ],
        out_shape=[jax.ShapeDtypeStruct((n, ZB_COLS), BF16),
                   jax.ShapeDtypeStruct((n, ZF_COLS), F32),
                   jax.ShapeDtypeStruct((GATE_ROWS, n), F32),
                   jax.ShapeDtypeStruct(side.shape, BF16)],
        scratch_shapes=[pltpu.VMEM((tm, D_MODEL), BF16)],
        compiler_params=pltpu.CompilerParams(
            dimension_semantics=("parallel",), vmem_limit_bytes=VMEM_LIMIT),
    )(h, g, w_all, w_all, side)


def _attn_kernel(n_side, scal_ref, q_ref, kp_ref, kc_ref, vp_ref, vc_ref, *rest):
    side_in, o_ref, side_out, bias_ref = (rest[:n_side], rest[n_side], rest[n_side + 1:-1], rest[-1])
    for src, dst in zip(side_in, side_out):
        dst[...] = src[...].astype(BF16)
    t = pl.program_id(1)
    blk = ATT_BLOCK
    half = ATT_HEAD_DIM
    scale = 1.0 / math.sqrt(ATT_HEAD_DIM)
    neg_inf = jnp.float32(-jnp.inf)

    qi = lax.broadcasted_iota(jnp.int32, (blk, 2 * blk), 0)
    kj = lax.broadcasted_iota(jnp.int32, (blk, 2 * blk), 1)

    @pl.when(t == 0)
    def _():
        dist = qi - kj + blk
        band = (dist >= 0) & (dist < WINDOW)
        distf = dist.astype(F32)
        for h in range(ATT_HEADS):
            bias_ref[h] = jnp.where(band, -(scal_ref[1, h] * distf), neg_inf)

    kk = jnp.concatenate([kp_ref[...], kc_ref[...]], axis=0).astype(F32) * scale
    vv = jnp.concatenate([vp_ref[...], vc_ref[...]], axis=0).astype(F32)
    rows = kk.shape[0]
    lane = lax.broadcasted_iota(jnp.int32, (rows, LANES), 1)
    low = lane < half

    def split_pair(x):
        xr = pltpu.roll(x, half, axis=1)
        zero = jnp.zeros_like(x)
        first = (jnp.where(low, x, zero).astype(BF16), jnp.where(low, zero, xr).astype(BF16))
        second = (jnp.where(low, xr, zero).astype(BF16), jnp.where(low, zero, x).astype(BF16))
        return first, second

    k_lo_hi, v_lo_hi = [], []
    for pair in range(ATT_KV // LANES):
        ka, kb = split_pair(kk[:, pair * LANES:(pair + 1) * LANES])
        va, vb = split_pair(vv[:, pair * LANES:(pair + 1) * LANES])
        k_lo_hi += [ka, kb]
        v_lo_hi += [va, vb]

    lane_q = lax.broadcasted_iota(jnp.int32, (blk, LANES), 1)
    has_prev = (kj >= blk) | (t > 0)

    low_q = lane_q < half
    pairs = [(g, pr) for g in range(ATT_KV_HEADS) for pr in range(ATT_GROUP // 2)]
    for j in range(ATT_TQ // blk):
        r0 = j * blk
        k2 = [jnp.concatenate([lo[r0:r0 + 2 * blk], hi[r0:r0 + 2 * blk]], axis=0)
              for lo, hi in k_lo_hi]
        v2 = [jnp.concatenate([lo[r0:r0 + 2 * blk], hi[r0:r0 + 2 * blk]], axis=0)
              for lo, hi in v_lo_hi]
        s2 = []
        for g, pr in pairs:
            c0 = (g * ATT_GROUP + 2 * pr) * half
            s2.append(lax.dot_general(q_ref[r0:r0 + blk, c0:c0 + LANES], k2[g],
                                      (((1,), (1,)), ((), ())), preferred_element_type=F32))
        p2, inv2 = [], []
        for idx, (g, pr) in enumerate(pairs):
            h0 = g * ATT_GROUP + 2 * pr
            ps, ms, sums = [], [], []
            for e in range(2):
                s = s2[idx][:, e * 2 * blk:(e + 1) * 2 * blk] + bias_ref[h0 + e]
                if j == 0:
                    s = jnp.where(has_prev, s, neg_inf)
                m = jnp.maximum(jnp.max(s, axis=-1, keepdims=True), scal_ref[0, h0 + e])
                p = jnp.exp(s - m)
                ps.append(p.astype(BF16))
                ms.append(scal_ref[0, h0 + e] - m)
                sums.append(jnp.sum(p, axis=-1, keepdims=True))
            den = jnp.where(low_q, sums[0], sums[1]) + jnp.exp(jnp.where(low_q, ms[0], ms[1]))
            inv2.append(1.0 / den)
            p2.append(jnp.concatenate(ps, axis=1))
        for idx, (g, pr) in enumerate(pairs):
            c0 = (g * ATT_GROUP + 2 * pr) * half
            o2 = jnp.dot(p2[idx], v2[g], preferred_element_type=F32) * inv2[idx]
            o_ref[r0:r0 + blk, c0:c0 + LANES] = o2.astype(BF16)


def _attention(zb, scal, batch, seq, side=()):
    n = zb.shape[0]
    tq = ATT_TQ
    nt = seq // tq
    per = tq // ATT_BLOCK
    kcol = (ATT_Q + M_V) // ATT_KV
    vcol = kcol + 1

    def prev_map(col):
        return lambda b, t: (b * (seq // ATT_BLOCK) + jnp.maximum(t * per - 1, 0), col)

    def cur_map(col):
        return lambda b, t: (b * nt + t, col)

    side_specs = []
    for w in side:
        rows = w.shape[0] // (batch * nt)
        assert rows % CAST_ROWS == 0 and rows * batch * nt == w.shape[0]
        side_specs.append(pl.BlockSpec((rows, w.shape[1]), cur_map(0)))
    outs = pl.pallas_call(
        functools.partial(_attn_kernel, len(side)),
        name="attn",
        grid=(batch, nt),
        in_specs=[
            pl.BlockSpec(memory_space=pltpu.SMEM),
            pl.BlockSpec((tq, ATT_Q), cur_map(0)),
            pl.BlockSpec((ATT_BLOCK, ATT_KV), prev_map(kcol)),
            pl.BlockSpec((tq, ATT_KV), cur_map(kcol)),
            pl.BlockSpec((ATT_BLOCK, ATT_KV), prev_map(vcol)),
            pl.BlockSpec((tq, ATT_KV), cur_map(vcol)),
            *side_specs,
        ],
        out_specs=[pl.BlockSpec((tq, ATT_Q), cur_map(0)), *side_specs],
        out_shape=[jax.ShapeDtypeStruct((n, ATT_Q), BF16),
                   *[jax.ShapeDtypeStruct(w.shape, BF16) for w in side]],
        scratch_shapes=[pltpu.VMEM((ATT_HEADS, ATT_BLOCK, 2 * ATT_BLOCK), F32)],
        compiler_params=pltpu.CompilerParams(
            dimension_semantics=("parallel", "arbitrary"), vmem_limit_bytes=VMEM_LIMIT),
    )(scal, zb, zb, zb, zb, zb, *side)
    return outs


def _mlstm_kernel(qk_ref, v_ref, og_ref, gt_ref, *rest):
    gtt_refs = rest[:M_SEQS]
    (convw_ref, convb_ref, gbias_ref, gbias_col_ref, headg_ref,
     o_ref, xbuf_ref, c_ref, n_ref, m_ref) = rest[M_SEQS:]
    L = M_CHUNK
    pad = 8

    @pl.when(pl.program_id(1) == 0)
    def _():
        xbuf_ref[:, 0:pad, :] = jnp.zeros((M_SEQS, pad, 2 * M_QK), F32)
        c_ref[...] = jnp.zeros_like(c_ref)
        n_ref[...] = jnp.zeros_like(n_ref)
        m_ref[...] = jnp.zeros_like(m_ref)

    w = convw_ref[...]
    base = pad - (CONV_WIDTH - 1)
    qk, gl, lf, gl_t, lf_t = [], [], [], [], []
    for b in range(M_SEQS):
        xbuf_ref[b, pad:pad + L, :] = qk_ref[b]
        y = xbuf_ref[b, pad:pad + L, :] * w[CONV_WIDTH - 1:CONV_WIDTH, :] + convb_ref[...]
        for j in range(CONV_WIDTH - 1):
            y = y + xbuf_ref[b, pl.ds(base + j, L), :] * w[j:j + 1, :]
        xbuf_ref[b, 0:pad, :] = xbuf_ref[b, L:L + pad, :]
        qk.append(y * jax.nn.sigmoid(y))
        gl.append(gt_ref[b] + gbias_ref[...])
        lf.append(_log_sigmoid(gl[b]))
        gl_t.append(gtt_refs[b][...] + gbias_col_ref[...])
        lf_t.append(_log_sigmoid(gl_t[b]))

    ti = lax.broadcasted_iota(jnp.int32, (L, L), 0)
    si = lax.broadcasted_iota(jnp.int32, (L, L), 1)
    tril = si <= ti
    triu = ti <= si
    neg_inf = jnp.float32(-jnp.inf)
    kscale = 1.0 / math.sqrt(M_QK_DIM)

    units = [(b, h) for b in range(M_SEQS) for h in range(M_HEADS)]
    heads = range(len(units))
    q_f = [qk[b][:, h * M_QK_DIM:(h + 1) * M_QK_DIM] for b, h in units]
    k_f = [qk[b][:, M_QK + h * M_QK_DIM:M_QK + (h + 1) * M_QK_DIM] * kscale for b, h in units]
    v_b = [v_ref[b, :, h * M_V_DIM:(h + 1) * M_V_DIM] for b, h in units]
    q_b = [q.astype(BF16) for q in q_f]
    k_b = [k.astype(BF16) for k in k_f]
    m_in = [m_ref[u, 0:1, 0:1] for u in heads]
    c_in = [c_ref[u] for u in heads]
    n_in = [n_ref[u] for u in heads]

    b_col, dmat, m_t, inter_w, pexp, kwt, decay, m_new = [], [], [], [], [], [], [], []
    for b, h in units:
        li_col = gl[b][:, h:h + 1]
        lf_col = lf[b][:, M_HEADS + h:M_HEADS + h + 1]
        li_row = gl_t[b][h:h + 1, :]
        lf_row = lf_t[b][M_HEADS + h:M_HEADS + h + 1, :]
        h = b * M_HEADS + h
        bc = jnp.sum(jnp.where(tril, lf_row, 0.0), axis=1, keepdims=True)
        br = jnp.sum(jnp.where(triu, lf_col, 0.0), axis=0, keepdims=True)
        b_last = bc[L - 1:L, :]
        dm = jnp.where(tril, bc - br + li_row, neg_inf)
        a_col = bc + m_in[h]
        mt = jnp.maximum(a_col, jnp.max(dm, axis=1, keepdims=True))
        inter_w.append(jnp.exp(a_col - mt))
        pexp.append(jnp.exp(dm - mt))
        m_t.append(mt)
        g_col = b_last - bc + li_col
        mn = jnp.maximum(b_last + m_in[h], jnp.max(g_col, axis=0, keepdims=True))
        decay.append(jnp.exp(b_last + m_in[h] - mn))
        kwt.append(jnp.exp(g_col - mn))
        m_new.append(mn)

    p = [lax.dot_general(q_b[h], k_b[h], (((1,), (1,)), ((), ())), preferred_element_type=F32)
         * pexp[h] for h in heads]
    hh = []
    for h in heads:
        num = (jnp.dot(p[h].astype(BF16), v_b[h], preferred_element_type=F32)
               + inter_w[h] * jnp.dot(q_b[h], c_in[h].astype(BF16), preferred_element_type=F32))
        den = (jnp.sum(p[h], axis=1, keepdims=True)
               + inter_w[h] * jnp.sum(q_f[h] * n_in[h], axis=1, keepdims=True))
        hh.append(num / jnp.maximum(jnp.abs(den), jnp.exp(-m_t[h])))
    for h in heads:
        kw = k_f[h] * kwt[h]
        c_ref[h] = decay[h] * c_in[h] + lax.dot_general(
            kw.astype(BF16), v_b[h], (((0,), (0,)), ((), ())), preferred_element_type=F32)
        n_ref[h] = decay[h] * n_in[h] + jnp.sum(kw, axis=0, keepdims=True)
        m_ref[h] = jnp.broadcast_to(m_new[h], m_ref.shape[1:])
    for u, (b, h) in enumerate(units):
        cs = slice(h * M_V_DIM, (h + 1) * M_V_DIM)
        hn = _rms(hh[u], headg_ref[:, cs])
        o_ref[b, :, cs] = (jax.nn.sigmoid(og_ref[b, :, cs]) * hn).astype(BF16)


def _mlstm(zb, zf, gt, conv_w, conv_b, gbias, head_g, batch, seq):
    n = zb.shape[0]
    L = M_CHUNK
    nc = seq // L
    nb = M_SEQS
    assert batch % nb == 0
    zb3 = zb.reshape(batch, seq, ZB_COLS)
    zf3 = zf.reshape(batch, seq, ZF_COLS)
    const = lambda g, c: (0, 0)

    def gtt_spec(b):
        return pl.BlockSpec((GATE_ROWS, L), lambda g, c: (0, (g * nb + b) * nc + c))

    out = pl.pallas_call(
        _mlstm_kernel,
        name="mlstm",
        grid=(batch // nb, nc),
        in_specs=[
            pl.BlockSpec((nb, L, 2 * M_QK), lambda g, c: (g, c, 0)),
            pl.BlockSpec((nb, L, M_V), lambda g, c: (g, c, ATT_Q // M_V)),
            pl.BlockSpec((nb, L, M_V), lambda g, c: (g, c, 2 * M_QK // M_V)),
            pl.BlockSpec((nb, L, GATE_PAD), lambda g, c: (g, c, (2 * M_QK + M_V) // GATE_PAD)),
            *[gtt_spec(b) for b in range(nb)],
            pl.BlockSpec((CONV_WIDTH, 2 * M_QK), const),
            pl.BlockSpec((1, 2 * M_QK), const),
            pl.BlockSpec((1, GATE_PAD), const),
            pl.BlockSpec((GATE_ROWS, 1), const),
            pl.BlockSpec((1, M_V), const),
        ],
        out_specs=pl.BlockSpec((nb, L, M_V), lambda g, c: (g, c, 0)),
        out_shape=jax.ShapeDtypeStruct((batch, seq, M_V), BF16),
        scratch_shapes=[
            pltpu.VMEM((nb, L + 8, 2 * M_QK), F32),
            pltpu.VMEM((nb * M_HEADS, M_QK_DIM, M_V_DIM), F32),
            pltpu.VMEM((nb * M_HEADS, 1, M_QK_DIM), F32),
            pltpu.VMEM((nb * M_HEADS, 8, LANES), F32),
        ],
        compiler_params=pltpu.CompilerParams(
            dimension_semantics=("parallel", "arbitrary"), vmem_limit_bytes=VMEM_LIMIT),
    )(zf3, zb3, zf3, zf3, *([gt] * nb), conv_w, conv_b, gbias, gbias[0, :GATE_ROWS, None], head_g)
    return out.reshape(n, M_V)


def _merge_kernel(ha_ref, hb_ref, att_ref, hm_ref, pre_g_ref, wga_ref, wgm_ref, wa_ref, wm_ref, wo_ref,
                  post_g_ref, o_ref, u0, u1, acc0, acc1):
    j = pl.program_id(1)

    n_slices = MERGE_TM // SLICE_ROWS

    def body(u_next, acc_prev, u_cur, acc_cur, do_main):
        _norm_slices(n_slices, 1.0, ha_ref, hb_ref, pre_g_ref, post_g_ref, o_ref, u_next, acc_prev)
        if do_main:
            u = u_cur[...]
            ga = jnp.dot(u, wga_ref[...], preferred_element_type=F32)
            gm = jnp.dot(u, wgm_ref[...], preferred_element_type=F32)
            a = jnp.dot(att_ref[...], wa_ref[...], preferred_element_type=F32)
            m = jnp.dot(hm_ref[...], wm_ref[...], preferred_element_type=F32)
            y = (jax.nn.sigmoid(ga) * a + jax.nn.sigmoid(gm) * m).astype(BF16)
            acc_cur[...] = (jnp.where(j == 0, 0.0, acc_cur[...])
                            + jnp.dot(y, wo_ref[...], preferred_element_type=F32))

    _for_row_roles((u0, u1), (acc0, acc1), body)


def _merge(h, att, hm, pre_g, wga, wgm, wa, wm, wo, post_g):
    n = h.shape[0]
    tm, tn = MERGE_TM, MERGE_TN
    n_tiles, n_steps, n_slices = n // tm, D_MODEL // tn, tm // SLICE_ROWS
    assert n_steps >= n_slices
    lead, lag, step, tile = _pipeline_maps(n_tiles, n_slices, n_steps)
    return pl.pallas_call(
        _merge_kernel,
        name="merge",
        grid=(n_tiles + 2, n_steps),
        in_specs=[
            pl.BlockSpec((SLICE_ROWS, D_MODEL), lead),
            pl.BlockSpec((SLICE_ROWS, D_MODEL), lag),
            pl.BlockSpec((tm, ATT_Q), tile),
            pl.BlockSpec((tm, M_V), tile),
            pl.BlockSpec((1, D_MODEL), lambda r, j: (0, 0)),
            pl.BlockSpec((D_MODEL, tn), lambda r, j: (0, step(r, j))),
            pl.BlockSpec((D_MODEL, tn), lambda r, j: (0, step(r, j))),
            pl.BlockSpec((ATT_Q, tn), lambda r, j: (0, step(r, j))),
            pl.BlockSpec((M_V, tn), lambda r, j: (0, step(r, j))),
            pl.BlockSpec((tn, D_MODEL), lambda r, j: (step(r, j), 0)),
            pl.BlockSpec((1, D_MODEL), lambda r, j: (0, 0)),
        ],
        out_specs=pl.BlockSpec((SLICE_ROWS, D_MODEL), lag),
        out_shape=jax.ShapeDtypeStruct((n, D_MODEL), F32),
        scratch_shapes=[pltpu.VMEM((tm, D_MODEL), BF16), pltpu.VMEM((tm, D_MODEL), BF16),
                        pltpu.VMEM((tm, D_MODEL), F32), pltpu.VMEM((tm, D_MODEL), F32)],
        compiler_params=pltpu.CompilerParams(
            dimension_semantics=("arbitrary", "arbitrary"), vmem_limit_bytes=VMEM_LIMIT),
    )(h, h, att, hm, pre_g, wga, wgm, wa, wm, wo, post_g)


def kernel(x, ffn1_pre_g, ffn1_wg, ffn1_wu, ffn1_wd, ffn1_post_g, mix_pre_g, w_in, conv_w, conv_b,
           attn_sinks, m_igate_b, m_fgate_b, m_head_g, w_attn_up, w_mlstm_up, w_out, mix_post_g,
           ffn2_pre_g, ffn2_wg, ffn2_wu, ffn2_wd, ffn2_post_g):
    batch, seq, _ = x.shape
    h = x.reshape(batch * seq, D_MODEL)
    for l in range(ffn1_wg.shape[0]):
        bf = lambda w: w.astype(BF16)
        gate0 = W_IN_MAIN + GATE_ROWS
        gbias = jnp.concatenate(
            [m_igate_b[l], m_fgate_b[l], jnp.zeros((GATE_PAD - 2 * M_HEADS,), F32)])[None, :]
        slopes = jnp.exp2(-8.0 * jnp.arange(1, ATT_HEADS + 1, dtype=F32) / ATT_HEADS)
        scal = jnp.stack([attn_sinks[l].astype(F32), slopes])

        h, w_proj = _ffn(h, ffn1_pre_g[l][None, :], ffn1_wg[l], ffn1_wu[l], ffn1_wd[l],
                         ffn1_post_g[l][None, :], side=(w_in[l], W_IN_MAIN + GATE_PAD))
        zb, zf, gt, w_all = _inproj(h, mix_pre_g[l][None, :], w_proj, w_in[l])
        w_ga = w_all[:, gate0:gate0 + D_MODEL]
        w_gm = w_all[:, gate0 + D_MODEL:gate0 + 2 * D_MODEL]
        att, w_au, w_mu, w_o = _attention(zb, scal, batch, seq,
                                          side=(w_attn_up[l], w_mlstm_up[l], w_out[l]))
        hm = _mlstm(zb, zf, gt, conv_w[l], conv_b[l][None, :], gbias, m_head_g[l][None, :],
                    batch, seq)
        h = _merge(h, att, hm, mix_pre_g[l][None, :], w_ga, w_gm, w_au, w_mu, w_o,
                   mix_post_g[l][None, :])
        h = _ffn(h, ffn2_pre_g[l][None, :], ffn2_wg[l], ffn2_wu[l], ffn2_wd[l],
                 ffn2_post_g[l][None, :])
    return h.reshape(batch, seq, D_MODEL)
```

```python
import functools
import math

import jax
import jax.numpy as jnp
from jax import lax
from jax.experimental import pallas as pl
from jax.experimental.pallas import tpu as pltpu

F32 = jnp.float32
BF16 = jnp.bfloat16

D_MODEL = 2048
ATT_HEADS = 16
ATT_KV_HEADS = 4
ATT_HEAD_DIM = 64
ATT_GROUP = ATT_HEADS // ATT_KV_HEADS
WINDOW = 128
ATT_BLOCK = 128
M_HEADS = 4
M_QK_DIM = 128
M_V_DIM = 256
CONV_WIDTH = 4
D_FF = 5632
EPS = 1e-6

ATT_Q = ATT_HEADS * ATT_HEAD_DIM
ATT_KV = ATT_KV_HEADS * ATT_HEAD_DIM
M_QK = M_HEADS * M_QK_DIM
M_V = M_HEADS * M_V_DIM

LANES = 128
GATE_PAD = LANES
GATE_ROWS = 2 * M_HEADS
ZB_COLS = ATT_Q + M_V + 2 * ATT_KV
ZF_COLS = 2 * M_QK + M_V + GATE_PAD

VMEM_LIMIT = 62 * 1024 * 1024

FFN_TM = 1024
FFN_TF = 512
PROJ_TM = 512
PROJ_TN = 512
ATT_TQ = 512
M_CHUNK = 256
M_SEQS = 4
MERGE_TM = 512
MERGE_TN = 512


def _rms(x, g):
    return x * lax.rsqrt(jnp.mean(x * x, axis=-1, keepdims=True) + EPS) * g


def _log_sigmoid(x):
    return jnp.minimum(x, 0.0) - jnp.log1p(jnp.exp(-jnp.abs(x)))


SLICE_ROWS = 128


def _slice_index(f, n_slices):
    return jnp.minimum(f, n_slices - 1)


def _norm_slices(n_slices, res_scale, xa_ref, xb_ref, pre_g_ref, post_g_ref, o_ref, xn_next, acc_prev):
    s = _slice_index(pl.program_id(1), n_slices)
    rows = pl.ds(pl.multiple_of(s * SLICE_ROWS, SLICE_ROWS), SLICE_ROWS)
    xn_next[rows, :] = _rms(xa_ref[...], pre_g_ref[...]).astype(BF16)
    o_ref[...] = xb_ref[...] + res_scale * _rms(acc_prev[rows, :], post_g_ref[...])


def _for_row_roles(xn_bufs, acc_bufs, body):
    r = pl.program_id(0)
    last = pl.num_programs(0) - 1

    @pl.when((r == 0) & (pl.program_id(1) == 0))
    def _():
        for acc in acc_bufs:
            acc[...] = jnp.zeros_like(acc)

    main = (r > 0) & (r < last)
    for p in range(2):
        roles = (xn_bufs[p], acc_bufs[p], xn_bufs[1 - p], acc_bufs[1 - p])
        pl.when(main & (r % 2 == p))(functools.partial(body, *roles, True))
        pl.when(jnp.logical_not(main) & (r % 2 == p))(functools.partial(body, *roles, False))


def _pipeline_maps(n_tiles, n_slices, n_steps):
    def lead(r, f):
        return (jnp.minimum(r, n_tiles - 1) * n_slices + _slice_index(f, n_slices), 0)

    def lag(r, f):
        return (jnp.where(r < 2, 0, (r - 2) * n_slices + _slice_index(f, n_slices)), 0)

    def step(r, f):
        return jnp.where(r == 0, 0, jnp.where(r == n_tiles + 1, n_steps - 1, f))

    def tile(r, f):
        return (jnp.clip(r - 1, 0, n_tiles - 1), 0)

    return lead, lag, step, tile


CAST_ROWS = 16


def _ffn_kernel(xa_ref, xb_ref, pre_g_ref, wg_ref, wu_ref, wd_ref, post_g_ref, o_ref,
                xn0, xn1, acc0, acc1):
    f = pl.program_id(1)

    n_slices = FFN_TM // SLICE_ROWS

    def body(xn_next, acc_prev, xn_cur, acc_cur, do_main):
        _norm_slices(n_slices, 0.5, xa_ref, xb_ref, pre_g_ref, post_g_ref, o_ref, xn_next, acc_prev)
        if do_main:
            xn = xn_cur[...]
            g = jnp.dot(xn, wg_ref[...].astype(BF16), preferred_element_type=F32)
            u = jnp.dot(xn, wu_ref[...].astype(BF16), preferred_element_type=F32)
            hid = (g * jax.nn.sigmoid(g) * u).astype(BF16)
            acc_cur[...] = (jnp.where(f == 0, 0.0, acc_cur[...])
                            + jnp.dot(hid, wd_ref[...].astype(BF16), preferred_element_type=F32))

    _for_row_roles((xn0, xn1), (acc0, acc1), body)


def _ffn(h, pre_g, wg, wu, wd, post_g):
    n = h.shape[0]
    tm, tf = FFN_TM, FFN_TF
    n_tiles, n_steps, n_slices = n // tm, D_FF // tf, tm // SLICE_ROWS
    assert n_steps >= n_slices
    lead, lag, step, _ = _pipeline_maps(n_tiles, n_slices, n_steps)
    return pl.pallas_call(
        _ffn_kernel,
        name="ffn",
        grid=(n_tiles + 2, n_steps),
        in_specs=[
            pl.BlockSpec((SLICE_ROWS, D_MODEL), lead),
            pl.BlockSpec((SLICE_ROWS, D_MODEL), lag),
            pl.BlockSpec((1, D_MODEL), lambda r, f: (0, 0)),
            pl.BlockSpec((D_MODEL, tf), lambda r, f: (0, step(r, f))),
            pl.BlockSpec((D_MODEL, tf), lambda r, f: (0, step(r, f))),
            pl.BlockSpec((tf, D_MODEL), lambda r, f: (step(r, f), 0)),
            pl.BlockSpec((1, D_MODEL), lambda r, f: (0, 0)),
        ],
        out_specs=pl.BlockSpec((SLICE_ROWS, D_MODEL), lag),
        out_shape=jax.ShapeDtypeStruct((n, D_MODEL), F32),
        scratch_shapes=[pltpu.VMEM((tm, D_MODEL), BF16), pltpu.VMEM((tm, D_MODEL), BF16),
                        pltpu.VMEM((tm, D_MODEL), F32), pltpu.VMEM((tm, D_MODEL), F32)],
        compiler_params=pltpu.CompilerParams(
            dimension_semantics=("arbitrary", "arbitrary"), vmem_limit_bytes=VMEM_LIMIT),
    )(h, h, pre_g, wg, wu, wd, post_g)


_OFF_AK = ATT_Q
_OFF_MQ = _OFF_AK + 2 * ATT_KV
_OFF_MV = _OFF_MQ + 2 * M_QK
_OFF_MO = _OFF_MV + M_V
W_IN_MAIN = _OFF_MO + M_V
_PROJ_SEGMENTS = (
    (0, 0, 0, ATT_Q),
    (0, ATT_Q, _OFF_MV, M_V),
    (0, ATT_Q + M_V, _OFF_AK, 2 * ATT_KV),
    (1, 0, _OFF_MQ, 2 * M_QK),
    (1, 2 * M_QK, _OFF_MO, M_V),
)


def _dot_t(x, wt):
    return lax.dot_general(x, wt, (((1,), (1,)), ((), ())), preferred_element_type=F32)


def _inproj_kernel(h_ref, g_ref, wt_ref, wtgate_ref, zb_ref, zf_ref, gt_ref, xn_ref):
    xn_ref[...] = _rms(h_ref[...], g_ref[...]).astype(BF16)
    xn = xn_ref[...]
    outs = (zb_ref, zf_ref)
    for which, out0, w0, width in _PROJ_SEGMENTS:
        for c in range(0, width, PROJ_TN):
            z = _dot_t(xn, wt_ref[w0 + c:w0 + c + PROJ_TN, :])
            outs[which][:, out0 + c:out0 + c + PROJ_TN] = z.astype(outs[which].dtype)
    zg = _dot_t(xn, wtgate_ref[...])
    zf_ref[:, 2 * M_QK + M_V:] = zg
    gt_ref[...] = zg.T[0:GATE_ROWS, :]


def _inproj(h, g, wt):
    n = h.shape[0]
    tm = PROJ_TM
    return pl.pallas_call(
        _inproj_kernel,
        name="inproj",
        grid=(n // tm,),
        in_specs=[
            pl.BlockSpec((tm, D_MODEL), lambda i: (i, 0)),
            pl.BlockSpec((1, D_MODEL), lambda i: (0, 0)),
            pl.BlockSpec((W_IN_MAIN, D_MODEL), lambda i: (0, 0), pipeline_mode=pl.Buffered(1)),
            pl.BlockSpec((GATE_PAD, D_MODEL), lambda i: (W_IN_MAIN // GATE_PAD, 0),
                         pipeline_mode=pl.Buffered(1)),
        ],
        out_specs=[
            pl.BlockSpec((tm, ZB_COLS), lambda i: (i, 0)),
            pl.BlockSpec((tm, ZF_COLS), lambda i: (i, 0)),
            pl.BlockSpec((GATE_ROWS, tm), lambda i: (0, i)),
        ],
        out_shape=[jax.ShapeDtypeStruct((n, ZB_COLS), BF16),
                   jax.ShapeDtypeStruct((n, ZF_COLS), F32),
                   jax.ShapeDtypeStruct((GATE_ROWS, n), F32)],
        scratch_shapes=[pltpu.VMEM((tm, D_MODEL), BF16)],
        compiler_params=pltpu.CompilerParams(
            dimension_semantics=("parallel",), vmem_limit_bytes=VMEM_LIMIT),
    )(h, g, wt, wt)


def _attn_kernel(n_side, scal_ref, q_ref, kp_ref, kc_ref, vp_ref, vc_ref, *rest):
    side_in, o_ref, side_out, bias_ref = (rest[:n_side], rest[n_side], rest[n_side + 1:-1], rest[-1])
    for src, dst in zip(side_in, side_out):
        dst[...] = src[...].astype(BF16)
    t = pl.program_id(1)
    blk = ATT_BLOCK
    half = ATT_HEAD_DIM
    scale = 1.0 / math.sqrt(ATT_HEAD_DIM)
    neg_inf = jnp.float32(-jnp.inf)

    qi = lax.broadcasted_iota(jnp.int32, (blk, 2 * blk), 0)
    kj = lax.broadcasted_iota(jnp.int32, (blk, 2 * blk), 1)

    @pl.when(t == 0)
    def _():
        dist = qi - kj + blk
        band = (dist >= 0) & (dist < WINDOW)
        distf = dist.astype(F32)
        for h in range(ATT_HEADS):
            bias_ref[h] = jnp.where(band, -(scal_ref[1, h] * distf), neg_inf)

    kk = jnp.concatenate([kp_ref[...], kc_ref[...]], axis=0).astype(F32) * scale
    vv = jnp.concatenate([vp_ref[...], vc_ref[...]], axis=0).astype(F32)
    rows = kk.shape[0]
    lane = lax.broadcasted_iota(jnp.int32, (rows, LANES), 1)
    low = lane < half

    def split_pair(x):
        xr = pltpu.roll(x, half, axis=1)
        zero = jnp.zeros_like(x)
        first = (jnp.where(low, x, zero).astype(BF16), jnp.where(low, zero, xr).astype(BF16))
        second = (jnp.where(low, xr, zero).astype(BF16), jnp.where(low, zero, x).astype(BF16))
        return first, second

    k_lo_hi, v_lo_hi = [], []
    for pair in range(ATT_KV // LANES):
        ka, kb = split_pair(kk[:, pair * LANES:(pair + 1) * LANES])
        va, vb = split_pair(vv[:, pair * LANES:(pair + 1) * LANES])
        k_lo_hi += [ka, kb]
        v_lo_hi += [va, vb]

    lane_q = lax.broadcasted_iota(jnp.int32, (blk, LANES), 1)
    has_prev = (kj >= blk) | (t > 0)

    low_q = lane_q < half
    pairs = [(g, pr) for g in range(ATT_KV_HEADS) for pr in range(ATT_GROUP // 2)]
    for j in range(ATT_TQ // blk):
        r0 = j * blk
        k2 = [jnp.concatenate([lo[r0:r0 + 2 * blk], hi[r0:r0 + 2 * blk]], axis=0)
              for lo, hi in k_lo_hi]
        v2 = [jnp.concatenate([lo[r0:r0 + 2 * blk], hi[r0:r0 + 2 * blk]], axis=0)
              for lo, hi in v_lo_hi]
        s2 = []
        for g, pr in pairs:
            c0 = (g * ATT_GROUP + 2 * pr) * half
            s2.append(lax.dot_general(q_ref[r0:r0 + blk, c0:c0 + LANES], k2[g],
                                      (((1,), (1,)), ((), ())), preferred_element_type=F32))
        p2, inv2 = [], []
        for idx, (g, pr) in enumerate(pairs):
            h0 = g * ATT_GROUP + 2 * pr
            ps, ms, sums = [], [], []
            for e in range(2):
                s = s2[idx][:, e * 2 * blk:(e + 1) * 2 * blk] + bias_ref[h0 + e]
                if j == 0:
                    s = jnp.where(has_prev, s, neg_inf)
                m = jnp.maximum(jnp.max(s, axis=-1, keepdims=True), scal_ref[0, h0 + e])
                p = jnp.exp(s - m)
                ps.append(p.astype(BF16))
                ms.append(scal_ref[0, h0 + e] - m)
                sums.append(jnp.sum(p, axis=-1, keepdims=True))
            den = jnp.where(low_q, sums[0], sums[1]) + jnp.exp(jnp.where(low_q, ms[0], ms[1]))
            inv2.append(1.0 / den)
            p2.append(jnp.concatenate(ps, axis=1))
        for idx, (g, pr) in enumerate(pairs):
            c0 = (g * ATT_GROUP + 2 * pr) * half
            o2 = jnp.dot(p2[idx], v2[g], preferred_element_type=F32) * inv2[idx]
            o_ref[r0:r0 + blk, c0:c0 + LANES] = o2.astype(BF16)


def _attention(zb, scal, batch, seq, side=()):
    n = zb.shape[0]
    tq = ATT_TQ
    nt = seq // tq
    per = tq // ATT_BLOCK
    kcol = (ATT_Q + M_V) // ATT_KV
    vcol = kcol + 1

    def prev_map(col):
        return lambda b, t: (b * (seq // ATT_BLOCK) + jnp.maximum(t * per - 1, 0), col)

    def cur_map(col):
        return lambda b, t: (b * nt + t, col)

    side_specs = []
    for w in side:
        rows = w.shape[0] // (batch * nt)
        assert rows % CAST_ROWS == 0 and rows * batch * nt == w.shape[0]
        side_specs.append(pl.BlockSpec((rows, w.shape[1]), cur_map(0)))
    outs = pl.pallas_call(
        functools.partial(_attn_kernel, len(side)),
        name="attn",
        grid=(batch, nt),
        in_specs=[
            pl.BlockSpec(memory_space=pltpu.SMEM),
            pl.BlockSpec((tq, ATT_Q), cur_map(0)),
            pl.BlockSpec((ATT_BLOCK, ATT_KV), prev_map(kcol)),
            pl.BlockSpec((tq, ATT_KV), cur_map(kcol)),
            pl.BlockSpec((ATT_BLOCK, ATT_KV), prev_map(vcol)),
            pl.BlockSpec((tq, ATT_KV), cur_map(vcol)),
            *side_specs,
        ],
        out_specs=[pl.BlockSpec((tq, ATT_Q), cur_map(0)), *side_specs],
        out_shape=[jax.ShapeDtypeStruct((n, ATT_Q), BF16),
                   *[jax.ShapeDtypeStruct(w.shape, BF16) for w in side]],
        scratch_shapes=[pltpu.VMEM((ATT_HEADS, ATT_BLOCK, 2 * ATT_BLOCK), F32)],
        compiler_params=pltpu.CompilerParams(
            dimension_semantics=("parallel", "arbitrary"), vmem_limit_bytes=VMEM_LIMIT),
    )(scal, zb, zb, zb, zb, zb, *side)
    return outs


def _mlstm_kernel(qk_ref, v_ref, og_ref, gt_ref, *rest):
    gtt_refs = rest[:M_SEQS]
    (convw_ref, convb_ref, gbias_ref, gbias_col_ref, headg_ref,
     o_ref, xbuf_ref, c_ref, n_ref, m_ref) = rest[M_SEQS:]
    L = M_CHUNK
    pad = 8

    @pl.when(pl.program_id(1) == 0)
    def _():
        xbuf_ref[:, 0:pad, :] = jnp.zeros((M_SEQS, pad, 2 * M_QK), F32)
        c_ref[...] = jnp.zeros_like(c_ref)
        n_ref[...] = jnp.zeros_like(n_ref)
        m_ref[...] = jnp.zeros_like(m_ref)

    w = convw_ref[...]
    base = pad - (CONV_WIDTH - 1)
    qk, gl, lf, gl_t, lf_t = [], [], [], [], []
    for b in range(M_SEQS):
        xbuf_ref[b, pad:pad + L, :] = qk_ref[b]
        y = xbuf_ref[b, pad:pad + L, :] * w[CONV_WIDTH - 1:CONV_WIDTH, :] + convb_ref[...]
        for j in range(CONV_WIDTH - 1):
            y = y + xbuf_ref[b, pl.ds(base + j, L), :] * w[j:j + 1, :]
        xbuf_ref[b, 0:pad, :] = xbuf_ref[b, L:L + pad, :]
        qk.append(y * jax.nn.sigmoid(y))
        gl.append(gt_ref[b] + gbias_ref[...])
        lf.append(_log_sigmoid(gl[b]))
        gl_t.append(gtt_refs[b][...] + gbias_col_ref[...])
        lf_t.append(_log_sigmoid(gl_t[b]))

    ti = lax.broadcasted_iota(jnp.int32, (L, L), 0)
    si = lax.broadcasted_iota(jnp.int32, (L, L), 1)
    tril = si <= ti
    triu = ti <= si
    neg_inf = jnp.float32(-jnp.inf)
    kscale = 1.0 / math.sqrt(M_QK_DIM)

    units = [(b, h) for b in range(M_SEQS) for h in range(M_HEADS)]
    heads = range(len(units))
    q_f = [qk[b][:, h * M_QK_DIM:(h + 1) * M_QK_DIM] for b, h in units]
    k_f = [qk[b][:, M_QK + h * M_QK_DIM:M_QK + (h + 1) * M_QK_DIM] * kscale for b, h in units]
    v_b = [v_ref[b, :, h * M_V_DIM:(h + 1) * M_V_DIM] for b, h in units]
    q_b = [q.astype(BF16) for q in q_f]
    k_b = [k.astype(BF16) for k in k_f]
    m_in = [m_ref[u, 0:1, 0:1] for u in heads]
    c_in = [c_ref[u] for u in heads]
    n_in = [n_ref[u] for u in heads]

    b_col, dmat, m_t, inter_w, pexp, kwt, decay, m_new = [], [], [], [], [], [], [], []
    for b, h in units:
        li_col = gl[b][:, h:h + 1]
        lf_col = lf[b][:, M_HEADS + h:M_HEADS + h + 1]
        li_row = gl_t[b][h:h + 1, :]
        lf_row = lf_t[b][M_HEADS + h:M_HEADS + h + 1, :]
        h = b * M_HEADS + h
        bc = jnp.sum(jnp.where(tril, lf_row, 0.0), axis=1, keepdims=True)
        br = jnp.sum(jnp.where(triu, lf_col, 0.0), axis=0, keepdims=True)
        b_last = bc[L - 1:L, :]
        dm = jnp.where(tril, bc - br + li_row, neg_inf)
        a_col = bc + m_in[h]
        mt = jnp.maximum(a_col, jnp.max(dm, axis=1, keepdims=True))
        inter_w.append(jnp.exp(a_col - mt))
        pexp.append(jnp.exp(dm - mt))
        m_t.append(mt)
        g_col = b_last - bc + li_col
        mn = jnp.maximum(b_last + m_in[h], jnp.max(g_col, axis=0, keepdims=True))
        decay.append(jnp.exp(b_last + m_in[h] - mn))
        kwt.append(jnp.exp(g_col - mn))
        m_new.append(mn)

    p = [lax.dot_general(q_b[h], k_b[h], (((1,), (1,)), ((), ())), preferred_element_type=F32)
         * pexp[h] for h in heads]
    hh = []
    for h in heads:
        num = (jnp.dot(p[h].astype(BF16), v_b[h], preferred_element_type=F32)
               + inter_w[h] * jnp.dot(q_b[h], c_in[h].astype(BF16), preferred_element_type=F32))
        den = (jnp.sum(p[h], axis=1, keepdims=True)
               + inter_w[h] * jnp.sum(q_f[h] * n_in[h], axis=1, keepdims=True))
        hh.append(num / jnp.maximum(jnp.abs(den), jnp.exp(-m_t[h])))
    for h in heads:
        kw = k_f[h] * kwt[h]
        c_ref[h] = decay[h] * c_in[h] + lax.dot_general(
            kw.astype(BF16), v_b[h], (((0,), (0,)), ((), ())), preferred_element_type=F32)
        n_ref[h] = decay[h] * n_in[h] + jnp.sum(kw, axis=0, keepdims=True)
        m_ref[h] = jnp.broadcast_to(m_new[h], m_ref.shape[1:])
    for u, (b, h) in enumerate(units):
        cs = slice(h * M_V_DIM, (h + 1) * M_V_DIM)
        hn = _rms(hh[u], headg_ref[:, cs])
        o_ref[b, :, cs] = (jax.nn.sigmoid(og_ref[b, :, cs]) * hn).astype(BF16)


def _mlstm(zb, zf, gt, conv_w, conv_b, gbias, head_g, batch, seq):
    n = zb.shape[0]
    L = M_CHUNK
    nc = seq // L
    nb = M_SEQS
    assert batch % nb == 0
    zb3 = zb.reshape(batch, seq, ZB_COLS)
    zf3 = zf.reshape(batch, seq, ZF_COLS)
    const = lambda g, c: (0, 0)

    def gtt_spec(b):
        return pl.BlockSpec((GATE_ROWS, L), lambda g, c: (0, (g * nb + b) * nc + c))

    out = pl.pallas_call(
        _mlstm_kernel,
        name="mlstm",
        grid=(batch // nb, nc),
        in_specs=[
            pl.BlockSpec((nb, L, 2 * M_QK), lambda g, c: (g, c, 0)),
            pl.BlockSpec((nb, L, M_V), lambda g, c: (g, c, ATT_Q // M_V)),
            pl.BlockSpec((nb, L, M_V), lambda g, c: (g, c, 2 * M_QK // M_V)),
            pl.BlockSpec((nb, L, GATE_PAD), lambda g, c: (g, c, (2 * M_QK + M_V) // GATE_PAD)),
            *[gtt_spec(b) for b in range(nb)],
            pl.BlockSpec((CONV_WIDTH, 2 * M_QK), const),
            pl.BlockSpec((1, 2 * M_QK), const),
            pl.BlockSpec((1, GATE_PAD), const),
            pl.BlockSpec((GATE_ROWS, 1), const),
            pl.BlockSpec((1, M_V), const),
        ],
        out_specs=pl.BlockSpec((nb, L, M_V), lambda g, c: (g, c, 0)),
        out_shape=jax.ShapeDtypeStruct((batch, seq, M_V), BF16),
        scratch_shapes=[
            pltpu.VMEM((nb, L + 8, 2 * M_QK), F32),
            pltpu.VMEM((nb * M_HEADS, M_QK_DIM, M_V_DIM), F32),
            pltpu.VMEM((nb * M_HEADS, 1, M_QK_DIM), F32),
            pltpu.VMEM((nb * M_HEADS, 8, LANES), F32),
        ],
        compiler_params=pltpu.CompilerParams(
            dimension_semantics=("parallel", "arbitrary"), vmem_limit_bytes=VMEM_LIMIT),
    )(zf3, zb3, zf3, zf3, *([gt] * nb), conv_w, conv_b, gbias, gbias[0, :GATE_ROWS, None], head_g)
    return out.reshape(n, M_V)


def _merge_kernel(ha_ref, hb_ref, att_ref, hm_ref, pre_g_ref, wga_ref, wgm_ref, wa_ref, wm_ref, wo_ref,
                  post_g_ref, o_ref, u0, u1, acc0, acc1):
    j = pl.program_id(1)

    n_slices = MERGE_TM // SLICE_ROWS

    def body(u_next, acc_prev, u_cur, acc_cur, do_main):
        _norm_slices(n_slices, 1.0, ha_ref, hb_ref, pre_g_ref, post_g_ref, o_ref, u_next, acc_prev)
        if do_main:
            u = u_cur[...]
            ga = _dot_t(u, wga_ref[...])
            gm = _dot_t(u, wgm_ref[...])
            a = jnp.dot(att_ref[...], wa_ref[...], preferred_element_type=F32)
            m = jnp.dot(hm_ref[...], wm_ref[...], preferred_element_type=F32)
            y = (jax.nn.sigmoid(ga) * a + jax.nn.sigmoid(gm) * m).astype(BF16)
            acc_cur[...] = (jnp.where(j == 0, 0.0, acc_cur[...])
                            + jnp.dot(y, wo_ref[...], preferred_element_type=F32))

    _for_row_roles((u0, u1), (acc0, acc1), body)


def _merge(h, att, hm, pre_g, wga, wgm, wa, wm, wo, post_g):
    n = h.shape[0]
    tm, tn = MERGE_TM, MERGE_TN
    n_tiles, n_steps, n_slices = n // tm, D_MODEL // tn, tm // SLICE_ROWS
    assert n_steps >= n_slices
    lead, lag, step, tile = _pipeline_maps(n_tiles, n_slices, n_steps)
    return pl.pallas_call(
        _merge_kernel,
        name="merge",
        grid=(n_tiles + 2, n_steps),
        in_specs=[
            pl.BlockSpec((SLICE_ROWS, D_MODEL), lead),
            pl.BlockSpec((SLICE_ROWS, D_MODEL), lag),
            pl.BlockSpec((tm, ATT_Q), tile),
            pl.BlockSpec((tm, M_V), tile),
            pl.BlockSpec((1, D_MODEL), lambda r, j: (0, 0)),
            pl.BlockSpec((tn, D_MODEL), lambda r, j: (step(r, j), 0)),
            pl.BlockSpec((tn, D_MODEL), lambda r, j: (step(r, j), 0)),
            pl.BlockSpec((ATT_Q, tn), lambda r, j: (0, step(r, j))),
            pl.BlockSpec((M_V, tn), lambda r, j: (0, step(r, j))),
            pl.BlockSpec((tn, D_MODEL), lambda r, j: (step(r, j), 0)),
            pl.BlockSpec((1, D_MODEL), lambda r, j: (0, 0)),
        ],
        out_specs=pl.BlockSpec((SLICE_ROWS, D_MODEL), lag),
        out_shape=jax.ShapeDtypeStruct((n, D_MODEL), F32),
        scratch_shapes=[pltpu.VMEM((tm, D_MODEL), BF16), pltpu.VMEM((tm, D_MODEL), BF16),
                        pltpu.VMEM((tm, D_MODEL), F32), pltpu.VMEM((tm, D_MODEL), F32)],
        compiler_params=pltpu.CompilerParams(
            dimension_semantics=("arbitrary", "arbitrary"), vmem_limit_bytes=VMEM_LIMIT),
    )(h, h, att, hm, pre_g, wga, wgm, wa, wm, wo, post_g)


def kernel(x, ffn1_pre_g, ffn1_wg, ffn1_wu, ffn1_wd, ffn1_post_g, mix_pre_g, w_in, conv_w, conv_b,
           attn_sinks, m_igate_b, m_fgate_b, m_head_g, w_attn_up, w_mlstm_up, w_out, mix_post_g,
           ffn2_pre_g, ffn2_wg, ffn2_wu, ffn2_wd, ffn2_post_g):
    batch, seq, _ = x.shape
    h = x.reshape(batch * seq, D_MODEL)
    for l in range(ffn1_wg.shape[0]):
        bf = lambda w: w.astype(BF16)
        gate0 = W_IN_MAIN + GATE_ROWS
        w_in_t = bf(jnp.swapaxes(w_in[l], 0, 1))
        w_ga_t = w_in_t[gate0:gate0 + D_MODEL]
        w_gm_t = w_in_t[gate0 + D_MODEL:gate0 + 2 * D_MODEL]
        gbias = jnp.concatenate(
            [m_igate_b[l], m_fgate_b[l], jnp.zeros((GATE_PAD - 2 * M_HEADS,), F32)])[None, :]
        slopes = jnp.exp2(-8.0 * jnp.arange(1, ATT_HEADS + 1, dtype=F32) / ATT_HEADS)
        scal = jnp.stack([attn_sinks[l].astype(F32), slopes])

        h = _ffn(h, ffn1_pre_g[l][None, :], ffn1_wg[l], ffn1_wu[l], ffn1_wd[l],
                 ffn1_post_g[l][None, :])
        zb, zf, gt = _inproj(h, mix_pre_g[l][None, :], w_in_t)
        att, w_au, w_mu, w_o = _attention(zb, scal, batch, seq,
                                          side=(w_attn_up[l], w_mlstm_up[l], w_out[l]))
        hm = _mlstm(zb, zf, gt, conv_w[l], conv_b[l][None, :], gbias, m_head_g[l][None, :],
                    batch, seq)
        h = _merge(h, att, hm, mix_pre_g[l][None, :], w_ga_t, w_gm_t, w_au, w_mu, w_o,
                   mix_post_g[l][None, :])
        h = _ffn(h, ffn2_pre_g[l][None, :], ffn2_wg[l], ffn2_wu[l], ffn2_wd[l],
                 ffn2_post_g[l][None, :])
    return h.reshape(batch, seq, D_MODEL)
```

```python
import functools
import math

import jax
import jax.numpy as jnp
from jax import lax
from jax.experimental import pallas as pl
from jax.experimental.pallas import tpu as pltpu

F32 = jnp.float32
BF16 = jnp.bfloat16

D_MODEL = 2048
ATT_HEADS = 16
ATT_KV_HEADS = 4
ATT_HEAD_DIM = 64
ATT_GROUP = ATT_HEADS // ATT_KV_HEADS
WINDOW = 128
ATT_BLOCK = 128
M_HEADS = 4
M_QK_DIM = 128
M_V_DIM = 256
CONV_WIDTH = 4
D_FF = 5632
EPS = 1e-6

ATT_Q = ATT_HEADS * ATT_HEAD_DIM
ATT_KV = ATT_KV_HEADS * ATT_HEAD_DIM
M_QK = M_HEADS * M_QK_DIM
M_V = M_HEADS * M_V_DIM

LANES = 128
GATE_PAD = LANES
GATE_ROWS = 2 * M_HEADS
ZB_COLS = ATT_Q + M_V + 2 * ATT_KV
ZF_COLS = 2 * M_QK + M_V + GATE_PAD

VMEM_LIMIT = 62 * 1024 * 1024

FFN_TM = 1024
FFN_TF = 512
PROJ_TM = 512
PROJ_TN = 512
ATT_TQ = 512
M_CHUNK = 256
M_SEQS = 4
MERGE_TM = 1024
MERGE_TN = 256


def _rms(x, g):
    return x * lax.rsqrt(jnp.mean(x * x, axis=-1, keepdims=True) + EPS) * g


def _log_sigmoid(x):
    return jnp.minimum(x, 0.0) - jnp.log1p(jnp.exp(-jnp.abs(x)))


SLICE_ROWS = 128


def _slice_index(f, n_slices):
    return jnp.minimum(f, n_slices - 1)


def _norm_slices(n_slices, res_scale, xa_ref, xb_ref, pre_g_ref, post_g_ref, o_ref, xn_next, acc_prev):
    s = _slice_index(pl.program_id(1), n_slices)
    rows = pl.ds(pl.multiple_of(s * SLICE_ROWS, SLICE_ROWS), SLICE_ROWS)
    xn_next[rows, :] = _rms(xa_ref[...], pre_g_ref[...]).astype(BF16)
    o_ref[...] = xb_ref[...] + res_scale * _rms(acc_prev[rows, :], post_g_ref[...])


def _for_row_roles(xn_bufs, acc_bufs, body):
    r = pl.program_id(0)
    last = pl.num_programs(0) - 1

    @pl.when((r == 0) & (pl.program_id(1) == 0))
    def _():
        for acc in acc_bufs:
            acc[...] = jnp.zeros_like(acc)

    main = (r > 0) & (r < last)
    for p in range(2):
        roles = (xn_bufs[p], acc_bufs[p], xn_bufs[1 - p], acc_bufs[1 - p])
        pl.when(main & (r % 2 == p))(functools.partial(body, *roles, True))
        pl.when(jnp.logical_not(main) & (r % 2 == p))(functools.partial(body, *roles, False))


def _pipeline_maps(n_tiles, n_slices, n_steps):
    def lead(r, f):
        return (jnp.minimum(r, n_tiles - 1) * n_slices + _slice_index(f, n_slices), 0)

    def lag(r, f):
        return (jnp.where(r < 2, 0, (r - 2) * n_slices + _slice_index(f, n_slices)), 0)

    def step(r, f):
        return jnp.where(r == 0, 0, jnp.where(r == n_tiles + 1, n_steps - 1, f))

    def tile(r, f):
        return (jnp.clip(r - 1, 0, n_tiles - 1), 0)

    return lead, lag, step, tile


CAST_ROWS = 16


def _ffn_kernel(xa_ref, xb_ref, pre_g_ref, wg_ref, wu_ref, wd_ref, post_g_ref, o_ref,
                xn0, xn1, acc0, acc1):
    f = pl.program_id(1)

    n_slices = FFN_TM // SLICE_ROWS

    def body(xn_next, acc_prev, xn_cur, acc_cur, do_main):
        _norm_slices(n_slices, 0.5, xa_ref, xb_ref, pre_g_ref, post_g_ref, o_ref, xn_next, acc_prev)
        if do_main:
            xn = xn_cur[...]
            g = jnp.dot(xn, wg_ref[...].astype(BF16), preferred_element_type=F32)
            u = jnp.dot(xn, wu_ref[...].astype(BF16), preferred_element_type=F32)
            hid = (g * jax.nn.sigmoid(g) * u).astype(BF16)
            acc_cur[...] = (jnp.where(f == 0, 0.0, acc_cur[...])
                            + jnp.dot(hid, wd_ref[...].astype(BF16), preferred_element_type=F32))

    _for_row_roles((xn0, xn1), (acc0, acc1), body)


def _ffn(h, pre_g, wg, wu, wd, post_g):
    n = h.shape[0]
    tm, tf = FFN_TM, FFN_TF
    n_tiles, n_steps, n_slices = n // tm, D_FF // tf, tm // SLICE_ROWS
    assert n_steps >= n_slices
    lead, lag, step, _ = _pipeline_maps(n_tiles, n_slices, n_steps)
    return pl.pallas_call(
        _ffn_kernel,
        name="ffn",
        grid=(n_tiles + 2, n_steps),
        in_specs=[
            pl.BlockSpec((SLICE_ROWS, D_MODEL), lead),
            pl.BlockSpec((SLICE_ROWS, D_MODEL), lag),
            pl.BlockSpec((1, D_MODEL), lambda r, f: (0, 0)),
            pl.BlockSpec((D_MODEL, tf), lambda r, f: (0, step(r, f))),
            pl.BlockSpec((D_MODEL, tf), lambda r, f: (0, step(r, f))),
            pl.BlockSpec((tf, D_MODEL), lambda r, f: (step(r, f), 0)),
            pl.BlockSpec((1, D_MODEL), lambda r, f: (0, 0)),
        ],
        out_specs=pl.BlockSpec((SLICE_ROWS, D_MODEL), lag),
        out_shape=jax.ShapeDtypeStruct((n, D_MODEL), F32),
        scratch_shapes=[pltpu.VMEM((tm, D_MODEL), BF16), pltpu.VMEM((tm, D_MODEL), BF16),
                        pltpu.VMEM((tm, D_MODEL), F32), pltpu.VMEM((tm, D_MODEL), F32)],
        compiler_params=pltpu.CompilerParams(
            dimension_semantics=("arbitrary", "arbitrary"), vmem_limit_bytes=VMEM_LIMIT),
    )(h, h, pre_g, wg, wu, wd, post_g)


_OFF_AK = ATT_Q
_OFF_MQ = _OFF_AK + 2 * ATT_KV
_OFF_MV = _OFF_MQ + 2 * M_QK
_OFF_MO = _OFF_MV + M_V
W_IN_MAIN = _OFF_MO + M_V
_PROJ_SEGMENTS = (
    (0, 0, 0, ATT_Q),
    (0, ATT_Q, _OFF_MV, M_V),
    (0, ATT_Q + M_V, _OFF_AK, 2 * ATT_KV),
    (1, 0, _OFF_MQ, 2 * M_QK),
    (1, 2 * M_QK, _OFF_MO, M_V),
)


def _dot_t(x, wt):
    return lax.dot_general(x, wt, (((1,), (1,)), ((), ())), preferred_element_type=F32)


def _inproj_kernel(h_ref, g_ref, wt_ref, wtgate_ref, zb_ref, zf_ref, gt_ref, xn_ref):
    xn_ref[...] = _rms(h_ref[...], g_ref[...]).astype(BF16)
    xn = xn_ref[...]
    outs = (zb_ref, zf_ref)
    for which, out0, w0, width in _PROJ_SEGMENTS:
        for c in range(0, width, PROJ_TN):
            z = _dot_t(xn, wt_ref[w0 + c:w0 + c + PROJ_TN, :])
            outs[which][:, out0 + c:out0 + c + PROJ_TN] = z.astype(outs[which].dtype)
    zg = _dot_t(xn, wtgate_ref[...])
    zf_ref[:, 2 * M_QK + M_V:] = zg
    gt_ref[...] = zg.T[0:GATE_ROWS, :]


def _inproj(h, g, wt):
    n = h.shape[0]
    tm = PROJ_TM
    return pl.pallas_call(
        _inproj_kernel,
        name="inproj",
        grid=(n // tm,),
        in_specs=[
            pl.BlockSpec((tm, D_MODEL), lambda i: (i, 0)),
            pl.BlockSpec((1, D_MODEL), lambda i: (0, 0)),
            pl.BlockSpec((W_IN_MAIN, D_MODEL), lambda i: (0, 0), pipeline_mode=pl.Buffered(1)),
            pl.BlockSpec((GATE_PAD, D_MODEL), lambda i: (W_IN_MAIN // GATE_PAD, 0),
                         pipeline_mode=pl.Buffered(1)),
        ],
        out_specs=[
            pl.BlockSpec((tm, ZB_COLS), lambda i: (i, 0)),
            pl.BlockSpec((tm, ZF_COLS), lambda i: (i, 0)),
            pl.BlockSpec((GATE_ROWS, tm), lambda i: (0, i)),
        ],
        out_shape=[jax.ShapeDtypeStruct((n, ZB_COLS), BF16),
                   jax.ShapeDtypeStruct((n, ZF_COLS), F32),
                   jax.ShapeDtypeStruct((GATE_ROWS, n), F32)],
        scratch_shapes=[pltpu.VMEM((tm, D_MODEL), BF16)],
        compiler_params=pltpu.CompilerParams(
            dimension_semantics=("parallel",), vmem_limit_bytes=VMEM_LIMIT),
    )(h, g, wt, wt)


def _attn_kernel(n_side, scal_ref, q_ref, kp_ref, kc_ref, vp_ref, vc_ref, *rest):
    side_in, o_ref, side_out, bias_ref = (rest[:n_side], rest[n_side], rest[n_side + 1:-1], rest[-1])
    for src, dst in zip(side_in, side_out):
        dst[...] = src[...].astype(BF16)
    t = pl.program_id(1)
    blk = ATT_BLOCK
    half = ATT_HEAD_DIM
    scale = 1.0 / math.sqrt(ATT_HEAD_DIM)
    neg_inf = jnp.float32(-jnp.inf)

    qi = lax.broadcasted_iota(jnp.int32, (blk, 2 * blk), 0)
    kj = lax.broadcasted_iota(jnp.int32, (blk, 2 * blk), 1)

    @pl.when(t == 0)
    def _():
        dist = qi - kj + blk
        band = (dist >= 0) & (dist < WINDOW)
        distf = dist.astype(F32)
        for h in range(ATT_HEADS):
            bias_ref[h] = jnp.where(band, -(scal_ref[1, h] * distf), neg_inf)

    kk = jnp.concatenate([kp_ref[...], kc_ref[...]], axis=0).astype(F32) * scale
    vv = jnp.concatenate([vp_ref[...], vc_ref[...]], axis=0).astype(F32)
    rows = kk.shape[0]
    lane = lax.broadcasted_iota(jnp.int32, (rows, LANES), 1)
    low = lane < half

    def split_pair(x):
        xr = pltpu.roll(x, half, axis=1)
        zero = jnp.zeros_like(x)
        first = (jnp.where(low, x, zero).astype(BF16), jnp.where(low, zero, xr).astype(BF16))
        second = (jnp.where(low, xr, zero).astype(BF16), jnp.where(low, zero, x).astype(BF16))
        return first, second

    k_lo_hi, v_lo_hi = [], []
    for pair in range(ATT_KV // LANES):
        ka, kb = split_pair(kk[:, pair * LANES:(pair + 1) * LANES])
        va, vb = split_pair(vv[:, pair * LANES:(pair + 1) * LANES])
        k_lo_hi += [ka, kb]
        v_lo_hi += [va, vb]

    lane_q = lax.broadcasted_iota(jnp.int32, (blk, LANES), 1)
    has_prev = (kj >= blk) | (t > 0)

    low_q = lane_q < half
    pairs = [(g, pr) for g in range(ATT_KV_HEADS) for pr in range(ATT_GROUP // 2)]
    for j in range(ATT_TQ // blk):
        r0 = j * blk
        k2 = [jnp.concatenate([lo[r0:r0 + 2 * blk], hi[r0:r0 + 2 * blk]], axis=0)
              for lo, hi in k_lo_hi]
        v2 = [jnp.concatenate([lo[r0:r0 + 2 * blk], hi[r0:r0 + 2 * blk]], axis=0)
              for lo, hi in v_lo_hi]
        s2 = []
        for g, pr in pairs:
            c0 = (g * ATT_GROUP + 2 * pr) * half
            s2.append(lax.dot_general(q_ref[r0:r0 + blk, c0:c0 + LANES], k2[g],
                                      (((1,), (1,)), ((), ())), preferred_element_type=F32))
        p2, inv2 = [], []
        for idx, (g, pr) in enumerate(pairs):
            h0 = g * ATT_GROUP + 2 * pr
            ps, ms, sums = [], [], []
            for e in range(2):
                s = s2[idx][:, e * 2 * blk:(e + 1) * 2 * blk] + bias_ref[h0 + e]
                if j == 0:
                    s = jnp.where(has_prev, s, neg_inf)
                m = jnp.maximum(jnp.max(s, axis=-1, keepdims=True), scal_ref[0, h0 + e])
                p = jnp.exp(s - m)
                ps.append(p.astype(BF16))
                ms.append(scal_ref[0, h0 + e] - m)
                sums.append(jnp.sum(p, axis=-1, keepdims=True))
            den = jnp.where(low_q, sums[0], sums[1]) + jnp.exp(jnp.where(low_q, ms[0], ms[1]))
            inv2.append(1.0 / den)
            p2.append(jnp.concatenate(ps, axis=1))
        for idx, (g, pr) in enumerate(pairs):
            c0 = (g * ATT_GROUP + 2 * pr) * half
            o2 = jnp.dot(p2[idx], v2[g], preferred_element_type=F32) * inv2[idx]
            o_ref[r0:r0 + blk, c0:c0 + LANES] = o2.astype(BF16)


def _attention(zb, scal, batch, seq, side=()):
    n = zb.shape[0]
    tq = ATT_TQ
    nt = seq // tq
    per = tq // ATT_BLOCK
    kcol = (ATT_Q + M_V) // ATT_KV
    vcol = kcol + 1

    def prev_map(col):
        return lambda b, t: (b * (seq // ATT_BLOCK) + jnp.maximum(t * per - 1, 0), col)

    def cur_map(col):
        return lambda b, t: (b * nt + t, col)

    side_specs = []
    for w in side:
        rows = w.shape[0] // (batch * nt)
        assert rows % CAST_ROWS == 0 and rows * batch * nt == w.shape[0]
        side_specs.append(pl.BlockSpec((rows, w.shape[1]), cur_map(0)))
    outs = pl.pallas_call(
        functools.partial(_attn_kernel, len(side)),
        name="attn",
        grid=(batch, nt),
        in_specs=[
            pl.BlockSpec(memory_space=pltpu.SMEM),
            pl.BlockSpec((tq, ATT_Q), cur_map(0)),
            pl.BlockSpec((ATT_BLOCK, ATT_KV), prev_map(kcol)),
            pl.BlockSpec((tq, ATT_KV), cur_map(kcol)),
            pl.BlockSpec((ATT_BLOCK, ATT_KV), prev_map(vcol)),
            pl.BlockSpec((tq, ATT_KV), cur_map(vcol)),
            *side_specs,
        ],
        out_specs=[pl.BlockSpec((tq, ATT_Q), cur_map(0)), *side_specs],
        out_shape=[jax.ShapeDtypeStruct((n, ATT_Q), BF16),
                   *[jax.ShapeDtypeStruct(w.shape, BF16) for w in side]],
        scratch_shapes=[pltpu.VMEM((ATT_HEADS, ATT_BLOCK, 2 * ATT_BLOCK), F32)],
        compiler_params=pltpu.CompilerParams(
            dimension_semantics=("parallel", "arbitrary"), vmem_limit_bytes=VMEM_LIMIT),
    )(scal, zb, zb, zb, zb, zb, *side)
    return outs


def _mlstm_kernel(qk_ref, v_ref, og_ref, gt_ref, *rest):
    gtt_refs = rest[:M_SEQS]
    (convw_ref, convb_ref, gbias_ref, gbias_col_ref, headg_ref,
     o_ref, xbuf_ref, c_ref, n_ref, m_ref) = rest[M_SEQS:]
    L = M_CHUNK
    pad = 8

    @pl.when(pl.program_id(1) == 0)
    def _():
        xbuf_ref[:, 0:pad, :] = jnp.zeros((M_SEQS, pad, 2 * M_QK), F32)
        c_ref[...] = jnp.zeros_like(c_ref)
        n_ref[...] = jnp.zeros_like(n_ref)
        m_ref[...] = jnp.zeros_like(m_ref)

    w = convw_ref[...]
    base = pad - (CONV_WIDTH - 1)
    qk, gl, lf, gl_t, lf_t = [], [], [], [], []
    for b in range(M_SEQS):
        xbuf_ref[b, pad:pad + L, :] = qk_ref[b]
        y = xbuf_ref[b, pad:pad + L, :] * w[CONV_WIDTH - 1:CONV_WIDTH, :] + convb_ref[...]
        for j in range(CONV_WIDTH - 1):
            y = y + xbuf_ref[b, pl.ds(base + j, L), :] * w[j:j + 1, :]
        xbuf_ref[b, 0:pad, :] = xbuf_ref[b, L:L + pad, :]
        qk.append(y * jax.nn.sigmoid(y))
        gl.append(gt_ref[b] + gbias_ref[...])
        lf.append(_log_sigmoid(gl[b]))
        gl_t.append(gtt_refs[b][...] + gbias_col_ref[...])
        lf_t.append(_log_sigmoid(gl_t[b]))

    ti = lax.broadcasted_iota(jnp.int32, (L, L), 0)
    si = lax.broadcasted_iota(jnp.int32, (L, L), 1)
    tril = si <= ti
    triu = ti <= si
    neg_inf = jnp.float32(-jnp.inf)
    kscale = 1.0 / math.sqrt(M_QK_DIM)

    units = [(b, h) for b in range(M_SEQS) for h in range(M_HEADS)]
    heads = range(len(units))
    q_f = [qk[b][:, h * M_QK_DIM:(h + 1) * M_QK_DIM] for b, h in units]
    k_f = [qk[b][:, M_QK + h * M_QK_DIM:M_QK + (h + 1) * M_QK_DIM] * kscale for b, h in units]
    v_b = [v_ref[b, :, h * M_V_DIM:(h + 1) * M_V_DIM] for b, h in units]
    q_b = [q.astype(BF16) for q in q_f]
    k_b = [k.astype(BF16) for k in k_f]
    m_in = [m_ref[u, 0:1, 0:1] for u in heads]
    c_in = [c_ref[u] for u in heads]
    n_in = [n_ref[u] for u in heads]

    b_col, dmat, m_t, inter_w, pexp, kwt, decay, m_new = [], [], [], [], [], [], [], []
    for b, h in units:
        li_col = gl[b][:, h:h + 1]
        lf_col = lf[b][:, M_HEADS + h:M_HEADS + h + 1]
        li_row = gl_t[b][h:h + 1, :]
        lf_row = lf_t[b][M_HEADS + h:M_HEADS + h + 1, :]
        h = b * M_HEADS + h
        bc = jnp.sum(jnp.where(tril, lf_row, 0.0), axis=1, keepdims=True)
        br = jnp.sum(jnp.where(triu, lf_col, 0.0), axis=0, keepdims=True)
        b_last = bc[L - 1:L, :]
        dm = jnp.where(tril, bc - br + li_row, neg_inf)
        a_col = bc + m_in[h]
        mt = jnp.maximum(a_col, jnp.max(dm, axis=1, keepdims=True))
        inter_w.append(jnp.exp(a_col - mt))
        pexp.append(jnp.exp(dm - mt))
        m_t.append(mt)
        g_col = b_last - bc + li_col
        mn = jnp.maximum(b_last + m_in[h], jnp.max(g_col, axis=0, keepdims=True))
        decay.append(jnp.exp(b_last + m_in[h] - mn))
        kwt.append(jnp.exp(g_col - mn))
        m_new.append(mn)

    p = [lax.dot_general(q_b[h], k_b[h], (((1,), (1,)), ((), ())), preferred_element_type=F32)
         * pexp[h] for h in heads]
    hh = []
    for h in heads:
        num = (jnp.dot(p[h].astype(BF16), v_b[h], preferred_element_type=F32)
               + inter_w[h] * jnp.dot(q_b[h], c_in[h].astype(BF16), preferred_element_type=F32))
        den = (jnp.sum(p[h], axis=1, keepdims=True)
               + inter_w[h] * jnp.sum(q_f[h] * n_in[h], axis=1, keepdims=True))
        hh.append(num / jnp.maximum(jnp.abs(den), jnp.exp(-m_t[h])))
    for h in heads:
        kw = k_f[h] * kwt[h]
        c_ref[h] = decay[h] * c_in[h] + lax.dot_general(
            kw.astype(BF16), v_b[h], (((0,), (0,)), ((), ())), preferred_element_type=F32)
        n_ref[h] = decay[h] * n_in[h] + jnp.sum(kw, axis=0, keepdims=True)
        m_ref[h] = jnp.broadcast_to(m_new[h], m_ref.shape[1:])
    for u, (b, h) in enumerate(units):
        cs = slice(h * M_V_DIM, (h + 1) * M_V_DIM)
        hn = _rms(hh[u], headg_ref[:, cs])
        o_ref[b, :, cs] = (jax.nn.sigmoid(og_ref[b, :, cs]) * hn).astype(BF16)


def _mlstm(zb, zf, gt, conv_w, conv_b, gbias, head_g, batch, seq):
    n = zb.shape[0]
    L = M_CHUNK
    nc = seq // L
    nb = M_SEQS
    assert batch % nb == 0
    zb3 = zb.reshape(batch, seq, ZB_COLS)
    zf3 = zf.reshape(batch, seq, ZF_COLS)
    const = lambda g, c: (0, 0)

    def gtt_spec(b):
        return pl.BlockSpec((GATE_ROWS, L), lambda g, c: (0, (g * nb + b) * nc + c))

    out = pl.pallas_call(
        _mlstm_kernel,
        name="mlstm",
        grid=(batch // nb, nc),
        in_specs=[
            pl.BlockSpec((nb, L, 2 * M_QK), lambda g, c: (g, c, 0)),
            pl.BlockSpec((nb, L, M_V), lambda g, c: (g, c, ATT_Q // M_V)),
            pl.BlockSpec((nb, L, M_V), lambda g, c: (g, c, 2 * M_QK // M_V)),
            pl.BlockSpec((nb, L, GATE_PAD), lambda g, c: (g, c, (2 * M_QK + M_V) // GATE_PAD)),
            *[gtt_spec(b) for b in range(nb)],
            pl.BlockSpec((CONV_WIDTH, 2 * M_QK), const),
            pl.BlockSpec((1, 2 * M_QK), const),
            pl.BlockSpec((1, GATE_PAD), const),
            pl.BlockSpec((GATE_ROWS, 1), const),
            pl.BlockSpec((1, M_V), const),
        ],
        out_specs=pl.BlockSpec((nb, L, M_V), lambda g, c: (g, c, 0)),
        out_shape=jax.ShapeDtypeStruct((batch, seq, M_V), BF16),
        scratch_shapes=[
            pltpu.VMEM((nb, L + 8, 2 * M_QK), F32),
            pltpu.VMEM((nb * M_HEADS, M_QK_DIM, M_V_DIM), F32),
            pltpu.VMEM((nb * M_HEADS, 1, M_QK_DIM), F32),
            pltpu.VMEM((nb * M_HEADS, 8, LANES), F32),
        ],
        compiler_params=pltpu.CompilerParams(
            dimension_semantics=("parallel", "arbitrary"), vmem_limit_bytes=VMEM_LIMIT),
    )(zf3, zb3, zf3, zf3, *([gt] * nb), conv_w, conv_b, gbias, gbias[0, :GATE_ROWS, None], head_g)
    return out.reshape(n, M_V)


def _merge_kernel(ha_ref, hb_ref, att_ref, hm_ref, pre_g_ref, wga_ref, wgm_ref, wa_ref, wm_ref, wo_ref,
                  post_g_ref, o_ref, u0, u1, acc0, acc1):
    j = pl.program_id(1)

    n_slices = MERGE_TM // SLICE_ROWS

    def body(u_next, acc_prev, u_cur, acc_cur, do_main):
        _norm_slices(n_slices, 1.0, ha_ref, hb_ref, pre_g_ref, post_g_ref, o_ref, u_next, acc_prev)
        if do_main:
            u = u_cur[...]
            ga = _dot_t(u, wga_ref[...])
            gm = _dot_t(u, wgm_ref[...])
            a = jnp.dot(att_ref[...], wa_ref[...], preferred_element_type=F32)
            m = jnp.dot(hm_ref[...], wm_ref[...], preferred_element_type=F32)
            y = (jax.nn.sigmoid(ga) * a + jax.nn.sigmoid(gm) * m).astype(BF16)
            acc_cur[...] = (jnp.where(j == 0, 0.0, acc_cur[...])
                            + jnp.dot(y, wo_ref[...], preferred_element_type=F32))

    _for_row_roles((u0, u1), (acc0, acc1), body)


def _merge(h, att, hm, pre_g, wga, wgm, wa, wm, wo, post_g):
    n = h.shape[0]
    tm, tn = MERGE_TM, MERGE_TN
    n_tiles, n_steps, n_slices = n // tm, D_MODEL // tn, tm // SLICE_ROWS
    assert n_steps >= n_slices
    lead, lag, step, tile = _pipeline_maps(n_tiles, n_slices, n_steps)
    return pl.pallas_call(
        _merge_kernel,
        name="merge",
        grid=(n_tiles + 2, n_steps),
        in_specs=[
            pl.BlockSpec((SLICE_ROWS, D_MODEL), lead),
            pl.BlockSpec((SLICE_ROWS, D_MODEL), lag),
            pl.BlockSpec((tm, ATT_Q), tile),
            pl.BlockSpec((tm, M_V), tile),
            pl.BlockSpec((1, D_MODEL), lambda r, j: (0, 0)),
            pl.BlockSpec((tn, D_MODEL), lambda r, j: (step(r, j), 0)),
            pl.BlockSpec((tn, D_MODEL), lambda r, j: (step(r, j), 0)),
            pl.BlockSpec((ATT_Q, tn), lambda r, j: (0, step(r, j))),
            pl.BlockSpec((M_V, tn), lambda r, j: (0, step(r, j))),
            pl.BlockSpec((tn, D_MODEL), lambda r, j: (step(r, j), 0)),
            pl.BlockSpec((1, D_MODEL), lambda r, j: (0, 0)),
        ],
        out_specs=pl.BlockSpec((SLICE_ROWS, D_MODEL), lag),
        out_shape=jax.ShapeDtypeStruct((n, D_MODEL), F32),
        scratch_shapes=[pltpu.VMEM((tm, D_MODEL), BF16), pltpu.VMEM((tm, D_MODEL), BF16),
                        pltpu.VMEM((tm, D_MODEL), F32), pltpu.VMEM((tm, D_MODEL), F32)],
        compiler_params=pltpu.CompilerParams(
            dimension_semantics=("arbitrary", "arbitrary"), vmem_limit_bytes=VMEM_LIMIT),
    )(h, h, att, hm, pre_g, wga, wgm, wa, wm, wo, post_g)


def kernel(x, ffn1_pre_g, ffn1_wg, ffn1_wu, ffn1_wd, ffn1_post_g, mix_pre_g, w_in, conv_w, conv_b,
           attn_sinks, m_igate_b, m_fgate_b, m_head_g, w_attn_up, w_mlstm_up, w_out, mix_post_g,
           ffn2_pre_g, ffn2_wg, ffn2_wu, ffn2_wd, ffn2_post_g):
    batch, seq, _ = x.shape
    h = x.reshape(batch * seq, D_MODEL)
    for l in range(ffn1_wg.shape[0]):
        bf = lambda w: w.astype(BF16)
        gate0 = W_IN_MAIN + GATE_ROWS
        w_in_t = bf(jnp.swapaxes(w_in[l], 0, 1))
        w_ga_t = w_in_t[gate0:gate0 + D_MODEL]
        w_gm_t = w_in_t[gate0 + D_MODEL:gate0 + 2 * D_MODEL]
        gbias = jnp.concatenate(
            [m_igate_b[l], m_fgate_b[l], jnp.zeros((GATE_PAD - 2 * M_HEADS,), F32)])[None, :]
        slopes = jnp.exp2(-8.0 * jnp.arange(1, ATT_HEADS + 1, dtype=F32) / ATT_HEADS)
        scal = jnp.stack([attn_sinks[l].astype(F32), slopes])

        h = _ffn(h, ffn1_pre_g[l][None, :], ffn1_wg[l], ffn1_wu[l], ffn1_wd[l],
                 ffn1_post_g[l][None, :])
        zb, zf, gt = _inproj(h, mix_pre_g[l][None, :], w_in_t)
        att, w_au, w_mu, w_o = _attention(zb, scal, batch, seq,
                                          side=(w_attn_up[l], w_mlstm_up[l], w_out[l]))
        hm = _mlstm(zb, zf, gt, conv_w[l], conv_b[l][None, :], gbias, m_head_g[l][None, :],
                    batch, seq)
        h = _merge(h, att, hm, mix_pre_g[l][None, :], w_ga_t, w_gm_t, w_au, w_mu, w_o,
                   mix_post_g[l][None, :])
        h = _ffn(h, ffn2_pre_g[l][None, :], ffn2_wg[l], ffn2_wu[l], ffn2_wd[l],
                 ffn2_post_g[l][None, :])
    return h.reshape(batch, seq, D_MODEL)
```

```python
import functools
import math

import jax
import jax.numpy as jnp
from jax import lax
from jax.experimental import pallas as pl
from jax.experimental.pallas import tpu as pltpu

F32 = jnp.float32
BF16 = jnp.bfloat16

D_MODEL = 2048
ATT_HEADS = 16
ATT_KV_HEADS = 4
ATT_HEAD_DIM = 64
ATT_GROUP = ATT_HEADS // ATT_KV_HEADS
WINDOW = 128
ATT_BLOCK = 128
M_HEADS = 4
M_QK_DIM = 128
M_V_DIM = 256
CONV_WIDTH = 4
D_FF = 5632
EPS = 1e-6

ATT_Q = ATT_HEADS * ATT_HEAD_DIM
ATT_KV = ATT_KV_HEADS * ATT_HEAD_DIM
M_QK = M_HEADS * M_QK_DIM
M_V = M_HEADS * M_V_DIM

LANES = 128
GATE_PAD = LANES
GATE_ROWS = 2 * M_HEADS
ZB_COLS = ATT_Q + M_V + 2 * ATT_KV
ZF_COLS = 2 * M_QK + M_V + GATE_PAD

VMEM_LIMIT = 62 * 1024 * 1024

FFN_TM = 1024
FFN_TF = 512
PROJ_TM = 512
PROJ_TN = 512
ATT_TQ = 512
M_CHUNK = 256
M_SEQS = 4
MERGE_TM = 512
MERGE_TN = 512


def _rms(x, g):
    return x * lax.rsqrt(jnp.mean(x * x, axis=-1, keepdims=True) + EPS) * g


def _log_sigmoid(x):
    return jnp.minimum(x, 0.0) - jnp.log1p(jnp.exp(-jnp.abs(x)))


SLICE_ROWS = 128


def _slice_index(f, n_slices):
    return jnp.minimum(f, n_slices - 1)


def _norm_slices(n_slices, res_scale, xa_ref, xb_ref, pre_g_ref, post_g_ref, o_ref, xn_next, acc_prev):
    s = _slice_index(pl.program_id(1), n_slices)
    rows = pl.ds(pl.multiple_of(s * SLICE_ROWS, SLICE_ROWS), SLICE_ROWS)
    xn_next[rows, :] = _rms(xa_ref[...], pre_g_ref[...]).astype(BF16)
    o_ref[...] = xb_ref[...] + res_scale * _rms(acc_prev[rows, :], post_g_ref[...])


def _for_row_roles(xn_bufs, acc_bufs, body):
    r = pl.program_id(0)
    last = pl.num_programs(0) - 1

    @pl.when((r == 0) & (pl.program_id(1) == 0))
    def _():
        for acc in acc_bufs:
            acc[...] = jnp.zeros_like(acc)

    main = (r > 0) & (r < last)
    for p in range(2):
        roles = (xn_bufs[p], acc_bufs[p], xn_bufs[1 - p], acc_bufs[1 - p])
        pl.when(main & (r % 2 == p))(functools.partial(body, *roles, True))
        pl.when(jnp.logical_not(main) & (r % 2 == p))(functools.partial(body, *roles, False))


def _pipeline_maps(n_tiles, n_slices, n_steps):
    def lead(r, f):
        return (jnp.minimum(r, n_tiles - 1) * n_slices + _slice_index(f, n_slices), 0)

    def lag(r, f):
        return (jnp.where(r < 2, 0, (r - 2) * n_slices + _slice_index(f, n_slices)), 0)

    def step(r, f):
        return jnp.where(r == 0, 0, jnp.where(r == n_tiles + 1, n_steps - 1, f))

    def tile(r, f):
        return (jnp.clip(r - 1, 0, n_tiles - 1), 0)

    return lead, lag, step, tile


CAST_ROWS = 16


def _ffn_kernel(xa_ref, xb_ref, pre_g_ref, wg_ref, wu_ref, wd_ref, post_g_ref, o_ref,
                xn0, xn1, acc0, acc1):
    f = pl.program_id(1)

    n_slices = FFN_TM // SLICE_ROWS

    def body(xn_next, acc_prev, xn_cur, acc_cur, do_main):
        _norm_slices(n_slices, 0.5, xa_ref, xb_ref, pre_g_ref, post_g_ref, o_ref, xn_next, acc_prev)
        if do_main:
            xn = xn_cur[...]
            g = jnp.dot(xn, wg_ref[...].astype(BF16), preferred_element_type=F32)
            u = jnp.dot(xn, wu_ref[...].astype(BF16), preferred_element_type=F32)
            hid = (g * jax.nn.sigmoid(g) * u).astype(BF16)
            acc_cur[...] = (jnp.where(f == 0, 0.0, acc_cur[...])
                            + jnp.dot(hid, wd_ref[...].astype(BF16), preferred_element_type=F32))

    _for_row_roles((xn0, xn1), (acc0, acc1), body)


def _ffn(h, pre_g, wg, wu, wd, post_g):
    n = h.shape[0]
    tm, tf = FFN_TM, FFN_TF
    n_tiles, n_steps, n_slices = n // tm, D_FF // tf, tm // SLICE_ROWS
    assert n_steps >= n_slices
    lead, lag, step, _ = _pipeline_maps(n_tiles, n_slices, n_steps)
    return pl.pallas_call(
        _ffn_kernel,
        name="ffn",
        grid=(n_tiles + 2, n_steps),
        in_specs=[
            pl.BlockSpec((SLICE_ROWS, D_MODEL), lead),
            pl.BlockSpec((SLICE_ROWS, D_MODEL), lag),
            pl.BlockSpec((1, D_MODEL), lambda r, f: (0, 0)),
            pl.BlockSpec((D_MODEL, tf), lambda r, f: (0, step(r, f))),
            pl.BlockSpec((D_MODEL, tf), lambda r, f: (0, step(r, f))),
            pl.BlockSpec((tf, D_MODEL), lambda r, f: (step(r, f), 0)),
            pl.BlockSpec((1, D_MODEL), lambda r, f: (0, 0)),
        ],
        out_specs=pl.BlockSpec((SLICE_ROWS, D_MODEL), lag),
        out_shape=jax.ShapeDtypeStruct((n, D_MODEL), F32),
        scratch_shapes=[pltpu.VMEM((tm, D_MODEL), BF16), pltpu.VMEM((tm, D_MODEL), BF16),
                        pltpu.VMEM((tm, D_MODEL), F32), pltpu.VMEM((tm, D_MODEL), F32)],
        compiler_params=pltpu.CompilerParams(
            dimension_semantics=("arbitrary", "arbitrary"), vmem_limit_bytes=VMEM_LIMIT),
    )(h, h, pre_g, wg, wu, wd, post_g)


_OFF_AK = ATT_Q
_OFF_MQ = _OFF_AK + 2 * ATT_KV
_OFF_MV = _OFF_MQ + 2 * M_QK
_OFF_MO = _OFF_MV + M_V
W_IN_MAIN = _OFF_MO + M_V
_PROJ_SEGMENTS = (
    (0, 0, 0, ATT_Q),
    (0, ATT_Q, _OFF_MV, M_V),
    (0, ATT_Q + M_V, _OFF_AK, 2 * ATT_KV),
    (1, 0, _OFF_MQ, 2 * M_QK),
    (1, 2 * M_QK, _OFF_MO, M_V),
)


def _dot_t(x, wt):
    return lax.dot_general(x, wt, (((1,), (1,)), ((), ())), preferred_element_type=F32)


def _inproj_kernel(h_ref, g_ref, wt_ref, wtgate_ref, zb_ref, zf_ref, gt_ref, xn0, xn1):
    r = pl.program_id(0)
    last = pl.num_programs(0) - 1

    def body(xn_next, xn_cur, do_main, do_norm):
        if do_main:
            xn = xn_cur[...]
            outs = (zb_ref, zf_ref)
            for which, out0, w0, width in _PROJ_SEGMENTS:
                for c in range(0, width, PROJ_TN):
                    z = _dot_t(xn, wt_ref[w0 + c:w0 + c + PROJ_TN, :])
                    outs[which][:, out0 + c:out0 + c + PROJ_TN] = z.astype(outs[which].dtype)
            zg = _dot_t(xn, wtgate_ref[...])
            zf_ref[:, 2 * M_QK + M_V:] = zg
            gt_ref[...] = zg.T[0:GATE_ROWS, :]
        if do_norm:
            xn_next[...] = _rms(h_ref[...], g_ref[...]).astype(BF16)

    xn_bufs = (xn0, xn1)
    for p in range(2):
        roles = (xn_bufs[p], xn_bufs[1 - p])
        pl.when((r == 0) & (r % 2 == p))(functools.partial(body, *roles, False, True))
        pl.when((r > 0) & (r < last) & (r % 2 == p))(functools.partial(body, *roles, True, True))
        pl.when((r == last) & (r % 2 == p))(functools.partial(body, *roles, True, False))


def _inproj(h, g, wt):
    n = h.shape[0]
    tm = PROJ_TM
    n_tiles = n // tm
    lead = lambda r: jnp.minimum(r, n_tiles - 1)
    lag = lambda r: jnp.maximum(r - 1, 0)
    return pl.pallas_call(
        _inproj_kernel,
        name="inproj",
        grid=(n_tiles + 1,),
        in_specs=[
            pl.BlockSpec((tm, D_MODEL), lambda r: (lead(r), 0)),
            pl.BlockSpec((1, D_MODEL), lambda r: (0, 0)),
            pl.BlockSpec((W_IN_MAIN, D_MODEL), lambda r: (0, 0), pipeline_mode=pl.Buffered(1)),
            pl.BlockSpec((GATE_PAD, D_MODEL), lambda r: (W_IN_MAIN // GATE_PAD, 0),
                         pipeline_mode=pl.Buffered(1)),
        ],
        out_specs=[
            pl.BlockSpec((tm, ZB_COLS), lambda r: (lag(r), 0)),
            pl.BlockSpec((tm, ZF_COLS), lambda r: (lag(r), 0)),
            pl.BlockSpec((GATE_ROWS, tm), lambda r: (0, lag(r))),
        ],
        out_shape=[jax.ShapeDtypeStruct((n, ZB_COLS), BF16),
                   jax.ShapeDtypeStruct((n, ZF_COLS), F32),
                   jax.ShapeDtypeStruct((GATE_ROWS, n), F32)],
        scratch_shapes=[pltpu.VMEM((tm, D_MODEL), BF16), pltpu.VMEM((tm, D_MODEL), BF16)],
        compiler_params=pltpu.CompilerParams(
            dimension_semantics=("arbitrary",), vmem_limit_bytes=VMEM_LIMIT),
    )(h, g, wt, wt)


def _attn_kernel(n_side, scal_ref, q_ref, kp_ref, kc_ref, vp_ref, vc_ref, *rest):
    side_in, o_ref, side_out, bias_ref = (rest[:n_side], rest[n_side], rest[n_side + 1:-1], rest[-1])
    for src, dst in zip(side_in, side_out):
        dst[...] = src[...].astype(BF16)
    t = pl.program_id(1)
    blk = ATT_BLOCK
    half = ATT_HEAD_DIM
    scale = 1.0 / math.sqrt(ATT_HEAD_DIM)
    neg_inf = jnp.float32(-jnp.inf)

    qi = lax.broadcasted_iota(jnp.int32, (blk, 2 * blk), 0)
    kj = lax.broadcasted_iota(jnp.int32, (blk, 2 * blk), 1)

    @pl.when(t == 0)
    def _():
        dist = qi - kj + blk
        band = (dist >= 0) & (dist < WINDOW)
        distf = dist.astype(F32)
        for h in range(ATT_HEADS):
            bias_ref[h] = jnp.where(band, -(scal_ref[1, h] * distf), neg_inf)

    kk = jnp.concatenate([kp_ref[...], kc_ref[...]], axis=0).astype(F32) * scale
    vv = jnp.concatenate([vp_ref[...], vc_ref[...]], axis=0).astype(F32)
    rows = kk.shape[0]
    lane = lax.broadcasted_iota(jnp.int32, (rows, LANES), 1)
    low = lane < half

    def split_pair(x):
        xr = pltpu.roll(x, half, axis=1)
        zero = jnp.zeros_like(x)
        first = (jnp.where(low, x, zero).astype(BF16), jnp.where(low, zero, xr).astype(BF16))
        second = (jnp.where(low, xr, zero).astype(BF16), jnp.where(low, zero, x).astype(BF16))
        return first, second

    k_lo_hi, v_lo_hi = [], []
    for pair in range(ATT_KV // LANES):
        ka, kb = split_pair(kk[:, pair * LANES:(pair + 1) * LANES])
        va, vb = split_pair(vv[:, pair * LANES:(pair + 1) * LANES])
        k_lo_hi += [ka, kb]
        v_lo_hi += [va, vb]

    lane_q = lax.broadcasted_iota(jnp.int32, (blk, LANES), 1)
    has_prev = (kj >= blk) | (t > 0)

    low_q = lane_q < half
    pairs = [(g, pr) for g in range(ATT_KV_HEADS) for pr in range(ATT_GROUP // 2)]
    for j in range(ATT_TQ // blk):
        r0 = j * blk
        k2 = [jnp.concatenate([lo[r0:r0 + 2 * blk], hi[r0:r0 + 2 * blk]], axis=0)
              for lo, hi in k_lo_hi]
        v2 = [jnp.concatenate([lo[r0:r0 + 2 * blk], hi[r0:r0 + 2 * blk]], axis=0)
              for lo, hi in v_lo_hi]
        s2 = []
        for g, pr in pairs:
            c0 = (g * ATT_GROUP + 2 * pr) * half
            s2.append(lax.dot_general(q_ref[r0:r0 + blk, c0:c0 + LANES], k2[g],
                                      (((1,), (1,)), ((), ())), preferred_element_type=F32))
        p2, inv2 = [], []
        for idx, (g, pr) in enumerate(pairs):
            h0 = g * ATT_GROUP + 2 * pr
            ps, ms, sums = [], [], []
            for e in range(2):
                s = s2[idx][:, e * 2 * blk:(e + 1) * 2 * blk] + bias_ref[h0 + e]
                if j == 0:
                    s = jnp.where(has_prev, s, neg_inf)
                m = jnp.maximum(jnp.max(s, axis=-1, keepdims=True), scal_ref[0, h0 + e])
                p = jnp.exp(s - m)
                ps.append(p.astype(BF16))
                ms.append(scal_ref[0, h0 + e] - m)
                sums.append(jnp.sum(p, axis=-1, keepdims=True))
            den = jnp.where(low_q, sums[0], sums[1]) + jnp.exp(jnp.where(low_q, ms[0], ms[1]))
            inv2.append(1.0 / den)
            p2.append(jnp.concatenate(ps, axis=1))
        for idx, (g, pr) in enumerate(pairs):
            c0 = (g * ATT_GROUP + 2 * pr) * half
            o2 = jnp.dot(p2[idx], v2[g], preferred_element_type=F32) * inv2[idx]
            o_ref[r0:r0 + blk, c0:c0 + LANES] = o2.astype(BF16)


def _attention(zb, scal, batch, seq, side=()):
    n = zb.shape[0]
    tq = ATT_TQ
    nt = seq // tq
    per = tq // ATT_BLOCK
    kcol = (ATT_Q + M_V) // ATT_KV
    vcol = kcol + 1

    def prev_map(col):
        return lambda b, t: (b * (seq // ATT_BLOCK) + jnp.maximum(t * per - 1, 0), col)

    def cur_map(col):
        return lambda b, t: (b * nt + t, col)

    side_specs = []
    for w in side:
        rows = w.shape[0] // (batch * nt)
        assert rows % CAST_ROWS == 0 and rows * batch * nt == w.shape[0]
        side_specs.append(pl.BlockSpec((rows, w.shape[1]), cur_map(0)))
    outs = pl.pallas_call(
        functools.partial(_attn_kernel, len(side)),
        name="attn",
        grid=(batch, nt),
        in_specs=[
            pl.BlockSpec(memory_space=pltpu.SMEM),
            pl.BlockSpec((tq, ATT_Q), cur_map(0)),
            pl.BlockSpec((ATT_BLOCK, ATT_KV), prev_map(kcol)),
            pl.BlockSpec((tq, ATT_KV), cur_map(kcol)),
            pl.BlockSpec((ATT_BLOCK, ATT_KV), prev_map(vcol)),
            pl.BlockSpec((tq, ATT_KV), cur_map(vcol)),
            *side_specs,
        ],
        out_specs=[pl.BlockSpec((tq, ATT_Q), cur_map(0)), *side_specs],
        out_shape=[jax.ShapeDtypeStruct((n, ATT_Q), BF16),
                   *[jax.ShapeDtypeStruct(w.shape, BF16) for w in side]],
        scratch_shapes=[pltpu.VMEM((ATT_HEADS, ATT_BLOCK, 2 * ATT_BLOCK), F32)],
        compiler_params=pltpu.CompilerParams(
            dimension_semantics=("parallel", "arbitrary"), vmem_limit_bytes=VMEM_LIMIT),
    )(scal, zb, zb, zb, zb, zb, *side)
    return outs


def _mlstm_kernel(qk_ref, v_ref, og_ref, gt_ref, *rest):
    gtt_refs = rest[:M_SEQS]
    (convw_ref, convb_ref, gbias_ref, gbias_col_ref, headg_ref,
     o_ref, xbuf_ref, c_ref, n_ref, m_ref) = rest[M_SEQS:]
    L = M_CHUNK
    pad = 8

    @pl.when(pl.program_id(1) == 0)
    def _():
        xbuf_ref[:, 0:pad, :] = jnp.zeros((M_SEQS, pad, 2 * M_QK), F32)
        c_ref[...] = jnp.zeros_like(c_ref)
        n_ref[...] = jnp.zeros_like(n_ref)
        m_ref[...] = jnp.zeros_like(m_ref)

    w = convw_ref[...]
    base = pad - (CONV_WIDTH - 1)
    qk, gl, lf, gl_t, lf_t = [], [], [], [], []
    for b in range(M_SEQS):
        xbuf_ref[b, pad:pad + L, :] = qk_ref[b]
        y = xbuf_ref[b, pad:pad + L, :] * w[CONV_WIDTH - 1:CONV_WIDTH, :] + convb_ref[...]
        for j in range(CONV_WIDTH - 1):
            y = y + xbuf_ref[b, pl.ds(base + j, L), :] * w[j:j + 1, :]
        xbuf_ref[b, 0:pad, :] = xbuf_ref[b, L:L + pad, :]
        qk.append(y * jax.nn.sigmoid(y))
        gl.append(gt_ref[b] + gbias_ref[...])
        lf.append(_log_sigmoid(gl[b]))
        gl_t.append(gtt_refs[b][...] + gbias_col_ref[...])
        lf_t.append(_log_sigmoid(gl_t[b]))

    ti = lax.broadcasted_iota(jnp.int32, (L, L), 0)
    si = lax.broadcasted_iota(jnp.int32, (L, L), 1)
    tril = si <= ti
    triu = ti <= si
    neg_inf = jnp.float32(-jnp.inf)
    kscale = 1.0 / math.sqrt(M_QK_DIM)

    units = [(b, h) for b in range(M_SEQS) for h in range(M_HEADS)]
    heads = range(len(units))
    q_f = [qk[b][:, h * M_QK_DIM:(h + 1) * M_QK_DIM] for b, h in units]
    k_f = [qk[b][:, M_QK + h * M_QK_DIM:M_QK + (h + 1) * M_QK_DIM] * kscale for b, h in units]
    v_b = [v_ref[b, :, h * M_V_DIM:(h + 1) * M_V_DIM] for b, h in units]
    q_b = [q.astype(BF16) for q in q_f]
    k_b = [k.astype(BF16) for k in k_f]
    m_in = [m_ref[u, 0:1, 0:1] for u in heads]
    c_in = [c_ref[u] for u in heads]
    n_in = [n_ref[u] for u in heads]

    b_col, dmat, m_t, inter_w, pexp, kwt, decay, m_new = [], [], [], [], [], [], [], []
    for b, h in units:
        li_col = gl[b][:, h:h + 1]
        lf_col = lf[b][:, M_HEADS + h:M_HEADS + h + 1]
        li_row = gl_t[b][h:h + 1, :]
        lf_row = lf_t[b][M_HEADS + h:M_HEADS + h + 1, :]
        h = b * M_HEADS + h
        bc = jnp.sum(jnp.where(tril, lf_row, 0.0), axis=1, keepdims=True)
        br = jnp.sum(jnp.where(triu, lf_col, 0.0), axis=0, keepdims=True)
        b_last = bc[L - 1:L, :]
        dm = jnp.where(tril, bc - br + li_row, neg_inf)
        a_col = bc + m_in[h]
        mt = jnp.maximum(a_col, jnp.max(dm, axis=1, keepdims=True))
        inter_w.append(jnp.exp(a_col - mt))
        pexp.append(jnp.exp(dm - mt))
        m_t.append(mt)
        g_col = b_last - bc + li_col
        mn = jnp.maximum(b_last + m_in[h], jnp.max(g_col, axis=0, keepdims=True))
        decay.append(jnp.exp(b_last + m_in[h] - mn))
        kwt.append(jnp.exp(g_col - mn))
        m_new.append(mn)

    p = [lax.dot_general(q_b[h], k_b[h], (((1,), (1,)), ((), ())), preferred_element_type=F32)
         * pexp[h] for h in heads]
    hh = []
    for h in heads:
        num = (jnp.dot(p[h].astype(BF16), v_b[h], preferred_element_type=F32)
               + inter_w[h] * jnp.dot(q_b[h], c_in[h].astype(BF16), preferred_element_type=F32))
        den = (jnp.sum(p[h], axis=1, keepdims=True)
               + inter_w[h] * jnp.sum(q_f[h] * n_in[h], axis=1, keepdims=True))
        hh.append(num / jnp.maximum(jnp.abs(den), jnp.exp(-m_t[h])))
    for h in heads:
        kw = k_f[h] * kwt[h]
        c_ref[h] = decay[h] * c_in[h] + lax.dot_general(
            kw.astype(BF16), v_b[h], (((0,), (0,)), ((), ())), preferred_element_type=F32)
        n_ref[h] = decay[h] * n_in[h] + jnp.sum(kw, axis=0, keepdims=True)
        m_ref[h] = jnp.broadcast_to(m_new[h], m_ref.shape[1:])
    for u, (b, h) in enumerate(units):
        cs = slice(h * M_V_DIM, (h + 1) * M_V_DIM)
        hn = _rms(hh[u], headg_ref[:, cs])
        o_ref[b, :, cs] = (jax.nn.sigmoid(og_ref[b, :, cs]) * hn).astype(BF16)


def _mlstm(zb, zf, gt, conv_w, conv_b, gbias, head_g, batch, seq):
    n = zb.shape[0]
    L = M_CHUNK
    nc = seq // L
    nb = M_SEQS
    assert batch % nb == 0
    zb3 = zb.reshape(batch, seq, ZB_COLS)
    zf3 = zf.reshape(batch, seq, ZF_COLS)
    const = lambda g, c: (0, 0)

    def gtt_spec(b):
        return pl.BlockSpec((GATE_ROWS, L), lambda g, c: (0, (g * nb + b) * nc + c))

    out = pl.pallas_call(
        _mlstm_kernel,
        name="mlstm",
        grid=(batch // nb, nc),
        in_specs=[
            pl.BlockSpec((nb, L, 2 * M_QK), lambda g, c: (g, c, 0)),
            pl.BlockSpec((nb, L, M_V), lambda g, c: (g, c, ATT_Q // M_V)),
            pl.BlockSpec((nb, L, M_V), lambda g, c: (g, c, 2 * M_QK // M_V)),
            pl.BlockSpec((nb, L, GATE_PAD), lambda g, c: (g, c, (2 * M_QK + M_V) // GATE_PAD)),
            *[gtt_spec(b) for b in range(nb)],
            pl.BlockSpec((CONV_WIDTH, 2 * M_QK), const),
            pl.BlockSpec((1, 2 * M_QK), const),
            pl.BlockSpec((1, GATE_PAD), const),
            pl.BlockSpec((GATE_ROWS, 1), const),
            pl.BlockSpec((1, M_V), const),
        ],
        out_specs=pl.BlockSpec((nb, L, M_V), lambda g, c: (g, c, 0)),
        out_shape=jax.ShapeDtypeStruct((batch, seq, M_V), BF16),
        scratch_shapes=[
            pltpu.VMEM((nb, L + 8, 2 * M_QK), F32),
            pltpu.VMEM((nb * M_HEADS, M_QK_DIM, M_V_DIM), F32),
            pltpu.VMEM((nb * M_HEADS, 1, M_QK_DIM), F32),
            pltpu.VMEM((nb * M_HEADS, 8, LANES), F32),
        ],
        compiler_params=pltpu.CompilerParams(
            dimension_semantics=("parallel", "arbitrary"), vmem_limit_bytes=VMEM_LIMIT),
    )(zf3, zb3, zf3, zf3, *([gt] * nb), conv_w, conv_b, gbias, gbias[0, :GATE_ROWS, None], head_g)
    return out.reshape(n, M_V)


def _merge_kernel(ha_ref, hb_ref, att_ref, hm_ref, pre_g_ref, wga_ref, wgm_ref, wa_ref, wm_ref, wo_ref,
                  post_g_ref, o_ref, u0, u1, acc0, acc1):
    j = pl.program_id(1)

    n_slices = MERGE_TM // SLICE_ROWS

    def body(u_next, acc_prev, u_cur, acc_cur, do_main):
        _norm_slices(n_slices, 1.0, ha_ref, hb_ref, pre_g_ref, post_g_ref, o_ref, u_next, acc_prev)
        if do_main:
            u = u_cur[...]
            ga = _dot_t(u, wga_ref[...])
            gm = _dot_t(u, wgm_ref[...])
            a = jnp.dot(att_ref[...], wa_ref[...], preferred_element_type=F32)
            m = jnp.dot(hm_ref[...], wm_ref[...], preferred_element_type=F32)
            y = (jax.nn.sigmoid(ga) * a + jax.nn.sigmoid(gm) * m).astype(BF16)
            acc_cur[...] = (jnp.where(j == 0, 0.0, acc_cur[...])
                            + jnp.dot(y, wo_ref[...], preferred_element_type=F32))

    _for_row_roles((u0, u1), (acc0, acc1), body)


def _merge(h, att, hm, pre_g, wga, wgm, wa, wm, wo, post_g):
    n = h.shape[0]
    tm, tn = MERGE_TM, MERGE_TN
    n_tiles, n_steps, n_slices = n // tm, D_MODEL // tn, tm // SLICE_ROWS
    assert n_steps >= n_slices
    lead, lag, step, tile = _pipeline_maps(n_tiles, n_slices, n_steps)
    return pl.pallas_call(
        _merge_kernel,
        name="merge",
        grid=(n_tiles + 2, n_steps),
        in_specs=[
            pl.BlockSpec((SLICE_ROWS, D_MODEL), lead),
            pl.BlockSpec((SLICE_ROWS, D_MODEL), lag),
            pl.BlockSpec((tm, ATT_Q), tile),
            pl.BlockSpec((tm, M_V), tile),
            pl.BlockSpec((1, D_MODEL), lambda r, j: (0, 0)),
            pl.BlockSpec((tn, D_MODEL), lambda r, j: (step(r, j), 0)),
            pl.BlockSpec((tn, D_MODEL), lambda r, j: (step(r, j), 0)),
            pl.BlockSpec((ATT_Q, tn), lambda r, j: (0, step(r, j))),
            pl.BlockSpec((M_V, tn), lambda r, j: (0, step(r, j))),
            pl.BlockSpec((tn, D_MODEL), lambda r, j: (step(r, j), 0)),
            pl.BlockSpec((1, D_MODEL), lambda r, j: (0, 0)),
        ],
        out_specs=pl.BlockSpec((SLICE_ROWS, D_MODEL), lag),
        out_shape=jax.ShapeDtypeStruct((n, D_MODEL), F32),
        scratch_shapes=[pltpu.VMEM((tm, D_MODEL), BF16), pltpu.VMEM((tm, D_MODEL), BF16),
                        pltpu.VMEM((tm, D_MODEL), F32), pltpu.VMEM((tm, D_MODEL), F32)],
        compiler_params=pltpu.CompilerParams(
            dimension_semantics=("arbitrary", "arbitrary"), vmem_limit_bytes=VMEM_LIMIT),
    )(h, h, att, hm, pre_g, wga, wgm, wa, wm, wo, post_g)


def kernel(x, ffn1_pre_g, ffn1_wg, ffn1_wu, ffn1_wd, ffn1_post_g, mix_pre_g, w_in, conv_w, conv_b,
           attn_sinks, m_igate_b, m_fgate_b, m_head_g, w_attn_up, w_mlstm_up, w_out, mix_post_g,
           ffn2_pre_g, ffn2_wg, ffn2_wu, ffn2_wd, ffn2_post_g):
    batch, seq, _ = x.shape
    h = x.reshape(batch * seq, D_MODEL)
    for l in range(ffn1_wg.shape[0]):
        bf = lambda w: w.astype(BF16)
        gate0 = W_IN_MAIN + GATE_ROWS
        w_in_t = bf(jnp.swapaxes(w_in[l], 0, 1))
        w_ga_t = w_in_t[gate0:gate0 + D_MODEL]
        w_gm_t = w_in_t[gate0 + D_MODEL:gate0 + 2 * D_MODEL]
        gbias = jnp.concatenate(
            [m_igate_b[l], m_fgate_b[l], jnp.zeros((GATE_PAD - 2 * M_HEADS,), F32)])[None, :]
        slopes = jnp.exp2(-8.0 * jnp.arange(1, ATT_HEADS + 1, dtype=F32) / ATT_HEADS)
        scal = jnp.stack([attn_sinks[l].astype(F32), slopes])

        h = _ffn(h, ffn1_pre_g[l][None, :], ffn1_wg[l], ffn1_wu[l], ffn1_wd[l],
                 ffn1_post_g[l][None, :])
        zb, zf, gt = _inproj(h, mix_pre_g[l][None, :], w_in_t)
        att, w_au, w_mu, w_o = _attention(zb, scal, batch, seq,
                                          side=(w_attn_up[l], w_mlstm_up[l], w_out[l]))
        hm = _mlstm(zb, zf, gt, conv_w[l], conv_b[l][None, :], gbias, m_head_g[l][None, :],
                    batch, seq)
        h = _merge(h, att, hm, mix_pre_g[l][None, :], w_ga_t, w_gm_t, w_au, w_mu, w_o,
                   mix_post_g[l][None, :])
        h = _ffn(h, ffn2_pre_g[l][None, :], ffn2_wg[l], ffn2_wu[l], ffn2_wd[l],
                 ffn2_post_g[l][None, :])
    return h.reshape(batch, seq, D_MODEL)
```

```python
import functools
import math

import jax
import jax.numpy as jnp
from jax import lax
from jax.experimental import pallas as pl
from jax.experimental.pallas import tpu as pltpu

F32 = jnp.float32
BF16 = jnp.bfloat16

D_MODEL = 2048
ATT_HEADS = 16
ATT_KV_HEADS = 4
ATT_HEAD_DIM = 64
ATT_GROUP = ATT_HEADS // ATT_KV_HEADS
WINDOW = 128
ATT_BLOCK = 128
M_HEADS = 4
M_QK_DIM = 128
M_V_DIM = 256
CONV_WIDTH = 4
D_FF = 5632
EPS = 1e-6

ATT_Q = ATT_HEADS * ATT_HEAD_DIM
ATT_KV = ATT_KV_HEADS * ATT_HEAD_DIM
M_QK = M_HEADS * M_QK_DIM
M_V = M_HEADS * M_V_DIM

LANES = 128
GATE_PAD = LANES
GATE_ROWS = 2 * M_HEADS
ZB_COLS = ATT_Q + M_V + 2 * ATT_KV
ZF_COLS = 2 * M_QK + M_V + GATE_PAD

VMEM_LIMIT = 62 * 1024 * 1024

FFN_TM = 1024
FFN_TF = 512
PROJ_TM = 512
PROJ_TN = 512
ATT_TQ = 512
M_CHUNK = 256
M_SEQS = 4
MERGE_TM = 512
MERGE_TN = 512


def _rms(x, g):
    return x * lax.rsqrt(jnp.mean(x * x, axis=-1, keepdims=True) + EPS) * g


def _log_sigmoid(x):
    return jnp.minimum(x, 0.0) - jnp.log1p(jnp.exp(-jnp.abs(x)))


SLICE_ROWS = 128


def _slice_index(f, n_slices):
    return jnp.minimum(f, n_slices - 1)


def _norm_slices(n_slices, res_scale, xa_ref, xb_ref, pre_g_ref, post_g_ref, o_ref, xn_next, acc_prev):
    s = _slice_index(pl.program_id(1), n_slices)
    rows = pl.ds(pl.multiple_of(s * SLICE_ROWS, SLICE_ROWS), SLICE_ROWS)
    xn_next[rows, :] = _rms(xa_ref[...], pre_g_ref[...]).astype(BF16)
    o_ref[...] = xb_ref[...] + res_scale * _rms(acc_prev[rows, :], post_g_ref[...])


def _for_row_roles(xn_bufs, acc_bufs, body):
    r = pl.program_id(0)
    last = pl.num_programs(0) - 1

    @pl.when((r == 0) & (pl.program_id(1) == 0))
    def _():
        for acc in acc_bufs:
            acc[...] = jnp.zeros_like(acc)

    main = (r > 0) & (r < last)
    for p in range(2):
        roles = (xn_bufs[p], acc_bufs[p], xn_bufs[1 - p], acc_bufs[1 - p])
        pl.when(main & (r % 2 == p))(functools.partial(body, *roles, True))
        pl.when(jnp.logical_not(main) & (r % 2 == p))(functools.partial(body, *roles, False))


def _pipeline_maps(n_tiles, n_slices, n_steps):
    def lead(r, f):
        return (jnp.minimum(r, n_tiles - 1) * n_slices + _slice_index(f, n_slices), 0)

    def lag(r, f):
        return (jnp.where(r < 2, 0, (r - 2) * n_slices + _slice_index(f, n_slices)), 0)

    def step(r, f):
        return jnp.where(r == 0, 0, jnp.where(r == n_tiles + 1, n_steps - 1, f))

    def tile(r, f):
        return (jnp.clip(r - 1, 0, n_tiles - 1), 0)

    return lead, lag, step, tile


CAST_ROWS = 16


def _ffn_kernel(xa_ref, xb_ref, pre_g_ref, wg_ref, wu_ref, wd_ref, post_g_ref, o_ref,
                xn0, xn1, acc0, acc1):
    f = pl.program_id(1)

    n_slices = FFN_TM // SLICE_ROWS

    def body(xn_next, acc_prev, xn_cur, acc_cur, do_main):
        _norm_slices(n_slices, 0.5, xa_ref, xb_ref, pre_g_ref, post_g_ref, o_ref, xn_next, acc_prev)
        if do_main:
            xn = xn_cur[...]
            g = jnp.dot(xn, wg_ref[...].astype(BF16), preferred_element_type=F32)
            u = jnp.dot(xn, wu_ref[...].astype(BF16), preferred_element_type=F32)
            hid = (g * jax.nn.sigmoid(g) * u).astype(BF16)
            acc_cur[...] = (jnp.where(f == 0, 0.0, acc_cur[...])
                            + jnp.dot(hid, wd_ref[...].astype(BF16), preferred_element_type=F32))

    _for_row_roles((xn0, xn1), (acc0, acc1), body)


def _ffn(h, pre_g, wg, wu, wd, post_g):
    n = h.shape[0]
    tm, tf = FFN_TM, FFN_TF
    n_tiles, n_steps, n_slices = n // tm, D_FF // tf, tm // SLICE_ROWS
    assert n_steps >= n_slices
    lead, lag, step, _ = _pipeline_maps(n_tiles, n_slices, n_steps)
    return pl.pallas_call(
        _ffn_kernel,
        name="ffn",
        grid=(n_tiles + 2, n_steps),
        in_specs=[
            pl.BlockSpec((SLICE_ROWS, D_MODEL), lead),
            pl.BlockSpec((SLICE_ROWS, D_MODEL), lag),
            pl.BlockSpec((1, D_MODEL), lambda r, f: (0, 0)),
            pl.BlockSpec((D_MODEL, tf), lambda r, f: (0, step(r, f))),
            pl.BlockSpec((D_MODEL, tf), lambda r, f: (0, step(r, f))),
            pl.BlockSpec((tf, D_MODEL), lambda r, f: (step(r, f), 0)),
            pl.BlockSpec((1, D_MODEL), lambda r, f: (0, 0)),
        ],
        out_specs=pl.BlockSpec((SLICE_ROWS, D_MODEL), lag),
        out_shape=jax.ShapeDtypeStruct((n, D_MODEL), F32),
        scratch_shapes=[pltpu.VMEM((tm, D_MODEL), BF16), pltpu.VMEM((tm, D_MODEL), BF16),
                        pltpu.VMEM((tm, D_MODEL), F32), pltpu.VMEM((tm, D_MODEL), F32)],
        compiler_params=pltpu.CompilerParams(
            dimension_semantics=("arbitrary", "arbitrary"), vmem_limit_bytes=VMEM_LIMIT),
    )(h, h, pre_g, wg, wu, wd, post_g)


_OFF_AK = ATT_Q
_OFF_MQ = _OFF_AK + 2 * ATT_KV
_OFF_MV = _OFF_MQ + 2 * M_QK
_OFF_MO = _OFF_MV + M_V
W_IN_MAIN = _OFF_MO + M_V
_PROJ_SEGMENTS = (
    (0, 0, 0, ATT_Q),
    (0, ATT_Q, _OFF_MV, M_V),
    (0, ATT_Q + M_V, _OFF_AK, 2 * ATT_KV),
    (1, 0, _OFF_MQ, 2 * M_QK),
    (1, 2 * M_QK, _OFF_MO, M_V),
)


def _dot_t(x, wt):
    return lax.dot_general(x, wt, (((1,), (1,)), ((), ())), preferred_element_type=F32)


def _inproj_kernel(h_ref, g_ref, wt_ref, wtgate_ref, zb_ref, zf_ref, gt_ref, xn_ref):
    xn_ref[...] = _rms(h_ref[...], g_ref[...]).astype(BF16)
    xn = xn_ref[...]
    outs = (zb_ref, zf_ref)
    for which, out0, w0, width in _PROJ_SEGMENTS:
        for c in range(0, width, PROJ_TN):
            z = _dot_t(xn, wt_ref[w0 + c:w0 + c + PROJ_TN, :])
            outs[which][:, out0 + c:out0 + c + PROJ_TN] = z.astype(outs[which].dtype)
    zg = _dot_t(xn, wtgate_ref[...])
    zf_ref[:, 2 * M_QK + M_V:] = zg
    gt_ref[...] = zg.T[0:GATE_ROWS, :]


def _inproj(h, g, wt):
    n = h.shape[0]
    tm = PROJ_TM
    return pl.pallas_call(
        _inproj_kernel,
        name="inproj",
        grid=(n // tm,),
        in_specs=[
            pl.BlockSpec((tm, D_MODEL), lambda i: (i, 0)),
            pl.BlockSpec((1, D_MODEL), lambda i: (0, 0)),
            pl.BlockSpec((W_IN_MAIN, D_MODEL), lambda i: (0, 0), pipeline_mode=pl.Buffered(1)),
            pl.BlockSpec((GATE_PAD, D_MODEL), lambda i: (W_IN_MAIN // GATE_PAD, 0),
                         pipeline_mode=pl.Buffered(1)),
        ],
        out_specs=[
            pl.BlockSpec((tm, ZB_COLS), lambda i: (i, 0)),
            pl.BlockSpec((tm, ZF_COLS), lambda i: (i, 0)),
            pl.BlockSpec((GATE_ROWS, tm), lambda i: (0, i)),
        ],
        out_shape=[jax.ShapeDtypeStruct((n, ZB_COLS), BF16),
                   jax.ShapeDtypeStruct((n, ZF_COLS), F32),
                   jax.ShapeDtypeStruct((GATE_ROWS, n), F32)],
        scratch_shapes=[pltpu.VMEM((tm, D_MODEL), BF16)],
        compiler_params=pltpu.CompilerParams(
            dimension_semantics=("parallel",), vmem_limit_bytes=VMEM_LIMIT),
    )(h, g, wt, wt)


def _attn_kernel(n_side, scal_ref, q_ref, kp_ref, kc_ref, vp_ref, vc_ref, *rest):
    side_in, o_ref, side_out, bias_ref = (rest[:n_side], rest[n_side], rest[n_side + 1:-1], rest[-1])
    for src, dst in zip(side_in, side_out):
        dst[...] = src[...].astype(BF16)
    t = pl.program_id(1)
    blk = ATT_BLOCK
    half = ATT_HEAD_DIM
    scale = 1.0 / math.sqrt(ATT_HEAD_DIM)
    neg_inf = jnp.float32(-jnp.inf)

    qi = lax.broadcasted_iota(jnp.int32, (blk, 2 * blk), 0)
    kj = lax.broadcasted_iota(jnp.int32, (blk, 2 * blk), 1)

    @pl.when(t == 0)
    def _():
        dist = qi - kj + blk
        band = (dist >= 0) & (dist < WINDOW)
        distf = dist.astype(F32)
        for h in range(ATT_HEADS):
            bias_ref[h] = jnp.where(band, -(scal_ref[1, h] * distf), neg_inf)

    kk = jnp.concatenate([kp_ref[...], kc_ref[...]], axis=0).astype(F32) * scale
    vv = jnp.concatenate([vp_ref[...], vc_ref[...]], axis=0).astype(F32)
    rows = kk.shape[0]
    lane = lax.broadcasted_iota(jnp.int32, (rows, LANES), 1)
    low = lane < half

    def split_pair(x):
        xr = pltpu.roll(x, half, axis=1)
        zero = jnp.zeros_like(x)
        first = (jnp.where(low, x, zero).astype(BF16), jnp.where(low, zero, xr).astype(BF16))
        second = (jnp.where(low, xr, zero).astype(BF16), jnp.where(low, zero, x).astype(BF16))
        return first, second

    k_lo_hi, v_lo_hi = [], []
    for pair in range(ATT_KV // LANES):
        ka, kb = split_pair(kk[:, pair * LANES:(pair + 1) * LANES])
        va, vb = split_pair(vv[:, pair * LANES:(pair + 1) * LANES])
        k_lo_hi += [ka, kb]
        v_lo_hi += [va, vb]

    lane_q = lax.broadcasted_iota(jnp.int32, (blk, LANES), 1)
    has_prev = (kj >= blk) | (t > 0)

    low_q = lane_q < half
    pairs = [(g, pr) for g in range(ATT_KV_HEADS) for pr in range(ATT_GROUP // 2)]
    for j in range(ATT_TQ // blk):
        r0 = j * blk
        k2 = [jnp.concatenate([lo[r0:r0 + 2 * blk], hi[r0:r0 + 2 * blk]], axis=0)
              for lo, hi in k_lo_hi]
        v2 = [jnp.concatenate([lo[r0:r0 + 2 * blk], hi[r0:r0 + 2 * blk]], axis=0)
              for lo, hi in v_lo_hi]
        s2 = []
        for g, pr in pairs:
            c0 = (g * ATT_GROUP + 2 * pr) * half
            s2.append(lax.dot_general(q_ref[r0:r0 + blk, c0:c0 + LANES], k2[g],
                                      (((1,), (1,)), ((), ())), preferred_element_type=F32))
        p2, inv2 = [], []
        for idx, (g, pr) in enumerate(pairs):
            h0 = g * ATT_GROUP + 2 * pr
            ps, ms, sums = [], [], []
            for e in range(2):
                s = s2[idx][:, e * 2 * blk:(e + 1) * 2 * blk] + bias_ref[h0 + e]
                if j == 0:
                    s = jnp.where(has_prev, s, neg_inf)
                m = jnp.maximum(jnp.max(s, axis=-1, keepdims=True), scal_ref[0, h0 + e])
                p = jnp.exp(s - m)
                ps.append(p.astype(BF16))
                ms.append(scal_ref[0, h0 + e] - m)
                sums.append(jnp.sum(p, axis=-1, keepdims=True))
            den = jnp.where(low_q, sums[0], sums[1]) + jnp.exp(jnp.where(low_q, ms[0], ms[1]))
            inv2.append(1.0 / den)
            p2.append(jnp.concatenate(ps, axis=1))
        for idx, (g, pr) in enumerate(pairs):
            c0 = (g * ATT_GROUP + 2 * pr) * half
            o2 = jnp.dot(p2[idx], v2[g], preferred_element_type=F32) * inv2[idx]
            o_ref[r0:r0 + blk, c0:c0 + LANES] = o2.astype(BF16)


def _attention(zb, scal, batch, seq, side=()):
    n = zb.shape[0]
    tq = ATT_TQ
    nt = seq // tq
    per = tq // ATT_BLOCK
    kcol = (ATT_Q + M_V) // ATT_KV
    vcol = kcol + 1

    def prev_map(col):
        return lambda b, t: (b * (seq // ATT_BLOCK) + jnp.maximum(t * per - 1, 0), col)

    def cur_map(col):
        return lambda b, t: (b * nt + t, col)

    side_specs = []
    for w in side:
        rows = w.shape[0] // (batch * nt)
        assert rows % CAST_ROWS == 0 and rows * batch * nt == w.shape[0]
        side_specs.append(pl.BlockSpec((rows, w.shape[1]), cur_map(0)))
    outs = pl.pallas_call(
        functools.partial(_attn_kernel, len(side)),
        name="attn",
        grid=(batch, nt),
        in_specs=[
            pl.BlockSpec(memory_space=pltpu.SMEM),
            pl.BlockSpec((tq, ATT_Q), cur_map(0)),
            pl.BlockSpec((ATT_BLOCK, ATT_KV), prev_map(kcol)),
            pl.BlockSpec((tq, ATT_KV), cur_map(kcol)),
            pl.BlockSpec((ATT_BLOCK, ATT_KV), prev_map(vcol)),
            pl.BlockSpec((tq, ATT_KV), cur_map(vcol)),
            *side_specs,
        ],
        out_specs=[pl.BlockSpec((tq, ATT_Q), cur_map(0)), *side_specs],
        out_shape=[jax.ShapeDtypeStruct((n, ATT_Q), BF16),
                   *[jax.ShapeDtypeStruct(w.shape, BF16) for w in side]],
        scratch_shapes=[pltpu.VMEM((ATT_HEADS, ATT_BLOCK, 2 * ATT_BLOCK), F32)],
        compiler_params=pltpu.CompilerParams(
            dimension_semantics=("parallel", "arbitrary"), vmem_limit_bytes=VMEM_LIMIT),
    )(scal, zb, zb, zb, zb, zb, *side)
    return outs


def _mlstm_kernel(qk_ref, v_ref, og_ref, gt_ref, *rest):
    gtt_refs = rest[:M_SEQS]
    (convw_ref, convb_ref, gbias_ref, gbias_col_ref, headg_ref,
     o_ref, xbuf_ref, cbuf_ref, c_ref, n_ref, m_ref) = rest[M_SEQS:]
    L = M_CHUNK
    pad = 8

    @pl.when(pl.program_id(1) == 0)
    def _():
        xbuf_ref[:, 0:pad, :] = jnp.zeros((M_SEQS, pad, 2 * M_QK), F32)
        cbuf_ref[:, 0:pad, :] = jnp.zeros((M_SEQS, pad, 2 * M_QK), F32)
        c_ref[...] = jnp.zeros_like(c_ref)
        n_ref[...] = jnp.zeros_like(n_ref)
        m_ref[...] = jnp.zeros_like(m_ref)

    w = convw_ref[...]
    assert CONV_WIDTH == 4
    qk, gl, lf, gl_t, lf_t = [], [], [], [], []
    for b in range(M_SEQS):
        xbuf_ref[b, pad:pad + L, :] = qk_ref[b]
        x0 = xbuf_ref[b, pad:pad + L, :]
        x1 = xbuf_ref[b, pl.ds(pad - 1, L), :]
        cbuf_ref[b, pad:pad + L, :] = x0 * w[1:2, :] + x1 * w[0:1, :]
        y = (x0 * w[3:4, :] + convb_ref[...]) + x1 * w[2:3, :] + cbuf_ref[b, pl.ds(pad - 2, L), :]
        xbuf_ref[b, 0:pad, :] = xbuf_ref[b, L:L + pad, :]
        cbuf_ref[b, 0:pad, :] = cbuf_ref[b, L:L + pad, :]
        qk.append(y * jax.nn.sigmoid(y))
        gl.append(gt_ref[b] + gbias_ref[...])
        lf.append(_log_sigmoid(gl[b]))
        gl_t.append(gtt_refs[b][...] + gbias_col_ref[...])
        lf_t.append(_log_sigmoid(gl_t[b]))

    ti = lax.broadcasted_iota(jnp.int32, (L, L), 0)
    si = lax.broadcasted_iota(jnp.int32, (L, L), 1)
    tril = si <= ti
    triu = ti <= si
    neg_inf = jnp.float32(-jnp.inf)
    kscale = 1.0 / math.sqrt(M_QK_DIM)

    units = [(b, h) for b in range(M_SEQS) for h in range(M_HEADS)]
    heads = range(len(units))
    q_f = [qk[b][:, h * M_QK_DIM:(h + 1) * M_QK_DIM] for b, h in units]
    k_f = [qk[b][:, M_QK + h * M_QK_DIM:M_QK + (h + 1) * M_QK_DIM] * kscale for b, h in units]
    v_b = [v_ref[b, :, h * M_V_DIM:(h + 1) * M_V_DIM] for b, h in units]
    q_b = [q.astype(BF16) for q in q_f]
    k_b = [k.astype(BF16) for k in k_f]
    m_in = [m_ref[u, 0:1, 0:1] for u in heads]
    c_in = [c_ref[u] for u in heads]
    n_in = [n_ref[u] for u in heads]

    m_t, inter_w, pexp, kwt, decay, m_new = [], [], [], [], [], []
    for b, h in units:
        li_col = gl[b][:, h:h + 1]
        lf_col = lf[b][:, M_HEADS + h:M_HEADS + h + 1]
        li_row = gl_t[b][h:h + 1, :]
        lf_row = lf_t[b][M_HEADS + h:M_HEADS + h + 1, :]
        h = b * M_HEADS + h
        bc = jnp.sum(jnp.where(tril, lf_row, 0.0), axis=1, keepdims=True)
        br = jnp.sum(jnp.where(triu, lf_col, 0.0), axis=0, keepdims=True)
        b_last = bc[L - 1:L, :]
        c_row = li_row - br
        big_m = jnp.maximum(m_in[h], jnp.max(jnp.where(tril, c_row, neg_inf), axis=1, keepdims=True))
        inter_w.append(jnp.exp(m_in[h] - big_m))
        pexp.append(jnp.exp(jnp.where(tril, c_row - big_m, neg_inf)))
        m_t.append(bc + big_m)
        m_last = big_m[L - 1:L, :]
        decay.append(jnp.exp(m_in[h] - m_last))
        kwt.append(jnp.exp(li_col - bc - m_last))
        m_new.append(b_last + m_last)

    p = [lax.dot_general(q_b[h], k_b[h], (((1,), (1,)), ((), ())), preferred_element_type=F32)
         * pexp[h] for h in heads]
    hh = []
    for h in heads:
        num = (jnp.dot(p[h].astype(BF16), v_b[h], preferred_element_type=F32)
               + inter_w[h] * jnp.dot(q_b[h], c_in[h].astype(BF16), preferred_element_type=F32))
        den = (jnp.sum(p[h], axis=1, keepdims=True)
               + inter_w[h] * jnp.sum(q_f[h] * n_in[h], axis=1, keepdims=True))
        hh.append(num / jnp.maximum(jnp.abs(den), jnp.exp(-m_t[h])))
    for h in heads:
        kw = k_f[h] * kwt[h]
        c_ref[h] = decay[h] * c_in[h] + lax.dot_general(
            kw.astype(BF16), v_b[h], (((0,), (0,)), ((), ())), preferred_element_type=F32)
        n_ref[h] = decay[h] * n_in[h] + jnp.sum(kw, axis=0, keepdims=True)
        m_ref[h] = jnp.broadcast_to(m_new[h], m_ref.shape[1:])
    for u, (b, h) in enumerate(units):
        cs = slice(h * M_V_DIM, (h + 1) * M_V_DIM)
        hn = _rms(hh[u], headg_ref[:, cs])
        o_ref[b, :, cs] = (jax.nn.sigmoid(og_ref[b, :, cs]) * hn).astype(BF16)


def _mlstm(zb, zf, gt, conv_w, conv_b, gbias, head_g, batch, seq):
    n = zb.shape[0]
    L = M_CHUNK
    nc = seq // L
    nb = M_SEQS
    assert batch % nb == 0
    zb3 = zb.reshape(batch, seq, ZB_COLS)
    zf3 = zf.reshape(batch, seq, ZF_COLS)
    const = lambda g, c: (0, 0)

    def gtt_spec(b):
        return pl.BlockSpec((GATE_ROWS, L), lambda g, c: (0, (g * nb + b) * nc + c))

    out = pl.pallas_call(
        _mlstm_kernel,
        name="mlstm",
        grid=(batch // nb, nc),
        in_specs=[
            pl.BlockSpec((nb, L, 2 * M_QK), lambda g, c: (g, c, 0)),
            pl.BlockSpec((nb, L, M_V), lambda g, c: (g, c, ATT_Q // M_V)),
            pl.BlockSpec((nb, L, M_V), lambda g, c: (g, c, 2 * M_QK // M_V)),
            pl.BlockSpec((nb, L, GATE_PAD), lambda g, c: (g, c, (2 * M_QK + M_V) // GATE_PAD)),
            *[gtt_spec(b) for b in range(nb)],
            pl.BlockSpec((CONV_WIDTH, 2 * M_QK), const),
            pl.BlockSpec((1, 2 * M_QK), const),
            pl.BlockSpec((1, GATE_PAD), const),
            pl.BlockSpec((GATE_ROWS, 1), const),
            pl.BlockSpec((1, M_V), const),
        ],
        out_specs=pl.BlockSpec((nb, L, M_V), lambda g, c: (g, c, 0)),
        out_shape=jax.ShapeDtypeStruct((batch, seq, M_V), BF16),
        scratch_shapes=[
            pltpu.VMEM((nb, L + 8, 2 * M_QK), F32),
            pltpu.VMEM((nb, L + 8, 2 * M_QK), F32),
            pltpu.VMEM((nb * M_HEADS, M_QK_DIM, M_V_DIM), F32),
            pltpu.VMEM((nb * M_HEADS, 1, M_QK_DIM), F32),
            pltpu.VMEM((nb * M_HEADS, 8, LANES), F32),
        ],
        compiler_params=pltpu.CompilerParams(
            dimension_semantics=("parallel", "arbitrary"), vmem_limit_bytes=VMEM_LIMIT),
    )(zf3, zb3, zf3, zf3, *([gt] * nb), conv_w, conv_b, gbias, gbias[0, :GATE_ROWS, None], head_g)
    return out.reshape(n, M_V)


def _merge_kernel(ha_ref, hb_ref, att_ref, hm_ref, pre_g_ref, wga_ref, wgm_ref, wa_ref, wm_ref, wo_ref,
                  post_g_ref, o_ref, u0, u1, acc0, acc1):
    j = pl.program_id(1)

    n_slices = MERGE_TM // SLICE_ROWS

    def body(u_next, acc_prev, u_cur, acc_cur, do_main):
        _norm_slices(n_slices, 1.0, ha_ref, hb_ref, pre_g_ref, post_g_ref, o_ref, u_next, acc_prev)
        if do_main:
            u = u_cur[...]
            ga = _dot_t(u, wga_ref[...])
            gm = _dot_t(u, wgm_ref[...])
            a = jnp.dot(att_ref[...], wa_ref[...], preferred_element_type=F32)
            m = jnp.dot(hm_ref[...], wm_ref[...], preferred_element_type=F32)
            y = (jax.nn.sigmoid(ga) * a + jax.nn.sigmoid(gm) * m).astype(BF16)
            acc_cur[...] = (jnp.where(j == 0, 0.0, acc_cur[...])
                            + jnp.dot(y, wo_ref[...], preferred_element_type=F32))

    _for_row_roles((u0, u1), (acc0, acc1), body)


def _merge(h, att, hm, pre_g, wga, wgm, wa, wm, wo, post_g):
    n = h.shape[0]
    tm, tn = MERGE_TM, MERGE_TN
    n_tiles, n_steps, n_slices = n // tm, D_MODEL // tn, tm // SLICE_ROWS
    assert n_steps >= n_slices
    lead, lag, step, tile = _pipeline_maps(n_tiles, n_slices, n_steps)
    return pl.pallas_call(
        _merge_kernel,
        name="merge",
        grid=(n_tiles + 2, n_steps),
        in_specs=[
            pl.BlockSpec((SLICE_ROWS, D_MODEL), lead),
            pl.BlockSpec((SLICE_ROWS, D_MODEL), lag),
            pl.BlockSpec((tm, ATT_Q), tile),
            pl.BlockSpec((tm, M_V), tile),
            pl.BlockSpec((1, D_MODEL), lambda r, j: (0, 0)),
            pl.BlockSpec((tn, D_MODEL), lambda r, j: (step(r, j), 0)),
            pl.BlockSpec((tn, D_MODEL), lambda r, j: (step(r, j), 0)),
            pl.BlockSpec((ATT_Q, tn), lambda r, j: (0, step(r, j))),
            pl.BlockSpec((M_V, tn), lambda r, j: (0, step(r, j))),
            pl.BlockSpec((tn, D_MODEL), lambda r, j: (step(r, j), 0)),
            pl.BlockSpec((1, D_MODEL), lambda r, j: (0, 0)),
        ],
        out_specs=pl.BlockSpec((SLICE_ROWS, D_MODEL), lag),
        out_shape=jax.ShapeDtypeStruct((n, D_MODEL), F32),
        scratch_shapes=[pltpu.VMEM((tm, D_MODEL), BF16), pltpu.VMEM((tm, D_MODEL), BF16),
                        pltpu.VMEM((tm, D_MODEL), F32), pltpu.VMEM((tm, D_MODEL), F32)],
        compiler_params=pltpu.CompilerParams(
            dimension_semantics=("arbitrary", "arbitrary"), vmem_limit_bytes=VMEM_LIMIT),
    )(h, h, att, hm, pre_g, wga, wgm, wa, wm, wo, post_g)


def kernel(x, ffn1_pre_g, ffn1_wg, ffn1_wu, ffn1_wd, ffn1_post_g, mix_pre_g, w_in, conv_w, conv_b,
           attn_sinks, m_igate_b, m_fgate_b, m_head_g, w_attn_up, w_mlstm_up, w_out, mix_post_g,
           ffn2_pre_g, ffn2_wg, ffn2_wu, ffn2_wd, ffn2_post_g):
    batch, seq, _ = x.shape
    h = x.reshape(batch * seq, D_MODEL)
    for l in range(ffn1_wg.shape[0]):
        bf = lambda w: w.astype(BF16)
        gate0 = W_IN_MAIN + GATE_ROWS
        w_in_t = bf(jnp.swapaxes(w_in[l], 0, 1))
        w_ga_t = w_in_t[gate0:gate0 + D_MODEL]
        w_gm_t = w_in_t[gate0 + D_MODEL:gate0 + 2 * D_MODEL]
        gbias = jnp.concatenate(
            [m_igate_b[l], m_fgate_b[l], jnp.zeros((GATE_PAD - 2 * M_HEADS,), F32)])[None, :]
        slopes = jnp.exp2(-8.0 * jnp.arange(1, ATT_HEADS + 1, dtype=F32) / ATT_HEADS)
        scal = jnp.stack([attn_sinks[l].astype(F32), slopes])

        h = _ffn(h, ffn1_pre_g[l][None, :], ffn1_wg[l], ffn1_wu[l], ffn1_wd[l],
                 ffn1_post_g[l][None, :])
        zb, zf, gt = _inproj(h, mix_pre_g[l][None, :], w_in_t)
        att, w_au, w_mu, w_o = _attention(zb, scal, batch, seq,
                                          side=(w_attn_up[l], w_mlstm_up[l], w_out[l]))
        hm = _mlstm(zb, zf, gt, conv_w[l], conv_b[l][None, :], gbias, m_head_g[l][None, :],
                    batch, seq)
        h = _merge(h, att, hm, mix_pre_g[l][None, :], w_ga_t, w_gm_t, w_au, w_mu, w_o,
                   mix_post_g[l][None, :])
        h = _ffn(h, ffn2_pre_g[l][None, :], ffn2_wg[l], ffn2_wu[l], ffn2_wd[l],
                 ffn2_post_g[l][None, :])
    return h.reshape(batch, seq, D_MODEL)
```

```python
import functools
import math

import jax
import jax.numpy as jnp
from jax import lax
from jax.experimental import pallas as pl
from jax.experimental.pallas import tpu as pltpu

F32 = jnp.float32
BF16 = jnp.bfloat16

D_MODEL = 2048
ATT_HEADS = 16
ATT_KV_HEADS = 4
ATT_HEAD_DIM = 64
ATT_GROUP = ATT_HEADS // ATT_KV_HEADS
WINDOW = 128
ATT_BLOCK = 128
M_HEADS = 4
M_QK_DIM = 128
M_V_DIM = 256
CONV_WIDTH = 4
D_FF = 5632
EPS = 1e-6

ATT_Q = ATT_HEADS * ATT_HEAD_DIM
ATT_KV = ATT_KV_HEADS * ATT_HEAD_DIM
M_QK = M_HEADS * M_QK_DIM
M_V = M_HEADS * M_V_DIM

LANES = 128
GATE_PAD = LANES
GATE_ROWS = 2 * M_HEADS
ZB_COLS = ATT_Q + M_V + 2 * ATT_KV
ZF_COLS = 2 * M_QK + M_V + GATE_PAD

VMEM_LIMIT = 62 * 1024 * 1024

FFN_TM = 1024
FFN_TF = 512
PROJ_TM = 512
PROJ_TN = 512
ATT_TQ = 512
M_CHUNK = 256
M_SEQS = 4
MERGE_TM = 512
MERGE_TN = 512


def _rms(x, g):
    return x * lax.rsqrt(jnp.mean(x * x, axis=-1, keepdims=True) + EPS) * g


def _log_sigmoid(x):
    return jnp.minimum(x, 0.0) - jnp.log1p(jnp.exp(-jnp.abs(x)))


SLICE_ROWS = 128


def _slice_index(f, n_slices):
    return jnp.minimum(f, n_slices - 1)


def _norm_slices(n_slices, res_scale, xa_ref, xb_ref, pre_g_ref, post_g_ref, o_ref, xn_next, acc_prev):
    s = _slice_index(pl.program_id(1), n_slices)
    rows = pl.ds(pl.multiple_of(s * SLICE_ROWS, SLICE_ROWS), SLICE_ROWS)
    xn_next[rows, :] = _rms(xa_ref[...], pre_g_ref[...]).astype(BF16)
    o_ref[...] = xb_ref[...] + res_scale * _rms(acc_prev[rows, :], post_g_ref[...])


def _for_row_roles(xn_bufs, acc_bufs, body):
    r = pl.program_id(0)
    last = pl.num_programs(0) - 1

    @pl.when((r == 0) & (pl.program_id(1) == 0))
    def _():
        for acc in acc_bufs:
            acc[...] = jnp.zeros_like(acc)

    main = (r > 0) & (r < last)
    for p in range(2):
        roles = (xn_bufs[p], acc_bufs[p], xn_bufs[1 - p], acc_bufs[1 - p])
        pl.when(main & (r % 2 == p))(functools.partial(body, *roles, True))
        pl.when(jnp.logical_not(main) & (r % 2 == p))(functools.partial(body, *roles, False))


def _pipeline_maps(n_tiles, n_slices, n_steps):
    def lead(r, f):
        return (jnp.minimum(r, n_tiles - 1) * n_slices + _slice_index(f, n_slices), 0)

    def lag(r, f):
        return (jnp.where(r < 2, 0, (r - 2) * n_slices + _slice_index(f, n_slices)), 0)

    def step(r, f):
        return jnp.where(r == 0, 0, jnp.where(r == n_tiles + 1, n_steps - 1, f))

    def tile(r, f):
        return (jnp.clip(r - 1, 0, n_tiles - 1), 0)

    return lead, lag, step, tile


CAST_ROWS = 16


def _ffn_kernel(xa_ref, xb_ref, pre_g_ref, wg_ref, wu_ref, wd_ref, post_g_ref, o_ref,
                xn0, xn1, acc0, acc1):
    f = pl.program_id(1)

    n_slices = FFN_TM // SLICE_ROWS

    def body(xn_next, acc_prev, xn_cur, acc_cur, do_main):
        _norm_slices(n_slices, 0.5, xa_ref, xb_ref, pre_g_ref, post_g_ref, o_ref, xn_next, acc_prev)
        if do_main:
            xn = xn_cur[...]
            g = jnp.dot(xn, wg_ref[...].astype(BF16), preferred_element_type=F32)
            u = jnp.dot(xn, wu_ref[...].astype(BF16), preferred_element_type=F32)
            hid = (g * jax.nn.sigmoid(g) * u).astype(BF16)
            acc_cur[...] = (jnp.where(f == 0, 0.0, acc_cur[...])
                            + jnp.dot(hid, wd_ref[...].astype(BF16), preferred_element_type=F32))

    _for_row_roles((xn0, xn1), (acc0, acc1), body)


def _ffn(h, pre_g, wg, wu, wd, post_g):
    n = h.shape[0]
    tm, tf = FFN_TM, FFN_TF
    n_tiles, n_steps, n_slices = n // tm, D_FF // tf, tm // SLICE_ROWS
    assert n_steps >= n_slices
    lead, lag, step, _ = _pipeline_maps(n_tiles, n_slices, n_steps)
    return pl.pallas_call(
        _ffn_kernel,
        name="ffn",
        grid=(n_tiles + 2, n_steps),
        in_specs=[
            pl.BlockSpec((SLICE_ROWS, D_MODEL), lead),
            pl.BlockSpec((SLICE_ROWS, D_MODEL), lag),
            pl.BlockSpec((1, D_MODEL), lambda r, f: (0, 0)),
            pl.BlockSpec((D_MODEL, tf), lambda r, f: (0, step(r, f))),
            pl.BlockSpec((D_MODEL, tf), lambda r, f: (0, step(r, f))),
            pl.BlockSpec((tf, D_MODEL), lambda r, f: (step(r, f), 0)),
            pl.BlockSpec((1, D_MODEL), lambda r, f: (0, 0)),
        ],
        out_specs=pl.BlockSpec((SLICE_ROWS, D_MODEL), lag),
        out_shape=jax.ShapeDtypeStruct((n, D_MODEL), F32),
        scratch_shapes=[pltpu.VMEM((tm, D_MODEL), BF16), pltpu.VMEM((tm, D_MODEL), BF16),
                        pltpu.VMEM((tm, D_MODEL), F32), pltpu.VMEM((tm, D_MODEL), F32)],
        compiler_params=pltpu.CompilerParams(
            dimension_semantics=("arbitrary", "arbitrary"), vmem_limit_bytes=VMEM_LIMIT),
    )(h, h, pre_g, wg, wu, wd, post_g)


_OFF_AK = ATT_Q
_OFF_MQ = _OFF_AK + 2 * ATT_KV
_OFF_MV = _OFF_MQ + 2 * M_QK
_OFF_MO = _OFF_MV + M_V
W_IN_MAIN = _OFF_MO + M_V
_PROJ_SEGMENTS = (
    (0, 0, 0, ATT_Q),
    (0, ATT_Q, _OFF_MV, M_V),
    (0, ATT_Q + M_V, _OFF_AK, 2 * ATT_KV),
    (1, 0, _OFF_MQ, 2 * M_QK),
    (1, 2 * M_QK, _OFF_MO, M_V),
)


def _dot_t(x, wt):
    return lax.dot_general(x, wt, (((1,), (1,)), ((), ())), preferred_element_type=F32)


def _inproj_kernel(h_ref, g_ref, wt_ref, wtgate_ref, zb_ref, zf_ref, gt_ref, xn_ref):
    xn_ref[...] = _rms(h_ref[...], g_ref[...]).astype(BF16)
    xn = xn_ref[...]
    outs = (zb_ref, zf_ref)
    for which, out0, w0, width in _PROJ_SEGMENTS:
        for c in range(0, width, PROJ_TN):
            z = _dot_t(xn, wt_ref[w0 + c:w0 + c + PROJ_TN, :])
            outs[which][:, out0 + c:out0 + c + PROJ_TN] = z.astype(outs[which].dtype)
    zg = _dot_t(xn, wtgate_ref[...])
    zf_ref[:, 2 * M_QK + M_V:] = zg
    gt_ref[...] = zg.T[0:GATE_ROWS, :]


def _inproj(h, g, wt):
    n = h.shape[0]
    tm = PROJ_TM
    return pl.pallas_call(
        _inproj_kernel,
        name="inproj",
        grid=(n // tm,),
        in_specs=[
            pl.BlockSpec((tm, D_MODEL), lambda i: (i, 0)),
            pl.BlockSpec((1, D_MODEL), lambda i: (0, 0)),
            pl.BlockSpec((W_IN_MAIN, D_MODEL), lambda i: (0, 0), pipeline_mode=pl.Buffered(1)),
            pl.BlockSpec((GATE_PAD, D_MODEL), lambda i: (W_IN_MAIN // GATE_PAD, 0),
                         pipeline_mode=pl.Buffered(1)),
        ],
        out_specs=[
            pl.BlockSpec((tm, ZB_COLS), lambda i: (i, 0)),
            pl.BlockSpec((tm, ZF_COLS), lambda i: (i, 0)),
            pl.BlockSpec((GATE_ROWS, tm), lambda i: (0, i)),
        ],
        out_shape=[jax.ShapeDtypeStruct((n, ZB_COLS), BF16),
                   jax.ShapeDtypeStruct((n, ZF_COLS), F32),
                   jax.ShapeDtypeStruct((GATE_ROWS, n), F32)],
        scratch_shapes=[pltpu.VMEM((tm, D_MODEL), BF16)],
        compiler_params=pltpu.CompilerParams(
            dimension_semantics=("parallel",), vmem_limit_bytes=VMEM_LIMIT),
    )(h, g, wt, wt)


def _attn_kernel(n_side, scal_ref, q_ref, kp_ref, kc_ref, vp_ref, vc_ref, *rest):
    side_in, o_ref, side_out, bias_ref = (rest[:n_side], rest[n_side], rest[n_side + 1:-1], rest[-1])
    for src, dst in zip(side_in, side_out):
        dst[...] = src[...].astype(BF16)
    t = pl.program_id(1)
    blk = ATT_BLOCK
    half = ATT_HEAD_DIM
    scale = 1.0 / math.sqrt(ATT_HEAD_DIM)
    neg_inf = jnp.float32(-jnp.inf)

    qi = lax.broadcasted_iota(jnp.int32, (blk, 2 * blk), 0)
    kj = lax.broadcasted_iota(jnp.int32, (blk, 2 * blk), 1)

    @pl.when(t == 0)
    def _():
        dist = qi - kj + blk
        band = (dist >= 0) & (dist < WINDOW)
        distf = dist.astype(F32)
        for h in range(ATT_HEADS):
            bias_ref[h] = jnp.where(band, -(scal_ref[1, h] * distf), neg_inf)

    kk = jnp.concatenate([kp_ref[...], kc_ref[...]], axis=0).astype(F32) * scale
    vv = jnp.concatenate([vp_ref[...], vc_ref[...]], axis=0).astype(F32)
    rows = kk.shape[0]
    lane = lax.broadcasted_iota(jnp.int32, (rows, LANES), 1)
    low = lane < half

    def split_pair(x):
        xr = pltpu.roll(x, half, axis=1)
        zero = jnp.zeros_like(x)
        first = (jnp.where(low, x, zero).astype(BF16), jnp.where(low, zero, xr).astype(BF16))
        second = (jnp.where(low, xr, zero).astype(BF16), jnp.where(low, zero, x).astype(BF16))
        return first, second

    k_lo_hi, v_lo_hi = [], []
    for pair in range(ATT_KV // LANES):
        ka, kb = split_pair(kk[:, pair * LANES:(pair + 1) * LANES])
        va, vb = split_pair(vv[:, pair * LANES:(pair + 1) * LANES])
        k_lo_hi += [ka, kb]
        v_lo_hi += [va, vb]

    lane_q = lax.broadcasted_iota(jnp.int32, (blk, LANES), 1)
    has_prev = (kj >= blk) | (t > 0)

    low_q = lane_q < half
    pairs = [(g, pr) for g in range(ATT_KV_HEADS) for pr in range(ATT_GROUP // 2)]
    for j in range(ATT_TQ // blk):
        r0 = j * blk
        k2 = [jnp.concatenate([lo[r0:r0 + 2 * blk], hi[r0:r0 + 2 * blk]], axis=0)
              for lo, hi in k_lo_hi]
        v2 = [jnp.concatenate([lo[r0:r0 + 2 * blk], hi[r0:r0 + 2 * blk]], axis=0)
              for lo, hi in v_lo_hi]
        s2 = []
        for g, pr in pairs:
            c0 = (g * ATT_GROUP + 2 * pr) * half
            s2.append(lax.dot_general(q_ref[r0:r0 + blk, c0:c0 + LANES], k2[g],
                                      (((1,), (1,)), ((), ())), preferred_element_type=F32))
        p2, inv2 = [], []
        for idx, (g, pr) in enumerate(pairs):
            h0 = g * ATT_GROUP + 2 * pr
            ps, ms, sums = [], [], []
            for e in range(2):
                s = s2[idx][:, e * 2 * blk:(e + 1) * 2 * blk] + bias_ref[h0 + e]
                if j == 0:
                    s = jnp.where(has_prev, s, neg_inf)
                m = jnp.maximum(jnp.max(s, axis=-1, keepdims=True), scal_ref[0, h0 + e])
                p = jnp.exp(s - m)
                ps.append(p.astype(BF16))
                ms.append(scal_ref[0, h0 + e] - m)
                sums.append(jnp.sum(p, axis=-1, keepdims=True))
            den = jnp.where(low_q, sums[0], sums[1]) + jnp.exp(jnp.where(low_q, ms[0], ms[1]))
            inv2.append(1.0 / den)
            p2.append(jnp.concatenate(ps, axis=1))
        for idx, (g, pr) in enumerate(pairs):
            c0 = (g * ATT_GROUP + 2 * pr) * half
            o2 = jnp.dot(p2[idx], v2[g], preferred_element_type=F32) * inv2[idx]
            o_ref[r0:r0 + blk, c0:c0 + LANES] = o2.astype(BF16)


def _attention(zb, scal, batch, seq, side=()):
    n = zb.shape[0]
    tq = ATT_TQ
    nt = seq // tq
    per = tq // ATT_BLOCK
    kcol = (ATT_Q + M_V) // ATT_KV
    vcol = kcol + 1

    def prev_map(col):
        return lambda b, t: (b * (seq // ATT_BLOCK) + jnp.maximum(t * per - 1, 0), col)

    def cur_map(col):
        return lambda b, t: (b * nt + t, col)

    side_specs = []
    for w in side:
        rows = w.shape[0] // (batch * nt)
        assert rows % CAST_ROWS == 0 and rows * batch * nt == w.shape[0]
        side_specs.append(pl.BlockSpec((rows, w.shape[1]), cur_map(0)))
    outs = pl.pallas_call(
        functools.partial(_attn_kernel, len(side)),
        name="attn",
        grid=(batch, nt),
        in_specs=[
            pl.BlockSpec(memory_space=pltpu.SMEM),
            pl.BlockSpec((tq, ATT_Q), cur_map(0)),
            pl.BlockSpec((ATT_BLOCK, ATT_KV), prev_map(kcol)),
            pl.BlockSpec((tq, ATT_KV), cur_map(kcol)),
            pl.BlockSpec((ATT_BLOCK, ATT_KV), prev_map(vcol)),
            pl.BlockSpec((tq, ATT_KV), cur_map(vcol)),
            *side_specs,
        ],
        out_specs=[pl.BlockSpec((tq, ATT_Q), cur_map(0)), *side_specs],
        out_shape=[jax.ShapeDtypeStruct((n, ATT_Q), BF16),
                   *[jax.ShapeDtypeStruct(w.shape, BF16) for w in side]],
        scratch_shapes=[pltpu.VMEM((ATT_HEADS, ATT_BLOCK, 2 * ATT_BLOCK), F32)],
        compiler_params=pltpu.CompilerParams(
            dimension_semantics=("parallel", "arbitrary"), vmem_limit_bytes=VMEM_LIMIT),
    )(scal, zb, zb, zb, zb, zb, *side)
    return outs


def _mlstm_kernel(qk_ref, v_ref, og_ref, gt_ref, *rest):
    gtt_refs = rest[:M_SEQS]
    (convw_ref, convb_ref, gbias_ref, gbias_col_ref, headg_ref,
     o_ref, xbuf_ref, cbuf_ref, c_ref, n_ref, m_ref) = rest[M_SEQS:]
    L = M_CHUNK
    pad = 8

    @pl.when(pl.program_id(1) == 0)
    def _():
        xbuf_ref[:, 0:pad, :] = jnp.zeros((M_SEQS, pad, 2 * M_QK), F32)
        cbuf_ref[:, 0:pad, :] = jnp.zeros((M_SEQS, pad, 2 * M_QK), F32)
        c_ref[...] = jnp.zeros_like(c_ref)
        n_ref[...] = jnp.zeros_like(n_ref)
        m_ref[...] = jnp.zeros_like(m_ref)

    w = convw_ref[...]
    assert CONV_WIDTH == 4
    qk, gl, lf, gl_t, lf_t = [], [], [], [], []
    for b in range(M_SEQS):
        xbuf_ref[b, pad:pad + L, :] = qk_ref[b]
        x0 = xbuf_ref[b, pad:pad + L, :]
        x1 = xbuf_ref[b, pl.ds(pad - 1, L), :]
        cbuf_ref[b, pad:pad + L, :] = x0 * w[1:2, :] + x1 * w[0:1, :]
        y = (x0 * w[3:4, :] + convb_ref[...]) + x1 * w[2:3, :] + cbuf_ref[b, pl.ds(pad - 2, L), :]
        xbuf_ref[b, 0:pad, :] = xbuf_ref[b, L:L + pad, :]
        cbuf_ref[b, 0:pad, :] = cbuf_ref[b, L:L + pad, :]
        qk.append(y * jax.nn.sigmoid(y))
        gl.append(gt_ref[b] + gbias_ref[...])
        lf.append(_log_sigmoid(gl[b]))
        gl_t.append(gtt_refs[b][...] + gbias_col_ref[...])
        lf_t.append(_log_sigmoid(gl_t[b]))

    ti = lax.broadcasted_iota(jnp.int32, (L, L), 0)
    si = lax.broadcasted_iota(jnp.int32, (L, L), 1)
    tril = si <= ti
    triu = ti <= si
    neg_inf = jnp.float32(-jnp.inf)
    kscale = 1.0 / math.sqrt(M_QK_DIM)

    units = [(b, h) for b in range(M_SEQS) for h in range(M_HEADS)]
    heads = range(len(units))
    q_f = [qk[b][:, h * M_QK_DIM:(h + 1) * M_QK_DIM] for b, h in units]
    k_f = [qk[b][:, M_QK + h * M_QK_DIM:M_QK + (h + 1) * M_QK_DIM] * kscale for b, h in units]
    v_b = [v_ref[b, :, h * M_V_DIM:(h + 1) * M_V_DIM] for b, h in units]
    q_b = [q.astype(BF16) for q in q_f]
    k_b = [k.astype(BF16) for k in k_f]
    m_in = [m_ref[u, 0:1, 0:1] for u in heads]
    c_in = [c_ref[u] for u in heads]
    n_in = [n_ref[u] for u in heads]

    m_t, inter_w, pexp, kwt, decay, m_new = [], [], [], [], [], []
    for b, h in units:
        li_col = gl[b][:, h:h + 1]
        lf_col = lf[b][:, M_HEADS + h:M_HEADS + h + 1]
        li_row = gl_t[b][h:h + 1, :]
        lf_row = lf_t[b][M_HEADS + h:M_HEADS + h + 1, :]
        h = b * M_HEADS + h
        bc = jnp.sum(jnp.where(tril, lf_row, 0.0), axis=1, keepdims=True)
        br = jnp.sum(jnp.where(triu, lf_col, 0.0), axis=0, keepdims=True)
        b_last = bc[L - 1:L, :]
        c_row = li_row - br
        big_m = jnp.maximum(m_in[h], jnp.max(jnp.where(tril, c_row, neg_inf), axis=1, keepdims=True))
        inter_w.append(jnp.exp(m_in[h] - big_m))
        pexp.append(jnp.exp(jnp.where(tril, c_row - big_m, neg_inf)))
        m_t.append(bc + big_m)
        m_last = big_m[L - 1:L, :]
        decay.append(jnp.exp(m_in[h] - m_last))
        kwt.append(jnp.exp(li_col - bc - m_last))
        m_new.append(b_last + m_last)

    p = [lax.dot_general(q_b[h], k_b[h], (((1,), (1,)), ((), ())), preferred_element_type=F32)
         * pexp[h] for h in heads]
    hh = []
    for h in heads:
        num = (jnp.dot(p[h].astype(BF16), v_b[h], preferred_element_type=F32)
               + inter_w[h] * jnp.dot(q_b[h], c_in[h].astype(BF16), preferred_element_type=F32))
        den = (jnp.sum(p[h], axis=1, keepdims=True)
               + inter_w[h] * jnp.sum(q_f[h] * n_in[h], axis=1, keepdims=True))
        hh.append(num / jnp.maximum(jnp.abs(den), jnp.exp(-m_t[h])))
    for h in heads:
        kw = k_f[h] * kwt[h]
        c_ref[h] = decay[h] * c_in[h] + lax.dot_general(
            kw.astype(BF16), v_b[h], (((0,), (0,)), ((), ())), preferred_element_type=F32)
        n_ref[h] = decay[h] * n_in[h] + jnp.sum(kw, axis=0, keepdims=True)
        m_ref[h] = jnp.broadcast_to(m_new[h], m_ref.shape[1:])
    for u, (b, h) in enumerate(units):
        cs = slice(h * M_V_DIM, (h + 1) * M_V_DIM)
        hn = _rms(hh[u], headg_ref[:, cs])
        o_ref[b, :, cs] = (jax.nn.sigmoid(og_ref[b, :, cs]) * hn).astype(BF16)


def _mlstm(zb, zf, gt, conv_w, conv_b, gbias, head_g, batch, seq):
    n = zb.shape[0]
    L = M_CHUNK
    nc = seq // L
    nb = M_SEQS
    assert batch % nb == 0
    zb3 = zb.reshape(batch, seq, ZB_COLS)
    zf3 = zf.reshape(batch, seq, ZF_COLS)
    const = lambda g, c: (0, 0)

    def gtt_spec(b):
        return pl.BlockSpec((GATE_ROWS, L), lambda g, c: (0, (g * nb + b) * nc + c))

    out = pl.pallas_call(
        _mlstm_kernel,
        name="mlstm",
        grid=(batch // nb, nc),
        in_specs=[
            pl.BlockSpec((nb, L, 2 * M_QK), lambda g, c: (g, c, 0)),
            pl.BlockSpec((nb, L, M_V), lambda g, c: (g, c, ATT_Q // M_V)),
            pl.BlockSpec((nb, L, M_V), lambda g, c: (g, c, 2 * M_QK // M_V)),
            pl.BlockSpec((nb, L, GATE_PAD), lambda g, c: (g, c, (2 * M_QK + M_V) // GATE_PAD)),
            *[gtt_spec(b) for b in range(nb)],
            pl.BlockSpec((CONV_WIDTH, 2 * M_QK), const),
            pl.BlockSpec((1, 2 * M_QK), const),
            pl.BlockSpec((1, GATE_PAD), const),
            pl.BlockSpec((GATE_ROWS, 1), const),
            pl.BlockSpec((1, M_V), const),
        ],
        out_specs=pl.BlockSpec((nb, L, M_V), lambda g, c: (g, c, 0)),
        out_shape=jax.ShapeDtypeStruct((batch, seq, M_V), BF16),
        scratch_shapes=[
            pltpu.VMEM((nb, L + 8, 2 * M_QK), F32),
            pltpu.VMEM((nb, L + 8, 2 * M_QK), F32),
            pltpu.VMEM((nb * M_HEADS, M_QK_DIM, M_V_DIM), F32),
            pltpu.VMEM((nb * M_HEADS, 1, M_QK_DIM), F32),
            pltpu.VMEM((nb * M_HEADS, 8, LANES), F32),
        ],
        compiler_params=pltpu.CompilerParams(
            dimension_semantics=("parallel", "arbitrary"), vmem_limit_bytes=VMEM_LIMIT),
    )(zf3, zb3, zf3, zf3, *([gt] * nb), conv_w, conv_b, gbias, gbias[0, :GATE_ROWS, None], head_g)
    return out.reshape(n, M_V)


def _merge_kernel(ha_ref, hb_ref, att_ref, hm_ref, pre_g_ref, wga_ref, wgm_ref, wa_ref, wm_ref, wo_ref,
                  post_g_ref, o_ref, u0, u1, y0, y1, acc0, acc1):
    r = pl.program_id(0)
    j = pl.program_id(1)
    n_tiles = pl.num_programs(0) - 3
    n_steps = D_MODEL // MERGE_TN
    n_slices = MERGE_TM // SLICE_ROWS

    @pl.when((r == 0) & (j == 0))
    def _():
        acc0[...] = jnp.zeros_like(acc0)
        acc1[...] = jnp.zeros_like(acc1)

    def body(u_next, u_cur, y_cur, y_prev, acc_cur, acc_prev, do_y, do_o):
        s = _slice_index(j, n_slices)
        rows = pl.ds(pl.multiple_of(s * SLICE_ROWS, SLICE_ROWS), SLICE_ROWS)
        u_next[rows, :] = _rms(ha_ref[...], pre_g_ref[...]).astype(BF16)
        out_rows = jnp.concatenate([acc_prev[k, rows, :] for k in range(n_steps)], axis=1)
        o_ref[...] = hb_ref[...] + _rms(out_rows, post_g_ref[...])
        if do_y:
            u = u_cur[...]
            ga = _dot_t(u, wga_ref[...])
            gm = _dot_t(u, wgm_ref[...])
            a = jnp.dot(att_ref[...], wa_ref[...], preferred_element_type=F32)
            m = jnp.dot(hm_ref[...], wm_ref[...], preferred_element_type=F32)
            y_cur[j] = (jax.nn.sigmoid(ga) * a + jax.nn.sigmoid(gm) * m).astype(BF16)
        if do_o:
            out = jnp.dot(y_prev[0], wo_ref[0:MERGE_TN, :], preferred_element_type=F32)
            for k in range(1, n_steps):
                out = out + jnp.dot(y_prev[k], wo_ref[k * MERGE_TN:(k + 1) * MERGE_TN, :],
                                    preferred_element_type=F32)
            acc_cur[j] = out

    u_bufs, y_bufs, acc_bufs = (u0, u1), (y0, y1), (acc0, acc1)
    has_y = (r >= 1) & (r <= n_tiles)
    has_o = (r >= 2) & (r <= n_tiles + 1)
    for p in range(2):
        roles = (u_bufs[p], u_bufs[1 - p], y_bufs[1 - p], y_bufs[p], acc_bufs[p], acc_bufs[1 - p])
        for do_y in (False, True):
            for do_o in (False, True):
                cond = (r % 2 == p) & (has_y == do_y) & (has_o == do_o)
                pl.when(cond)(functools.partial(body, *roles, do_y, do_o))


def _merge(h, att, hm, pre_g, wga, wgm, wa, wm, wo, post_g):
    n = h.shape[0]
    tm, tn = MERGE_TM, MERGE_TN
    n_tiles, n_steps, n_slices = n // tm, D_MODEL // tn, tm // SLICE_ROWS
    assert n_steps >= n_slices

    def slice_of(tile, j):
        return tile * n_slices + _slice_index(j, n_slices)

    lead = lambda r, j: (slice_of(jnp.minimum(r, n_tiles - 1), j), 0)
    lag = lambda r, j: (jnp.where(r < 3, 0, slice_of(r - 3, j)), 0)
    tile = lambda r, j: (jnp.clip(r - 1, 0, n_tiles - 1), 0)
    step_y = lambda r, j: jnp.where(r < 1, 0, jnp.where(r > n_tiles, n_steps - 1, j))
    step_o = lambda r, j: jnp.where(r < 2, 0, jnp.where(r > n_tiles + 1, n_steps - 1, j))
    return pl.pallas_call(
        _merge_kernel,
        name="merge",
        grid=(n_tiles + 3, n_steps),
        in_specs=[
            pl.BlockSpec((SLICE_ROWS, D_MODEL), lead),
            pl.BlockSpec((SLICE_ROWS, D_MODEL), lag),
            pl.BlockSpec((tm, ATT_Q), tile),
            pl.BlockSpec((tm, M_V), tile),
            pl.BlockSpec((1, D_MODEL), lambda r, j: (0, 0)),
            pl.BlockSpec((tn, D_MODEL), lambda r, j: (step_y(r, j), 0)),
            pl.BlockSpec((tn, D_MODEL), lambda r, j: (step_y(r, j), 0)),
            pl.BlockSpec((ATT_Q, tn), lambda r, j: (0, step_y(r, j))),
            pl.BlockSpec((M_V, tn), lambda r, j: (0, step_y(r, j))),
            pl.BlockSpec((D_MODEL, tn), lambda r, j: (0, step_o(r, j))),
            pl.BlockSpec((1, D_MODEL), lambda r, j: (0, 0)),
        ],
        out_specs=pl.BlockSpec((SLICE_ROWS, D_MODEL), lag),
        out_shape=jax.ShapeDtypeStruct((n, D_MODEL), F32),
        scratch_shapes=[pltpu.VMEM((tm, D_MODEL), BF16), pltpu.VMEM((tm, D_MODEL), BF16),
                        pltpu.VMEM((n_steps, tm, tn), BF16), pltpu.VMEM((n_steps, tm, tn), BF16),
                        pltpu.VMEM((n_steps, tm, tn), F32), pltpu.VMEM((n_steps, tm, tn), F32)],
        compiler_params=pltpu.CompilerParams(
            dimension_semantics=("arbitrary", "arbitrary"), vmem_limit_bytes=VMEM_LIMIT),
    )(h, h, att, hm, pre_g, wga, wgm, wa, wm, wo, post_g)


def kernel(x, ffn1_pre_g, ffn1_wg, ffn1_wu, ffn1_wd, ffn1_post_g, mix_pre_g, w_in, conv_w, conv_b,
           attn_sinks, m_igate_b, m_fgate_b, m_head_g, w_attn_up, w_mlstm_up, w_out, mix_post_g,
           ffn2_pre_g, ffn2_wg, ffn2_wu, ffn2_wd, ffn2_post_g):
    batch, seq, _ = x.shape
    h = x.reshape(batch * seq, D_MODEL)
    for l in range(ffn1_wg.shape[0]):
        bf = lambda w: w.astype(BF16)
        gate0 = W_IN_MAIN + GATE_ROWS
        w_in_t = bf(jnp.swapaxes(w_in[l], 0, 1))
        w_ga_t = w_in_t[gate0:gate0 + D_MODEL]
        w_gm_t = w_in_t[gate0 + D_MODEL:gate0 + 2 * D_MODEL]
        gbias = jnp.concatenate(
            [m_igate_b[l], m_fgate_b[l], jnp.zeros((GATE_PAD - 2 * M_HEADS,), F32)])[None, :]
        slopes = jnp.exp2(-8.0 * jnp.arange(1, ATT_HEADS + 1, dtype=F32) / ATT_HEADS)
        scal = jnp.stack([attn_sinks[l].astype(F32), slopes])

        h = _ffn(h, ffn1_pre_g[l][None, :], ffn1_wg[l], ffn1_wu[l], ffn1_wd[l],
                 ffn1_post_g[l][None, :])
        zb, zf, gt = _inproj(h, mix_pre_g[l][None, :], w_in_t)
        att, w_au, w_mu, w_o = _attention(zb, scal, batch, seq,
                                          side=(w_attn_up[l], w_mlstm_up[l], w_out[l]))
        hm = _mlstm(zb, zf, gt, conv_w[l], conv_b[l][None, :], gbias, m_head_g[l][None, :],
                    batch, seq)
        h = _merge(h, att, hm, mix_pre_g[l][None, :], w_ga_t, w_gm_t, w_au, w_mu, w_o,
                   mix_post_g[l][None, :])
        h = _ffn(h, ffn2_pre_g[l][None, :], ffn2_wg[l], ffn2_wu[l], ffn2_wd[l],
                 ffn2_post_g[l][None, :])
    return h.reshape(batch, seq, D_MODEL)
```

```python
import functools
import math

import jax
import jax.numpy as jnp
from jax import lax
from jax.experimental import pallas as pl
from jax.experimental.pallas import tpu as pltpu

F32 = jnp.float32
BF16 = jnp.bfloat16

D_MODEL = 2048
ATT_HEADS = 16
ATT_KV_HEADS = 4
ATT_HEAD_DIM = 64
ATT_GROUP = ATT_HEADS // ATT_KV_HEADS
WINDOW = 128
ATT_BLOCK = 128
M_HEADS = 4
M_QK_DIM = 128
M_V_DIM = 256
CONV_WIDTH = 4
D_FF = 5632
EPS = 1e-6

ATT_Q = ATT_HEADS * ATT_HEAD_DIM
ATT_KV = ATT_KV_HEADS * ATT_HEAD_DIM
M_QK = M_HEADS * M_QK_DIM
M_V = M_HEADS * M_V_DIM

LANES = 128
GATE_PAD = LANES
GATE_ROWS = 2 * M_HEADS
ZB_COLS = ATT_Q + M_V + 2 * ATT_KV
ZF_COLS = 2 * M_QK + M_V + GATE_PAD

VMEM_LIMIT = 62 * 1024 * 1024

FFN_TM = 1024
FFN_TF = 512
PROJ_TM = 512
PROJ_TN = 512
ATT_TQ = 512
M_CHUNK = 256
M_SEQS = 4
MERGE_TM = 512
MERGE_TN = 512


def _rms(x, g):
    return x * lax.rsqrt(jnp.mean(x * x, axis=-1, keepdims=True) + EPS) * g


def _log_sigmoid(x):
    return jnp.minimum(x, 0.0) - jnp.log1p(jnp.exp(-jnp.abs(x)))


SLICE_ROWS = 128


def _slice_index(f, n_slices):
    return jnp.minimum(f, n_slices - 1)


def _norm_slices(n_slices, res_scale, xa_ref, xb_ref, pre_g_ref, post_g_ref, o_ref, xn_next, acc_prev):
    s = _slice_index(pl.program_id(1), n_slices)
    rows = pl.ds(pl.multiple_of(s * SLICE_ROWS, SLICE_ROWS), SLICE_ROWS)
    xn_next[rows, :] = _rms(xa_ref[...], pre_g_ref[...]).astype(BF16)
    o_ref[...] = xb_ref[...] + res_scale * _rms(acc_prev[rows, :], post_g_ref[...])


def _for_row_roles(xn_bufs, acc_bufs, body):
    r = pl.program_id(0)
    last = pl.num_programs(0) - 1

    @pl.when((r == 0) & (pl.program_id(1) == 0))
    def _():
        for acc in acc_bufs:
            acc[...] = jnp.zeros_like(acc)

    main = (r > 0) & (r < last)
    for p in range(2):
        roles = (xn_bufs[p], acc_bufs[p], xn_bufs[1 - p], acc_bufs[1 - p])
        pl.when(main & (r % 2 == p))(functools.partial(body, *roles, True))
        pl.when(jnp.logical_not(main) & (r % 2 == p))(functools.partial(body, *roles, False))


def _pipeline_maps(n_tiles, n_slices, n_steps):
    def lead(r, f):
        return (jnp.minimum(r, n_tiles - 1) * n_slices + _slice_index(f, n_slices), 0)

    def lag(r, f):
        return (jnp.where(r < 2, 0, (r - 2) * n_slices + _slice_index(f, n_slices)), 0)

    def step(r, f):
        return jnp.where(r == 0, 0, jnp.where(r == n_tiles + 1, n_steps - 1, f))

    return lead, lag, step


CAST_ROWS = 16


def _ffn_kernel(xa_ref, xb_ref, pre_g_ref, wg_ref, wu_ref, wd_ref, post_g_ref, o_ref,
                xn0, xn1, acc0, acc1):
    f = pl.program_id(1)

    n_slices = FFN_TM // SLICE_ROWS

    def body(xn_next, acc_prev, xn_cur, acc_cur, do_main):
        _norm_slices(n_slices, 0.5, xa_ref, xb_ref, pre_g_ref, post_g_ref, o_ref, xn_next, acc_prev)
        if do_main:
            xn = xn_cur[...]
            g = jnp.dot(xn, wg_ref[...].astype(BF16), preferred_element_type=F32)
            u = jnp.dot(xn, wu_ref[...].astype(BF16), preferred_element_type=F32)
            hid = (g * jax.nn.sigmoid(g) * u).astype(BF16)
            acc_cur[...] = (jnp.where(f == 0, 0.0, acc_cur[...])
                            + jnp.dot(hid, wd_ref[...].astype(BF16), preferred_element_type=F32))

    _for_row_roles((xn0, xn1), (acc0, acc1), body)


def _ffn(h, pre_g, wg, wu, wd, post_g):
    n = h.shape[0]
    tm, tf = FFN_TM, FFN_TF
    n_tiles, n_steps, n_slices = n // tm, D_FF // tf, tm // SLICE_ROWS
    assert n_steps >= n_slices
    lead, lag, step = _pipeline_maps(n_tiles, n_slices, n_steps)
    return pl.pallas_call(
        _ffn_kernel,
        name="ffn",
        grid=(n_tiles + 2, n_steps),
        in_specs=[
            pl.BlockSpec((SLICE_ROWS, D_MODEL), lead),
            pl.BlockSpec((SLICE_ROWS, D_MODEL), lag),
            pl.BlockSpec((1, D_MODEL), lambda r, f: (0, 0)),
            pl.BlockSpec((D_MODEL, tf), lambda r, f: (0, step(r, f))),
            pl.BlockSpec((D_MODEL, tf), lambda r, f: (0, step(r, f))),
            pl.BlockSpec((tf, D_MODEL), lambda r, f: (step(r, f), 0)),
            pl.BlockSpec((1, D_MODEL), lambda r, f: (0, 0)),
        ],
        out_specs=pl.BlockSpec((SLICE_ROWS, D_MODEL), lag),
        out_shape=jax.ShapeDtypeStruct((n, D_MODEL), F32),
        scratch_shapes=[pltpu.VMEM((tm, D_MODEL), BF16), pltpu.VMEM((tm, D_MODEL), BF16),
                        pltpu.VMEM((tm, D_MODEL), F32), pltpu.VMEM((tm, D_MODEL), F32)],
        compiler_params=pltpu.CompilerParams(
            dimension_semantics=("arbitrary", "arbitrary"), vmem_limit_bytes=VMEM_LIMIT),
    )(h, h, pre_g, wg, wu, wd, post_g)


_OFF_AK = ATT_Q
_OFF_MQ = _OFF_AK + 2 * ATT_KV
_OFF_MV = _OFF_MQ + 2 * M_QK
_OFF_MO = _OFF_MV + M_V
W_IN_MAIN = _OFF_MO + M_V
_PROJ_SEGMENTS = (
    (0, 0, 0, ATT_Q),
    (0, ATT_Q, _OFF_MV, M_V),
    (0, ATT_Q + M_V, _OFF_AK, 2 * ATT_KV),
    (1, 0, _OFF_MQ, 2 * M_QK),
    (1, 2 * M_QK, _OFF_MO, M_V),
)


def _dot_t(x, wt):
    return lax.dot_general(x, wt, (((1,), (1,)), ((), ())), preferred_element_type=F32)


def _inproj_kernel(h_ref, g_ref, wt_ref, wtgate_ref, zb_ref, zf_ref, gt_ref, xn_ref):
    xn_ref[...] = _rms(h_ref[...], g_ref[...]).astype(BF16)
    xn = xn_ref[...]
    outs = (zb_ref, zf_ref)
    for which, out0, w0, width in _PROJ_SEGMENTS:
        for c in range(0, width, PROJ_TN):
            z = _dot_t(xn, wt_ref[w0 + c:w0 + c + PROJ_TN, :])
            outs[which][:, out0 + c:out0 + c + PROJ_TN] = z.astype(outs[which].dtype)
    zg = _dot_t(xn, wtgate_ref[...])
    zf_ref[:, 2 * M_QK + M_V:] = zg
    gt_ref[...] = zg.T[0:GATE_ROWS, :]


def _inproj(h, g, wt):
    n = h.shape[0]
    tm = PROJ_TM
    return pl.pallas_call(
        _inproj_kernel,
        name="inproj",
        grid=(n // tm,),
        in_specs=[
            pl.BlockSpec((tm, D_MODEL), lambda i: (i, 0)),
            pl.BlockSpec((1, D_MODEL), lambda i: (0, 0)),
            pl.BlockSpec((W_IN_MAIN, D_MODEL), lambda i: (0, 0), pipeline_mode=pl.Buffered(1)),
            pl.BlockSpec((GATE_PAD, D_MODEL), lambda i: (W_IN_MAIN // GATE_PAD, 0),
                         pipeline_mode=pl.Buffered(1)),
        ],
        out_specs=[
            pl.BlockSpec((tm, ZB_COLS), lambda i: (i, 0)),
            pl.BlockSpec((tm, ZF_COLS), lambda i: (i, 0)),
            pl.BlockSpec((GATE_ROWS, tm), lambda i: (0, i)),
        ],
        out_shape=[jax.ShapeDtypeStruct((n, ZB_COLS), BF16),
                   jax.ShapeDtypeStruct((n, ZF_COLS), F32),
                   jax.ShapeDtypeStruct((GATE_ROWS, n), F32)],
        scratch_shapes=[pltpu.VMEM((tm, D_MODEL), BF16)],
        compiler_params=pltpu.CompilerParams(
            dimension_semantics=("parallel",), vmem_limit_bytes=VMEM_LIMIT),
    )(h, g, wt, wt)


def _attn_kernel(n_side, scal_ref, q_ref, kp_ref, kc_ref, vp_ref, vc_ref, *rest):
    side_in, o_ref, side_out, bias_ref = (rest[:n_side], rest[n_side], rest[n_side + 1:-1], rest[-1])
    for src, dst in zip(side_in, side_out):
        dst[...] = src[...].astype(BF16)
    t = pl.program_id(1)
    blk = ATT_BLOCK
    half = ATT_HEAD_DIM
    scale = 1.0 / math.sqrt(ATT_HEAD_DIM)
    neg_inf = jnp.float32(-jnp.inf)

    qi = lax.broadcasted_iota(jnp.int32, (blk, 2 * blk), 0)
    kj = lax.broadcasted_iota(jnp.int32, (blk, 2 * blk), 1)

    @pl.when(t == 0)
    def _():
        dist = qi - kj + blk
        band = (dist >= 0) & (dist < WINDOW)
        distf = dist.astype(F32)
        for h in range(ATT_HEADS):
            bias_ref[h] = jnp.where(band, -(scal_ref[1, h] * distf), neg_inf)

    kk = jnp.concatenate([kp_ref[...], kc_ref[...]], axis=0).astype(F32) * scale
    vv = jnp.concatenate([vp_ref[...], vc_ref[...]], axis=0).astype(F32)
    rows = kk.shape[0]
    lane = lax.broadcasted_iota(jnp.int32, (rows, LANES), 1)
    low = lane < half

    def split_pair(x):
        xr = pltpu.roll(x, half, axis=1)
        zero = jnp.zeros_like(x)
        first = (jnp.where(low, x, zero).astype(BF16), jnp.where(low, zero, xr).astype(BF16))
        second = (jnp.where(low, xr, zero).astype(BF16), jnp.where(low, zero, x).astype(BF16))
        return first, second

    k_lo_hi, v_lo_hi = [], []
    for pair in range(ATT_KV // LANES):
        ka, kb = split_pair(kk[:, pair * LANES:(pair + 1) * LANES])
        va, vb = split_pair(vv[:, pair * LANES:(pair + 1) * LANES])
        k_lo_hi += [ka, kb]
        v_lo_hi += [va, vb]

    lane_q = lax.broadcasted_iota(jnp.int32, (blk, LANES), 1)
    has_prev = (kj >= blk) | (t > 0)

    low_q = lane_q < half
    pairs = [(g, pr) for g in range(ATT_KV_HEADS) for pr in range(ATT_GROUP // 2)]
    for j in range(ATT_TQ // blk):
        r0 = j * blk
        k2 = [jnp.concatenate([lo[r0:r0 + 2 * blk], hi[r0:r0 + 2 * blk]], axis=0)
              for lo, hi in k_lo_hi]
        v2 = [jnp.concatenate([lo[r0:r0 + 2 * blk], hi[r0:r0 + 2 * blk]], axis=0)
              for lo, hi in v_lo_hi]
        s2 = []
        for g, pr in pairs:
            c0 = (g * ATT_GROUP + 2 * pr) * half
            s2.append(lax.dot_general(q_ref[r0:r0 + blk, c0:c0 + LANES], k2[g],
                                      (((1,), (1,)), ((), ())), preferred_element_type=F32))
        p2, inv2 = [], []
        for idx, (g, pr) in enumerate(pairs):
            h0 = g * ATT_GROUP + 2 * pr
            ps, ms, sums = [], [], []
            for e in range(2):
                s = s2[idx][:, e * 2 * blk:(e + 1) * 2 * blk] + bias_ref[h0 + e]
                if j == 0:
                    s = jnp.where(has_prev, s, neg_inf)
                m = jnp.maximum(jnp.max(s, axis=-1, keepdims=True), scal_ref[0, h0 + e])
                p = jnp.exp(s - m)
                ps.append(p.astype(BF16))
                ms.append(scal_ref[0, h0 + e] - m)
                sums.append(jnp.sum(p, axis=-1, keepdims=True))
            den = jnp.where(low_q, sums[0], sums[1]) + jnp.exp(jnp.where(low_q, ms[0], ms[1]))
            inv2.append(1.0 / den)
            p2.append(jnp.concatenate(ps, axis=1))
        for idx, (g, pr) in enumerate(pairs):
            c0 = (g * ATT_GROUP + 2 * pr) * half
            o2 = jnp.dot(p2[idx], v2[g], preferred_element_type=F32) * inv2[idx]
            o_ref[r0:r0 + blk, c0:c0 + LANES] = o2.astype(BF16)


def _attention(zb, scal, batch, seq, side=()):
    n = zb.shape[0]
    tq = ATT_TQ
    nt = seq // tq
    per = tq // ATT_BLOCK
    kcol = (ATT_Q + M_V) // ATT_KV
    vcol = kcol + 1

    def prev_map(col):
        return lambda b, t: (b * (seq // ATT_BLOCK) + jnp.maximum(t * per - 1, 0), col)

    def cur_map(col):
        return lambda b, t: (b * nt + t, col)

    side_specs = []
    for w in side:
        rows = w.shape[0] // (batch * nt)
        assert rows % CAST_ROWS == 0 and rows * batch * nt == w.shape[0]
        side_specs.append(pl.BlockSpec((rows, w.shape[1]), cur_map(0)))
    outs = pl.pallas_call(
        functools.partial(_attn_kernel, len(side)),
        name="attn",
        grid=(batch, nt),
        in_specs=[
            pl.BlockSpec(memory_space=pltpu.SMEM),
            pl.BlockSpec((tq, ATT_Q), cur_map(0)),
            pl.BlockSpec((ATT_BLOCK, ATT_KV), prev_map(kcol)),
            pl.BlockSpec((tq, ATT_KV), cur_map(kcol)),
            pl.BlockSpec((ATT_BLOCK, ATT_KV), prev_map(vcol)),
            pl.BlockSpec((tq, ATT_KV), cur_map(vcol)),
            *side_specs,
        ],
        out_specs=[pl.BlockSpec((tq, ATT_Q), cur_map(0)), *side_specs],
        out_shape=[jax.ShapeDtypeStruct((n, ATT_Q), BF16),
                   *[jax.ShapeDtypeStruct(w.shape, BF16) for w in side]],
        scratch_shapes=[pltpu.VMEM((ATT_HEADS, ATT_BLOCK, 2 * ATT_BLOCK), F32)],
        compiler_params=pltpu.CompilerParams(
            dimension_semantics=("parallel", "arbitrary"), vmem_limit_bytes=VMEM_LIMIT),
    )(scal, zb, zb, zb, zb, zb, *side)
    return outs


def _mlstm_kernel(qk_ref, v_ref, og_ref, gt_ref, *rest):
    gtt_refs = rest[:M_SEQS]
    (convw_ref, convb_ref, gbias_ref, gbias_col_ref, headg_ref,
     o_ref, xbuf_ref, cbuf_ref, c_ref, n_ref, m_ref) = rest[M_SEQS:]
    L = M_CHUNK
    pad = 8

    @pl.when(pl.program_id(1) == 0)
    def _():
        xbuf_ref[:, 0:pad, :] = jnp.zeros((M_SEQS, pad, 2 * M_QK), F32)
        cbuf_ref[:, 0:pad, :] = jnp.zeros((M_SEQS, pad, 2 * M_QK), F32)
        c_ref[...] = jnp.zeros_like(c_ref)
        n_ref[...] = jnp.zeros_like(n_ref)
        m_ref[...] = jnp.zeros_like(m_ref)

    w = convw_ref[...]
    assert CONV_WIDTH == 4
    qk, gl, lf, gl_t, lf_t = [], [], [], [], []
    for b in range(M_SEQS):
        xbuf_ref[b, pad:pad + L, :] = qk_ref[b]
        x0 = xbuf_ref[b, pad:pad + L, :]
        x1 = xbuf_ref[b, pl.ds(pad - 1, L), :]
        cbuf_ref[b, pad:pad + L, :] = x0 * w[1:2, :] + x1 * w[0:1, :]
        y = (x0 * w[3:4, :] + convb_ref[...]) + x1 * w[2:3, :] + cbuf_ref[b, pl.ds(pad - 2, L), :]
        xbuf_ref[b, 0:pad, :] = xbuf_ref[b, L:L + pad, :]
        cbuf_ref[b, 0:pad, :] = cbuf_ref[b, L:L + pad, :]
        qk.append(y * jax.nn.sigmoid(y))
        gl.append(gt_ref[b] + gbias_ref[...])
        lf.append(_log_sigmoid(gl[b]))
        gl_t.append(gtt_refs[b][...] + gbias_col_ref[...])
        lf_t.append(_log_sigmoid(gl_t[b]))

    ti = lax.broadcasted_iota(jnp.int32, (L, L), 0)
    si = lax.broadcasted_iota(jnp.int32, (L, L), 1)
    tril = si <= ti
    triu = ti <= si
    neg_inf = jnp.float32(-jnp.inf)
    kscale = 1.0 / math.sqrt(M_QK_DIM)

    units = [(b, h) for b in range(M_SEQS) for h in range(M_HEADS)]
    heads = range(len(units))
    q_f = [qk[b][:, h * M_QK_DIM:(h + 1) * M_QK_DIM] for b, h in units]
    k_f = [qk[b][:, M_QK + h * M_QK_DIM:M_QK + (h + 1) * M_QK_DIM] * kscale for b, h in units]
    v_b = [v_ref[b, :, h * M_V_DIM:(h + 1) * M_V_DIM] for b, h in units]
    q_b = [q.astype(BF16) for q in q_f]
    k_b = [k.astype(BF16) for k in k_f]
    m_in = [m_ref[u, 0:1, 0:1] for u in heads]
    c_in = [c_ref[u] for u in heads]
    n_in = [n_ref[u] for u in heads]

    m_t, inter_w, pexp, kwt, decay, m_new = [], [], [], [], [], []
    for b, h in units:
        li_col = gl[b][:, h:h + 1]
        lf_col = lf[b][:, M_HEADS + h:M_HEADS + h + 1]
        li_row = gl_t[b][h:h + 1, :]
        lf_row = lf_t[b][M_HEADS + h:M_HEADS + h + 1, :]
        h = b * M_HEADS + h
        bc = jnp.sum(jnp.where(tril, lf_row, 0.0), axis=1, keepdims=True)
        br = jnp.sum(jnp.where(triu, lf_col, 0.0), axis=0, keepdims=True)
        b_last = bc[L - 1:L, :]
        c_row = li_row - br
        big_m = jnp.maximum(m_in[h], jnp.max(jnp.where(tril, c_row, neg_inf), axis=1, keepdims=True))
        inter_w.append(jnp.exp(m_in[h] - big_m))
        pexp.append(jnp.exp(jnp.where(tril, c_row - big_m, neg_inf)))
        m_t.append(bc + big_m)
        m_last = big_m[L - 1:L, :]
        decay.append(jnp.exp(m_in[h] - m_last))
        kwt.append(jnp.exp(li_col - bc - m_last))
        m_new.append(b_last + m_last)

    p = [lax.dot_general(q_b[h], k_b[h], (((1,), (1,)), ((), ())), preferred_element_type=F32)
         * pexp[h] for h in heads]
    hh = []
    for h in heads:
        num = (jnp.dot(p[h].astype(BF16), v_b[h], preferred_element_type=F32)
               + inter_w[h] * jnp.dot(q_b[h], c_in[h].astype(BF16), preferred_element_type=F32))
        den = (jnp.sum(p[h], axis=1, keepdims=True)
               + inter_w[h] * jnp.sum(q_f[h] * n_in[h], axis=1, keepdims=True))
        hh.append(num / jnp.maximum(jnp.abs(den), jnp.exp(-m_t[h])))
    for h in heads:
        kw = k_f[h] * kwt[h]
        c_ref[h] = decay[h] * c_in[h] + lax.dot_general(
            kw.astype(BF16), v_b[h], (((0,), (0,)), ((), ())), preferred_element_type=F32)
        n_ref[h] = decay[h] * n_in[h] + jnp.sum(kw, axis=0, keepdims=True)
        m_ref[h] = jnp.broadcast_to(m_new[h], m_ref.shape[1:])
    for u, (b, h) in enumerate(units):
        cs = slice(h * M_V_DIM, (h + 1) * M_V_DIM)
        hn = _rms(hh[u], headg_ref[:, cs])
        o_ref[b, :, cs] = (jax.nn.sigmoid(og_ref[b, :, cs]) * hn).astype(BF16)


def _mlstm(zb, zf, gt, conv_w, conv_b, gbias, head_g, batch, seq):
    n = zb.shape[0]
    L = M_CHUNK
    nc = seq // L
    nb = M_SEQS
    assert batch % nb == 0
    zb3 = zb.reshape(batch, seq, ZB_COLS)
    zf3 = zf.reshape(batch, seq, ZF_COLS)
    const = lambda g, c: (0, 0)

    def gtt_spec(b):
        return pl.BlockSpec((GATE_ROWS, L), lambda g, c: (0, (g * nb + b) * nc + c))

    out = pl.pallas_call(
        _mlstm_kernel,
        name="mlstm",
        grid=(batch // nb, nc),
        in_specs=[
            pl.BlockSpec((nb, L, 2 * M_QK), lambda g, c: (g, c, 0)),
            pl.BlockSpec((nb, L, M_V), lambda g, c: (g, c, ATT_Q // M_V)),
            pl.BlockSpec((nb, L, M_V), lambda g, c: (g, c, 2 * M_QK // M_V)),
            pl.BlockSpec((nb, L, GATE_PAD), lambda g, c: (g, c, (2 * M_QK + M_V) // GATE_PAD)),
            *[gtt_spec(b) for b in range(nb)],
            pl.BlockSpec((CONV_WIDTH, 2 * M_QK), const),
            pl.BlockSpec((1, 2 * M_QK), const),
            pl.BlockSpec((1, GATE_PAD), const),
            pl.BlockSpec((GATE_ROWS, 1), const),
            pl.BlockSpec((1, M_V), const),
        ],
        out_specs=pl.BlockSpec((nb, L, M_V), lambda g, c: (g, c, 0)),
        out_shape=jax.ShapeDtypeStruct((batch, seq, M_V), BF16),
        scratch_shapes=[
            pltpu.VMEM((nb, L + 8, 2 * M_QK), F32),
            pltpu.VMEM((nb, L + 8, 2 * M_QK), F32),
            pltpu.VMEM((nb * M_HEADS, M_QK_DIM, M_V_DIM), F32),
            pltpu.VMEM((nb * M_HEADS, 1, M_QK_DIM), F32),
            pltpu.VMEM((nb * M_HEADS, 8, LANES), F32),
        ],
        compiler_params=pltpu.CompilerParams(
            dimension_semantics=("parallel", "arbitrary"), vmem_limit_bytes=VMEM_LIMIT),
    )(zf3, zb3, zf3, zf3, *([gt] * nb), conv_w, conv_b, gbias, gbias[0, :GATE_ROWS, None], head_g)
    return out.reshape(n, M_V)


def _merge_kernel(ha_ref, hb_ref, att_ref, hm_ref, pre_g_ref, wga_ref, wgm_ref, wa_ref, wm_ref, wo_ref,
                  post_g_ref, o_ref, u0, u1, y0, y1, acc0, acc1):
    r = pl.program_id(0)
    j = pl.program_id(1)
    n_tiles = pl.num_programs(0) - 3
    n_steps = D_MODEL // MERGE_TN
    n_slices = MERGE_TM // SLICE_ROWS

    @pl.when((r == 0) & (j == 0))
    def _():
        acc0[...] = jnp.zeros_like(acc0)
        acc1[...] = jnp.zeros_like(acc1)

    def body(u_next, u_cur, y_cur, y_prev, acc_cur, acc_prev, do_y, do_o):
        s = _slice_index(j, n_slices)
        rows = pl.ds(pl.multiple_of(s * SLICE_ROWS, SLICE_ROWS), SLICE_ROWS)
        u_next[rows, :] = _rms(ha_ref[...], pre_g_ref[...]).astype(BF16)
        out_rows = jnp.concatenate([acc_prev[k, rows, :] for k in range(n_steps)], axis=1)
        o_ref[...] = hb_ref[...] + _rms(out_rows, post_g_ref[...])
        if do_y:
            u = u_cur[...]
            ga = _dot_t(u, wga_ref[...])
            gm = _dot_t(u, wgm_ref[...])
            a = jnp.dot(att_ref[...], wa_ref[...], preferred_element_type=F32)
            m = jnp.dot(hm_ref[...], wm_ref[...], preferred_element_type=F32)
            y_cur[j] = (jax.nn.sigmoid(ga) * a + jax.nn.sigmoid(gm) * m).astype(BF16)
        if do_o:
            out = jnp.dot(y_prev[0], wo_ref[0:MERGE_TN, :], preferred_element_type=F32)
            for k in range(1, n_steps):
                out = out + jnp.dot(y_prev[k], wo_ref[k * MERGE_TN:(k + 1) * MERGE_TN, :],
                                    preferred_element_type=F32)
            acc_cur[j] = out

    u_bufs, y_bufs, acc_bufs = (u0, u1), (y0, y1), (acc0, acc1)
    has_y = (r >= 1) & (r <= n_tiles)
    has_o = (r >= 2) & (r <= n_tiles + 1)
    for p in range(2):
        roles = (u_bufs[p], u_bufs[1 - p], y_bufs[1 - p], y_bufs[p], acc_bufs[p], acc_bufs[1 - p])
        for do_y in (False, True):
            for do_o in (False, True):
                cond = (r % 2 == p) & (has_y == do_y) & (has_o == do_o)
                pl.when(cond)(functools.partial(body, *roles, do_y, do_o))


def _merge(h, att, hm, pre_g, wga, wgm, wa, wm, wo, post_g):
    n = h.shape[0]
    tm, tn = MERGE_TM, MERGE_TN
    n_tiles, n_steps, n_slices = n // tm, D_MODEL // tn, tm // SLICE_ROWS
    assert n_steps >= n_slices

    def slice_of(tile, j):
        return tile * n_slices + _slice_index(j, n_slices)

    lead = lambda r, j: (slice_of(jnp.minimum(r, n_tiles - 1), j), 0)
    lag = lambda r, j: (jnp.where(r < 3, 0, slice_of(r - 3, j)), 0)
    tile = lambda r, j: (jnp.clip(r - 1, 0, n_tiles - 1), 0)
    step_y = lambda r, j: jnp.where(r < 1, 0, jnp.where(r > n_tiles, n_steps - 1, j))
    step_o = lambda r, j: jnp.where(r < 2, 0, jnp.where(r > n_tiles + 1, n_steps - 1, j))
    return pl.pallas_call(
        _merge_kernel,
        name="merge",
        grid=(n_tiles + 3, n_steps),
        in_specs=[
            pl.BlockSpec((SLICE_ROWS, D_MODEL), lead),
            pl.BlockSpec((SLICE_ROWS, D_MODEL), lag),
            pl.BlockSpec((tm, ATT_Q), tile),
            pl.BlockSpec((tm, M_V), tile),
            pl.BlockSpec((1, D_MODEL), lambda r, j: (0, 0)),
            pl.BlockSpec((tn, D_MODEL), lambda r, j: (step_y(r, j), 0)),
            pl.BlockSpec((tn, D_MODEL), lambda r, j: (step_y(r, j), 0)),
            pl.BlockSpec((ATT_Q, tn), lambda r, j: (0, step_y(r, j))),
            pl.BlockSpec((M_V, tn), lambda r, j: (0, step_y(r, j))),
            pl.BlockSpec((D_MODEL, tn), lambda r, j: (0, step_o(r, j))),
            pl.BlockSpec((1, D_MODEL), lambda r, j: (0, 0)),
        ],
        out_specs=pl.BlockSpec((SLICE_ROWS, D_MODEL), lag),
        out_shape=jax.ShapeDtypeStruct((n, D_MODEL), F32),
        scratch_shapes=[pltpu.VMEM((tm, D_MODEL), BF16), pltpu.VMEM((tm, D_MODEL), BF16),
                        pltpu.VMEM((n_steps, tm, tn), BF16), pltpu.VMEM((n_steps, tm, tn), BF16),
                        pltpu.VMEM((n_steps, tm, tn), F32), pltpu.VMEM((n_steps, tm, tn), F32)],
        compiler_params=pltpu.CompilerParams(
            dimension_semantics=("arbitrary", "arbitrary"), vmem_limit_bytes=VMEM_LIMIT),
    )(h, h, att, hm, pre_g, wga, wgm, wa, wm, wo, post_g)


def kernel(x, ffn1_pre_g, ffn1_wg, ffn1_wu, ffn1_wd, ffn1_post_g, mix_pre_g, w_in, conv_w, conv_b,
           attn_sinks, m_igate_b, m_fgate_b, m_head_g, w_attn_up, w_mlstm_up, w_out, mix_post_g,
           ffn2_pre_g, ffn2_wg, ffn2_wu, ffn2_wd, ffn2_post_g):
    batch, seq, _ = x.shape
    h = x.reshape(batch * seq, D_MODEL)
    for l in range(ffn1_wg.shape[0]):
        bf = lambda w: w.astype(BF16)
        gate0 = W_IN_MAIN + GATE_ROWS
        w_in_t = bf(jnp.swapaxes(w_in[l], 0, 1))
        w_ga_t = w_in_t[gate0:gate0 + D_MODEL]
        w_gm_t = w_in_t[gate0 + D_MODEL:gate0 + 2 * D_MODEL]
        gbias = jnp.concatenate(
            [m_igate_b[l], m_fgate_b[l], jnp.zeros((GATE_PAD - 2 * M_HEADS,), F32)])[None, :]
        slopes = jnp.exp2(-8.0 * jnp.arange(1, ATT_HEADS + 1, dtype=F32) / ATT_HEADS)
        scal = jnp.stack([attn_sinks[l].astype(F32), slopes])

        h = _ffn(h, ffn1_pre_g[l][None, :], ffn1_wg[l], ffn1_wu[l], ffn1_wd[l],
                 ffn1_post_g[l][None, :])
        zb, zf, gt = _inproj(h, mix_pre_g[l][None, :], w_in_t)
        att, w_au, w_mu, w_o = _attention(zb, scal, batch, seq,
                                          side=(w_attn_up[l], w_mlstm_up[l], w_out[l]))
        hm = _mlstm(zb, zf, gt, conv_w[l], conv_b[l][None, :], gbias, m_head_g[l][None, :],
                    batch, seq)
        h = _merge(h, att, hm, mix_pre_g[l][None, :], w_ga_t, w_gm_t, w_au, w_mu, w_o,
                   mix_post_g[l][None, :])
        h = _ffn(h, ffn2_pre_g[l][None, :], ffn2_wg[l], ffn2_wu[l], ffn2_wd[l],
                 ffn2_post_g[l][None, :])
    return h.reshape(batch, seq, D_MODEL)
```

```python
import functools
import math

import jax
import jax.numpy as jnp
from jax import lax
from jax.experimental import pallas as pl
from jax.experimental.pallas import tpu as pltpu

F32 = jnp.float32
BF16 = jnp.bfloat16

D_MODEL = 2048
ATT_HEADS = 16
ATT_KV_HEADS = 4
ATT_HEAD_DIM = 64
ATT_GROUP = ATT_HEADS // ATT_KV_HEADS
WINDOW = 128
ATT_BLOCK = 128
M_HEADS = 4
M_QK_DIM = 128
M_V_DIM = 256
CONV_WIDTH = 4
D_FF = 5632
EPS = 1e-6

ATT_Q = ATT_HEADS * ATT_HEAD_DIM
ATT_KV = ATT_KV_HEADS * ATT_HEAD_DIM
M_QK = M_HEADS * M_QK_DIM
M_V = M_HEADS * M_V_DIM

LANES = 128
GATE_PAD = LANES
GATE_ROWS = 2 * M_HEADS
ZB_COLS = ATT_Q + M_V + 2 * ATT_KV
ZF_COLS = 2 * M_QK + M_V + GATE_PAD

VMEM_LIMIT = 62 * 1024 * 1024

FFN_TM = 1024
FFN_TF = 512
PROJ_TM = 512
PROJ_TN = 512
ATT_TQ = 512
M_CHUNK = 256
M_SEQS = 4
MERGE_TM = 512
MERGE_TN = 512


def _rms(x, g):
    return x * lax.rsqrt(jnp.mean(x * x, axis=-1, keepdims=True) + EPS) * g


def _log_sigmoid(x):
    return jnp.minimum(x, 0.0) - jnp.log1p(jnp.exp(-jnp.abs(x)))


SLICE_ROWS = 128


def _slice_index(f, n_slices):
    return jnp.minimum(f, n_slices - 1)


def _norm_slices(n_slices, res_scale, xa_ref, xb_ref, pre_g_ref, post_g_ref, o_ref, xn_next, acc_prev,
                 next_norm=None):
    s = _slice_index(pl.program_id(1), n_slices)
    rows = pl.ds(pl.multiple_of(s * SLICE_ROWS, SLICE_ROWS), SLICE_ROWS)
    xn_next[rows, :] = _rms(xa_ref[...], pre_g_ref[...]).astype(BF16)
    out = xb_ref[...] + res_scale * _rms(acc_prev[rows, :], post_g_ref[...])
    o_ref[...] = out
    if next_norm is not None:
        next_g_ref, un_ref = next_norm
        un_ref[...] = _rms(out, next_g_ref[...]).astype(BF16)


def _for_row_roles(xn_bufs, acc_bufs, body):
    r = pl.program_id(0)
    last = pl.num_programs(0) - 1

    @pl.when((r == 0) & (pl.program_id(1) == 0))
    def _():
        for acc in acc_bufs:
            acc[...] = jnp.zeros_like(acc)

    main = (r > 0) & (r < last)
    for p in range(2):
        roles = (xn_bufs[p], acc_bufs[p], xn_bufs[1 - p], acc_bufs[1 - p])
        pl.when(main & (r % 2 == p))(functools.partial(body, *roles, True))
        pl.when(jnp.logical_not(main) & (r % 2 == p))(functools.partial(body, *roles, False))


def _pipeline_maps(n_tiles, n_slices, n_steps):
    def lead(r, f):
        return (jnp.minimum(r, n_tiles - 1) * n_slices + _slice_index(f, n_slices), 0)

    def lag(r, f):
        return (jnp.where(r < 2, 0, (r - 2) * n_slices + _slice_index(f, n_slices)), 0)

    def step(r, f):
        return jnp.where(r == 0, 0, jnp.where(r == n_tiles + 1, n_steps - 1, f))

    return lead, lag, step


CAST_ROWS = 16


def _ffn_kernel(emit_next, xa_ref, xb_ref, pre_g_ref, wg_ref, wu_ref, wd_ref, post_g_ref, *rest):
    if emit_next:
        next_g_ref, o_ref, un_ref, xn0, xn1, acc0, acc1 = rest
        next_norm = (next_g_ref, un_ref)
    else:
        o_ref, xn0, xn1, acc0, acc1 = rest
        next_norm = None
    f = pl.program_id(1)

    n_slices = FFN_TM // SLICE_ROWS

    def body(xn_next, acc_prev, xn_cur, acc_cur, do_main):
        _norm_slices(n_slices, 0.5, xa_ref, xb_ref, pre_g_ref, post_g_ref, o_ref, xn_next, acc_prev,
                     next_norm)
        if do_main:
            xn = xn_cur[...]
            g = jnp.dot(xn, wg_ref[...].astype(BF16), preferred_element_type=F32)
            u = jnp.dot(xn, wu_ref[...].astype(BF16), preferred_element_type=F32)
            hid = (g * jax.nn.sigmoid(g) * u).astype(BF16)
            acc_cur[...] = (jnp.where(f == 0, 0.0, acc_cur[...])
                            + jnp.dot(hid, wd_ref[...].astype(BF16), preferred_element_type=F32))

    _for_row_roles((xn0, xn1), (acc0, acc1), body)


def _ffn(h, pre_g, wg, wu, wd, post_g, next_g=None):
    n = h.shape[0]
    tm, tf = FFN_TM, FFN_TF
    n_tiles, n_steps, n_slices = n // tm, D_FF // tf, tm // SLICE_ROWS
    assert n_steps >= n_slices
    lead, lag, step = _pipeline_maps(n_tiles, n_slices, n_steps)
    const = pl.BlockSpec((1, D_MODEL), lambda r, f: (0, 0))
    emit = next_g is not None
    outs = pl.pallas_call(
        functools.partial(_ffn_kernel, emit),
        name="ffn",
        grid=(n_tiles + 2, n_steps),
        in_specs=[
            pl.BlockSpec((SLICE_ROWS, D_MODEL), lead),
            pl.BlockSpec((SLICE_ROWS, D_MODEL), lag),
            const,
            pl.BlockSpec((D_MODEL, tf), lambda r, f: (0, step(r, f))),
            pl.BlockSpec((D_MODEL, tf), lambda r, f: (0, step(r, f))),
            pl.BlockSpec((tf, D_MODEL), lambda r, f: (step(r, f), 0)),
            const,
        ] + ([const] if emit else []),
        out_specs=[pl.BlockSpec((SLICE_ROWS, D_MODEL), lag)] * (2 if emit else 1),
        out_shape=[jax.ShapeDtypeStruct((n, D_MODEL), F32)]
        + ([jax.ShapeDtypeStruct((n, D_MODEL), BF16)] if emit else []),
        scratch_shapes=[pltpu.VMEM((tm, D_MODEL), BF16), pltpu.VMEM((tm, D_MODEL), BF16),
                        pltpu.VMEM((tm, D_MODEL), F32), pltpu.VMEM((tm, D_MODEL), F32)],
        compiler_params=pltpu.CompilerParams(
            dimension_semantics=("arbitrary", "arbitrary"), vmem_limit_bytes=VMEM_LIMIT),
    )(h, h, pre_g, wg, wu, wd, post_g, *([next_g] if emit else []))
    return outs if emit else outs[0]


_OFF_AK = ATT_Q
_OFF_MQ = _OFF_AK + 2 * ATT_KV
_OFF_MV = _OFF_MQ + 2 * M_QK
_OFF_MO = _OFF_MV + M_V
W_IN_MAIN = _OFF_MO + M_V
_PROJ_SEGMENTS = (
    (0, 0, 0, ATT_Q),
    (0, ATT_Q, _OFF_MV, M_V),
    (0, ATT_Q + M_V, _OFF_AK, 2 * ATT_KV),
    (1, 0, _OFF_MQ, 2 * M_QK),
    (1, 2 * M_QK, _OFF_MO, M_V),
)


def _dot_t(x, wt):
    return lax.dot_general(x, wt, (((1,), (1,)), ((), ())), preferred_element_type=F32)


def _inproj_kernel(u_ref, wt_ref, wtgate_ref, zb_ref, zf_ref, gt_ref):
    xn = u_ref[...]
    outs = (zb_ref, zf_ref)
    for which, out0, w0, width in _PROJ_SEGMENTS:
        for c in range(0, width, PROJ_TN):
            z = _dot_t(xn, wt_ref[w0 + c:w0 + c + PROJ_TN, :])
            outs[which][:, out0 + c:out0 + c + PROJ_TN] = z.astype(outs[which].dtype)
    zg = _dot_t(xn, wtgate_ref[...])
    zf_ref[:, 2 * M_QK + M_V:] = zg
    gt_ref[...] = zg.T[0:GATE_ROWS, :]


def _inproj(u, wt):
    n = u.shape[0]
    tm = PROJ_TM
    return pl.pallas_call(
        _inproj_kernel,
        name="inproj",
        grid=(n // tm,),
        in_specs=[
            pl.BlockSpec((tm, D_MODEL), lambda i: (i, 0)),
            pl.BlockSpec((W_IN_MAIN, D_MODEL), lambda i: (0, 0), pipeline_mode=pl.Buffered(1)),
            pl.BlockSpec((GATE_PAD, D_MODEL), lambda i: (W_IN_MAIN // GATE_PAD, 0),
                         pipeline_mode=pl.Buffered(1)),
        ],
        out_specs=[
            pl.BlockSpec((tm, ZB_COLS), lambda i: (i, 0)),
            pl.BlockSpec((tm, ZF_COLS), lambda i: (i, 0)),
            pl.BlockSpec((GATE_ROWS, tm), lambda i: (0, i)),
        ],
        out_shape=[jax.ShapeDtypeStruct((n, ZB_COLS), BF16),
                   jax.ShapeDtypeStruct((n, ZF_COLS), F32),
                   jax.ShapeDtypeStruct((GATE_ROWS, n), F32)],
        compiler_params=pltpu.CompilerParams(
            dimension_semantics=("parallel",), vmem_limit_bytes=VMEM_LIMIT),
    )(u, wt, wt)


def _attn_kernel(n_side, scal_ref, q_ref, kp_ref, kc_ref, vp_ref, vc_ref, *rest):
    side_in, o_ref, side_out, bias_ref = (rest[:n_side], rest[n_side], rest[n_side + 1:-1], rest[-1])
    for src, dst in zip(side_in, side_out):
        dst[...] = src[...].astype(BF16)
    t = pl.program_id(1)
    blk = ATT_BLOCK
    half = ATT_HEAD_DIM
    scale = 1.0 / math.sqrt(ATT_HEAD_DIM)
    neg_inf = jnp.float32(-jnp.inf)

    qi = lax.broadcasted_iota(jnp.int32, (blk, 2 * blk), 0)
    kj = lax.broadcasted_iota(jnp.int32, (blk, 2 * blk), 1)

    @pl.when(t == 0)
    def _():
        dist = qi - kj + blk
        band = (dist >= 0) & (dist < WINDOW)
        distf = dist.astype(F32)
        for h in range(ATT_HEADS):
            bias_ref[h] = jnp.where(band, -(scal_ref[1, h] * distf), neg_inf)

    kk = jnp.concatenate([kp_ref[...], kc_ref[...]], axis=0).astype(F32) * scale
    vv = jnp.concatenate([vp_ref[...], vc_ref[...]], axis=0).astype(F32)
    rows = kk.shape[0]
    lane = lax.broadcasted_iota(jnp.int32, (rows, LANES), 1)
    low = lane < half

    def split_pair(x):
        xr = pltpu.roll(x, half, axis=1)
        zero = jnp.zeros_like(x)
        first = (jnp.where(low, x, zero).astype(BF16), jnp.where(low, zero, xr).astype(BF16))
        second = (jnp.where(low, xr, zero).astype(BF16), jnp.where(low, zero, x).astype(BF16))
        return first, second

    k_lo_hi, v_lo_hi = [], []
    for pair in range(ATT_KV // LANES):
        ka, kb = split_pair(kk[:, pair * LANES:(pair + 1) * LANES])
        va, vb = split_pair(vv[:, pair * LANES:(pair + 1) * LANES])
        k_lo_hi += [ka, kb]
        v_lo_hi += [va, vb]

    lane_q = lax.broadcasted_iota(jnp.int32, (blk, LANES), 1)
    has_prev = (kj >= blk) | (t > 0)

    low_q = lane_q < half
    pairs = [(g, pr) for g in range(ATT_KV_HEADS) for pr in range(ATT_GROUP // 2)]
    for j in range(ATT_TQ // blk):
        r0 = j * blk
        k2 = [jnp.concatenate([lo[r0:r0 + 2 * blk], hi[r0:r0 + 2 * blk]], axis=0)
              for lo, hi in k_lo_hi]
        v2 = [jnp.concatenate([lo[r0:r0 + 2 * blk], hi[r0:r0 + 2 * blk]], axis=0)
              for lo, hi in v_lo_hi]
        s2 = []
        for g, pr in pairs:
            c0 = (g * ATT_GROUP + 2 * pr) * half
            s2.append(lax.dot_general(q_ref[r0:r0 + blk, c0:c0 + LANES], k2[g],
                                      (((1,), (1,)), ((), ())), preferred_element_type=F32))
        p2, inv2 = [], []
        for idx, (g, pr) in enumerate(pairs):
            h0 = g * ATT_GROUP + 2 * pr
            ps, ms, sums = [], [], []
            for e in range(2):
                s = s2[idx][:, e * 2 * blk:(e + 1) * 2 * blk] + bias_ref[h0 + e]
                if j == 0:
                    s = jnp.where(has_prev, s, neg_inf)
                m = jnp.maximum(jnp.max(s, axis=-1, keepdims=True), scal_ref[0, h0 + e])
                p = jnp.exp(s - m)
                ps.append(p.astype(BF16))
                ms.append(scal_ref[0, h0 + e] - m)
                sums.append(jnp.sum(p, axis=-1, keepdims=True))
            den = jnp.where(low_q, sums[0], sums[1]) + jnp.exp(jnp.where(low_q, ms[0], ms[1]))
            inv2.append(1.0 / den)
            p2.append(jnp.concatenate(ps, axis=1))
        for idx, (g, pr) in enumerate(pairs):
            c0 = (g * ATT_GROUP + 2 * pr) * half
            o2 = jnp.dot(p2[idx], v2[g], preferred_element_type=F32) * inv2[idx]
            o_ref[r0:r0 + blk, c0:c0 + LANES] = o2.astype(BF16)


def _attention(zb, scal, batch, seq, side=()):
    n = zb.shape[0]
    tq = ATT_TQ
    nt = seq // tq
    per = tq // ATT_BLOCK
    kcol = (ATT_Q + M_V) // ATT_KV
    vcol = kcol + 1

    def prev_map(col):
        return lambda b, t: (b * (seq // ATT_BLOCK) + jnp.maximum(t * per - 1, 0), col)

    def cur_map(col):
        return lambda b, t: (b * nt + t, col)

    side_specs = []
    for w in side:
        rows = w.shape[0] // (batch * nt)
        assert rows % CAST_ROWS == 0 and rows * batch * nt == w.shape[0]
        side_specs.append(pl.BlockSpec((rows, w.shape[1]), cur_map(0)))
    outs = pl.pallas_call(
        functools.partial(_attn_kernel, len(side)),
        name="attn",
        grid=(batch, nt),
        in_specs=[
            pl.BlockSpec(memory_space=pltpu.SMEM),
            pl.BlockSpec((tq, ATT_Q), cur_map(0)),
            pl.BlockSpec((ATT_BLOCK, ATT_KV), prev_map(kcol)),
            pl.BlockSpec((tq, ATT_KV), cur_map(kcol)),
            pl.BlockSpec((ATT_BLOCK, ATT_KV), prev_map(vcol)),
            pl.BlockSpec((tq, ATT_KV), cur_map(vcol)),
            *side_specs,
        ],
        out_specs=[pl.BlockSpec((tq, ATT_Q), cur_map(0)), *side_specs],
        out_shape=[jax.ShapeDtypeStruct((n, ATT_Q), BF16),
                   *[jax.ShapeDtypeStruct(w.shape, BF16) for w in side]],
        scratch_shapes=[pltpu.VMEM((ATT_HEADS, ATT_BLOCK, 2 * ATT_BLOCK), F32)],
        compiler_params=pltpu.CompilerParams(
            dimension_semantics=("parallel", "arbitrary"), vmem_limit_bytes=VMEM_LIMIT),
    )(scal, zb, zb, zb, zb, zb, *side)
    return outs


def _mlstm_kernel(qk_ref, v_ref, og_ref, gt_ref, *rest):
    gtt_refs = rest[:M_SEQS]
    (convw_ref, convb_ref, gbias_ref, gbias_col_ref, headg_ref,
     o_ref, xbuf_ref, cbuf_ref, c_ref, n_ref, m_ref) = rest[M_SEQS:]
    L = M_CHUNK
    pad = 8

    @pl.when(pl.program_id(1) == 0)
    def _():
        xbuf_ref[:, 0:pad, :] = jnp.zeros((M_SEQS, pad, 2 * M_QK), F32)
        cbuf_ref[:, 0:pad, :] = jnp.zeros((M_SEQS, pad, 2 * M_QK), F32)
        c_ref[...] = jnp.zeros_like(c_ref)
        n_ref[...] = jnp.zeros_like(n_ref)
        m_ref[...] = jnp.zeros_like(m_ref)

    w = convw_ref[...]
    assert CONV_WIDTH == 4
    qk, gl, lf, gl_t, lf_t = [], [], [], [], []
    for b in range(M_SEQS):
        xbuf_ref[b, pad:pad + L, :] = qk_ref[b]
        x0 = xbuf_ref[b, pad:pad + L, :]
        x1 = xbuf_ref[b, pl.ds(pad - 1, L), :]
        cbuf_ref[b, pad:pad + L, :] = x0 * w[1:2, :] + x1 * w[0:1, :]
        y = (x0 * w[3:4, :] + convb_ref[...]) + x1 * w[2:3, :] + cbuf_ref[b, pl.ds(pad - 2, L), :]
        xbuf_ref[b, 0:pad, :] = xbuf_ref[b, L:L + pad, :]
        cbuf_ref[b, 0:pad, :] = cbuf_ref[b, L:L + pad, :]
        qk.append(y * jax.nn.sigmoid(y))
        gl.append(gt_ref[b] + gbias_ref[...])
        lf.append(_log_sigmoid(gl[b]))
        gl_t.append(gtt_refs[b][...] + gbias_col_ref[...])
        lf_t.append(_log_sigmoid(gl_t[b]))

    ti = lax.broadcasted_iota(jnp.int32, (L, L), 0)
    si = lax.broadcasted_iota(jnp.int32, (L, L), 1)
    tril = si <= ti
    triu = ti <= si
    neg_inf = jnp.float32(-jnp.inf)
    kscale = 1.0 / math.sqrt(M_QK_DIM)

    units = [(b, h) for b in range(M_SEQS) for h in range(M_HEADS)]
    heads = range(len(units))
    q_f = [qk[b][:, h * M_QK_DIM:(h + 1) * M_QK_DIM] for b, h in units]
    k_f = [qk[b][:, M_QK + h * M_QK_DIM:M_QK + (h + 1) * M_QK_DIM] * kscale for b, h in units]
    v_b = [v_ref[b, :, h * M_V_DIM:(h + 1) * M_V_DIM] for b, h in units]
    q_b = [q.astype(BF16) for q in q_f]
    k_b = [k.astype(BF16) for k in k_f]
    m_in = [m_ref[u, 0:1, 0:1] for u in heads]
    c_in = [c_ref[u] for u in heads]
    n_in = [n_ref[u] for u in heads]

    m_t, inter_w, pexp, kwt, decay, m_new = [], [], [], [], [], []
    for b, h in units:
        li_col = gl[b][:, h:h + 1]
        lf_col = lf[b][:, M_HEADS + h:M_HEADS + h + 1]
        li_row = gl_t[b][h:h + 1, :]
        lf_row = lf_t[b][M_HEADS + h:M_HEADS + h + 1, :]
        h = b * M_HEADS + h
        bc = jnp.sum(jnp.where(tril, lf_row, 0.0), axis=1, keepdims=True)
        br = jnp.sum(jnp.where(triu, lf_col, 0.0), axis=0, keepdims=True)
        b_last = bc[L - 1:L, :]
        c_row = li_row - br
        big_m = jnp.maximum(m_in[h], jnp.max(jnp.where(tril, c_row, neg_inf), axis=1, keepdims=True))
        inter_w.append(jnp.exp(m_in[h] - big_m))
        pexp.append(jnp.exp(jnp.where(tril, c_row - big_m, neg_inf)))
        m_t.append(bc + big_m)
        m_last = big_m[L - 1:L, :]
        decay.append(jnp.exp(m_in[h] - m_last))
        kwt.append(jnp.exp(li_col - bc - m_last))
        m_new.append(b_last + m_last)

    p = [lax.dot_general(q_b[h], k_b[h], (((1,), (1,)), ((), ())), preferred_element_type=F32)
         * pexp[h] for h in heads]
    hh = []
    for h in heads:
        num = (jnp.dot(p[h].astype(BF16), v_b[h], preferred_element_type=F32)
               + inter_w[h] * jnp.dot(q_b[h], c_in[h].astype(BF16), preferred_element_type=F32))
        den = (jnp.sum(p[h], axis=1, keepdims=True)
               + inter_w[h] * jnp.sum(q_f[h] * n_in[h], axis=1, keepdims=True))
        hh.append(num / jnp.maximum(jnp.abs(den), jnp.exp(-m_t[h])))
    for h in heads:
        kw = k_f[h] * kwt[h]
        c_ref[h] = decay[h] * c_in[h] + lax.dot_general(
            kw.astype(BF16), v_b[h], (((0,), (0,)), ((), ())), preferred_element_type=F32)
        n_ref[h] = decay[h] * n_in[h] + jnp.sum(kw, axis=0, keepdims=True)
        m_ref[h] = jnp.broadcast_to(m_new[h], m_ref.shape[1:])
    for u, (b, h) in enumerate(units):
        cs = slice(h * M_V_DIM, (h + 1) * M_V_DIM)
        hn = _rms(hh[u], headg_ref[:, cs])
        o_ref[b, :, cs] = (jax.nn.sigmoid(og_ref[b, :, cs]) * hn).astype(BF16)


def _mlstm(zb, zf, gt, conv_w, conv_b, gbias, head_g, batch, seq):
    n = zb.shape[0]
    L = M_CHUNK
    nc = seq // L
    nb = M_SEQS
    assert batch % nb == 0
    zb3 = zb.reshape(batch, seq, ZB_COLS)
    zf3 = zf.reshape(batch, seq, ZF_COLS)
    const = lambda g, c: (0, 0)

    def gtt_spec(b):
        return pl.BlockSpec((GATE_ROWS, L), lambda g, c: (0, (g * nb + b) * nc + c))

    out = pl.pallas_call(
        _mlstm_kernel,
        name="mlstm",
        grid=(batch // nb, nc),
        in_specs=[
            pl.BlockSpec((nb, L, 2 * M_QK), lambda g, c: (g, c, 0)),
            pl.BlockSpec((nb, L, M_V), lambda g, c: (g, c, ATT_Q // M_V)),
            pl.BlockSpec((nb, L, M_V), lambda g, c: (g, c, 2 * M_QK // M_V)),
            pl.BlockSpec((nb, L, GATE_PAD), lambda g, c: (g, c, (2 * M_QK + M_V) // GATE_PAD)),
            *[gtt_spec(b) for b in range(nb)],
            pl.BlockSpec((CONV_WIDTH, 2 * M_QK), const),
            pl.BlockSpec((1, 2 * M_QK), const),
            pl.BlockSpec((1, GATE_PAD), const),
            pl.BlockSpec((GATE_ROWS, 1), const),
            pl.BlockSpec((1, M_V), const),
        ],
        out_specs=pl.BlockSpec((nb, L, M_V), lambda g, c: (g, c, 0)),
        out_shape=jax.ShapeDtypeStruct((batch, seq, M_V), BF16),
        scratch_shapes=[
            pltpu.VMEM((nb, L + 8, 2 * M_QK), F32),
            pltpu.VMEM((nb, L + 8, 2 * M_QK), F32),
            pltpu.VMEM((nb * M_HEADS, M_QK_DIM, M_V_DIM), F32),
            pltpu.VMEM((nb * M_HEADS, 1, M_QK_DIM), F32),
            pltpu.VMEM((nb * M_HEADS, 8, LANES), F32),
        ],
        compiler_params=pltpu.CompilerParams(
            dimension_semantics=("parallel", "arbitrary"), vmem_limit_bytes=VMEM_LIMIT),
    )(zf3, zb3, zf3, zf3, *([gt] * nb), conv_w, conv_b, gbias, gbias[0, :GATE_ROWS, None], head_g)
    return out.reshape(n, M_V)


def _merge_kernel(ha_ref, hb_ref, att_ref, hm_ref, pre_g_ref, wga_ref, wgm_ref, wa_ref, wm_ref, wo_ref,
                  post_g_ref, o_ref, u0, u1, y0, y1, acc0, acc1):
    r = pl.program_id(0)
    j = pl.program_id(1)
    n_tiles = pl.num_programs(0) - 3
    n_steps = D_MODEL // MERGE_TN
    n_slices = MERGE_TM // SLICE_ROWS

    @pl.when((r == 0) & (j == 0))
    def _():
        acc0[...] = jnp.zeros_like(acc0)
        acc1[...] = jnp.zeros_like(acc1)

    def body(u_next, u_cur, y_cur, y_prev, acc_cur, acc_prev, do_y, do_o):
        s = _slice_index(j, n_slices)
        rows = pl.ds(pl.multiple_of(s * SLICE_ROWS, SLICE_ROWS), SLICE_ROWS)
        u_next[rows, :] = _rms(ha_ref[...], pre_g_ref[...]).astype(BF16)
        out_rows = jnp.concatenate([acc_prev[k, rows, :] for k in range(n_steps)], axis=1)
        o_ref[...] = hb_ref[...] + _rms(out_rows, post_g_ref[...])
        if do_y:
            u = u_cur[...]
            ga = _dot_t(u, wga_ref[...])
            gm = _dot_t(u, wgm_ref[...])
            a = jnp.dot(att_ref[...], wa_ref[...], preferred_element_type=F32)
            m = jnp.dot(hm_ref[...], wm_ref[...], preferred_element_type=F32)
            y_cur[j] = (jax.nn.sigmoid(ga) * a + jax.nn.sigmoid(gm) * m).astype(BF16)
        if do_o:
            out = jnp.dot(y_prev[0], wo_ref[0:MERGE_TN, :], preferred_element_type=F32)
            for k in range(1, n_steps):
                out = out + jnp.dot(y_prev[k], wo_ref[k * MERGE_TN:(k + 1) * MERGE_TN, :],
                                    preferred_element_type=F32)
            acc_cur[j] = out

    u_bufs, y_bufs, acc_bufs = (u0, u1), (y0, y1), (acc0, acc1)
    has_y = (r >= 1) & (r <= n_tiles)
    has_o = (r >= 2) & (r <= n_tiles + 1)
    for p in range(2):
        roles = (u_bufs[p], u_bufs[1 - p], y_bufs[1 - p], y_bufs[p], acc_bufs[p], acc_bufs[1 - p])
        for do_y in (False, True):
            for do_o in (False, True):
                cond = (r % 2 == p) & (has_y == do_y) & (has_o == do_o)
                pl.when(cond)(functools.partial(body, *roles, do_y, do_o))


def _merge(h, att, hm, pre_g, wga, wgm, wa, wm, wo, post_g):
    n = h.shape[0]
    tm, tn = MERGE_TM, MERGE_TN
    n_tiles, n_steps, n_slices = n // tm, D_MODEL // tn, tm // SLICE_ROWS
    assert n_steps >= n_slices

    def slice_of(tile, j):
        return tile * n_slices + _slice_index(j, n_slices)

    lead = lambda r, j: (slice_of(jnp.minimum(r, n_tiles - 1), j), 0)
    lag = lambda r, j: (jnp.where(r < 3, 0, slice_of(r - 3, j)), 0)
    tile = lambda r, j: (jnp.clip(r - 1, 0, n_tiles - 1), 0)
    step_y = lambda r, j: jnp.where(r < 1, 0, jnp.where(r > n_tiles, n_steps - 1, j))
    step_o = lambda r, j: jnp.where(r < 2, 0, jnp.where(r > n_tiles + 1, n_steps - 1, j))
    return pl.pallas_call(
        _merge_kernel,
        name="merge",
        grid=(n_tiles + 3, n_steps),
        in_specs=[
            pl.BlockSpec((SLICE_ROWS, D_MODEL), lead),
            pl.BlockSpec((SLICE_ROWS, D_MODEL), lag),
            pl.BlockSpec((tm, ATT_Q), tile),
            pl.BlockSpec((tm, M_V), tile),
            pl.BlockSpec((1, D_MODEL), lambda r, j: (0, 0)),
            pl.BlockSpec((tn, D_MODEL), lambda r, j: (step_y(r, j), 0)),
            pl.BlockSpec((tn, D_MODEL), lambda r, j: (step_y(r, j), 0)),
            pl.BlockSpec((ATT_Q, tn), lambda r, j: (0, step_y(r, j))),
            pl.BlockSpec((M_V, tn), lambda r, j: (0, step_y(r, j))),
            pl.BlockSpec((D_MODEL, tn), lambda r, j: (0, step_o(r, j))),
            pl.BlockSpec((1, D_MODEL), lambda r, j: (0, 0)),
        ],
        out_specs=pl.BlockSpec((SLICE_ROWS, D_MODEL), lag),
        out_shape=jax.ShapeDtypeStruct((n, D_MODEL), F32),
        scratch_shapes=[pltpu.VMEM((tm, D_MODEL), BF16), pltpu.VMEM((tm, D_MODEL), BF16),
                        pltpu.VMEM((n_steps, tm, tn), BF16), pltpu.VMEM((n_steps, tm, tn), BF16),
                        pltpu.VMEM((n_steps, tm, tn), F32), pltpu.VMEM((n_steps, tm, tn), F32)],
        compiler_params=pltpu.CompilerParams(
            dimension_semantics=("arbitrary", "arbitrary"), vmem_limit_bytes=VMEM_LIMIT),
    )(h, h, att, hm, pre_g, wga, wgm, wa, wm, wo, post_g)


def kernel(x, ffn1_pre_g, ffn1_wg, ffn1_wu, ffn1_wd, ffn1_post_g, mix_pre_g, w_in, conv_w, conv_b,
           attn_sinks, m_igate_b, m_fgate_b, m_head_g, w_attn_up, w_mlstm_up, w_out, mix_post_g,
           ffn2_pre_g, ffn2_wg, ffn2_wu, ffn2_wd, ffn2_post_g):
    batch, seq, _ = x.shape
    h = x.reshape(batch * seq, D_MODEL)
    for l in range(ffn1_wg.shape[0]):
        bf = lambda w: w.astype(BF16)
        gate0 = W_IN_MAIN + GATE_ROWS
        w_in_t = bf(jnp.swapaxes(w_in[l], 0, 1))
        w_ga_t = w_in_t[gate0:gate0 + D_MODEL]
        w_gm_t = w_in_t[gate0 + D_MODEL:gate0 + 2 * D_MODEL]
        gbias = jnp.concatenate(
            [m_igate_b[l], m_fgate_b[l], jnp.zeros((GATE_PAD - 2 * M_HEADS,), F32)])[None, :]
        slopes = jnp.exp2(-8.0 * jnp.arange(1, ATT_HEADS + 1, dtype=F32) / ATT_HEADS)
        scal = jnp.stack([attn_sinks[l].astype(F32), slopes])

        h, u = _ffn(h, ffn1_pre_g[l][None, :], ffn1_wg[l], ffn1_wu[l], ffn1_wd[l],
                    ffn1_post_g[l][None, :], next_g=mix_pre_g[l][None, :])
        zb, zf, gt = _inproj(u, w_in_t)
        att, w_au, w_mu, w_o = _attention(zb, scal, batch, seq,
                                          side=(w_attn_up[l], w_mlstm_up[l], w_out[l]))
        hm = _mlstm(zb, zf, gt, conv_w[l], conv_b[l][None, :], gbias, m_head_g[l][None, :],
                    batch, seq)
        h = _merge(h, att, hm, mix_pre_g[l][None, :], w_ga_t, w_gm_t, w_au, w_mu, w_o,
                   mix_post_g[l][None, :])
        h = _ffn(h, ffn2_pre_g[l][None, :], ffn2_wg[l], ffn2_wu[l], ffn2_wd[l],
                 ffn2_post_g[l][None, :])
    return h.reshape(batch, seq, D_MODEL)
```

```python
import functools
import math

import jax
import jax.numpy as jnp
from jax import lax
from jax.experimental import pallas as pl
from jax.experimental.pallas import tpu as pltpu

F32 = jnp.float32
BF16 = jnp.bfloat16

D_MODEL = 2048
ATT_HEADS = 16
ATT_KV_HEADS = 4
ATT_HEAD_DIM = 64
ATT_GROUP = ATT_HEADS // ATT_KV_HEADS
WINDOW = 128
ATT_BLOCK = 128
M_HEADS = 4
M_QK_DIM = 128
M_V_DIM = 256
CONV_WIDTH = 4
D_FF = 5632
EPS = 1e-6

ATT_Q = ATT_HEADS * ATT_HEAD_DIM
ATT_KV = ATT_KV_HEADS * ATT_HEAD_DIM
M_QK = M_HEADS * M_QK_DIM
M_V = M_HEADS * M_V_DIM

LANES = 128
GATE_PAD = LANES
GATE_ROWS = 2 * M_HEADS
ZB_COLS = ATT_Q + M_V + 2 * ATT_KV
ZF_COLS = 2 * M_QK + M_V + GATE_PAD

VMEM_LIMIT = 62 * 1024 * 1024

FFN_TM = 1024
FFN_TF = 512
PROJ_TM = 512
PROJ_TN = 512
ATT_TQ = 512
M_CHUNK = 256
M_SEQS = 4
MERGE_TM = 512
MERGE_TN = 512


def _rms(x, g):
    return x * lax.rsqrt(jnp.mean(x * x, axis=-1, keepdims=True) + EPS) * g


def _log_sigmoid(x):
    return jnp.minimum(x, 0.0) - jnp.log1p(jnp.exp(-jnp.abs(x)))


SLICE_ROWS = 128


def _slice_index(f, n_slices):
    return jnp.minimum(f, n_slices - 1)


def _norm_slices(n_slices, res_scale, xa_ref, xb_ref, pre_g_ref, post_g_ref, o_ref, xn_next, acc_prev):
    s = _slice_index(pl.program_id(1), n_slices)
    rows = pl.ds(pl.multiple_of(s * SLICE_ROWS, SLICE_ROWS), SLICE_ROWS)
    xn_next[rows, :] = _rms(xa_ref[...], pre_g_ref[...]).astype(BF16)
    o_ref[...] = xb_ref[...] + res_scale * _rms(acc_prev[rows, :], post_g_ref[...])


def _for_row_roles(xn_bufs, acc_bufs, body):
    r = pl.program_id(0)
    last = pl.num_programs(0) - 1

    @pl.when((r == 0) & (pl.program_id(1) == 0))
    def _():
        for acc in acc_bufs:
            acc[...] = jnp.zeros_like(acc)

    main = (r > 0) & (r < last)
    for p in range(2):
        roles = (xn_bufs[p], acc_bufs[p], xn_bufs[1 - p], acc_bufs[1 - p])
        pl.when(main & (r % 2 == p))(functools.partial(body, *roles, True))
        pl.when(jnp.logical_not(main) & (r % 2 == p))(functools.partial(body, *roles, False))


def _pipeline_maps(n_tiles, n_slices, n_steps):
    def lead(r, f):
        return (jnp.minimum(r, n_tiles - 1) * n_slices + _slice_index(f, n_slices), 0)

    def lag(r, f):
        return (jnp.where(r < 2, 0, (r - 2) * n_slices + _slice_index(f, n_slices)), 0)

    def step(r, f):
        return jnp.where(r == 0, 0, jnp.where(r == n_tiles + 1, n_steps - 1, f))

    return lead, lag, step


CAST_ROWS = 16


def _ffn_kernel(xa_ref, xb_ref, pre_g_ref, wg_ref, wu_ref, wd_ref, post_g_ref, o_ref,
                xn0, xn1, acc0, acc1):
    f = pl.program_id(1)

    n_slices = FFN_TM // SLICE_ROWS

    def body(xn_next, acc_prev, xn_cur, acc_cur, do_main):
        _norm_slices(n_slices, 0.5, xa_ref, xb_ref, pre_g_ref, post_g_ref, o_ref, xn_next, acc_prev)
        if do_main:
            xn = xn_cur[...]
            g = jnp.dot(xn, wg_ref[...].astype(BF16), preferred_element_type=F32)
            u = jnp.dot(xn, wu_ref[...].astype(BF16), preferred_element_type=F32)
            hid = (g * jax.nn.sigmoid(g) * u).astype(BF16)
            acc_cur[...] = (jnp.where(f == 0, 0.0, acc_cur[...])
                            + jnp.dot(hid, wd_ref[...].astype(BF16), preferred_element_type=F32))

    _for_row_roles((xn0, xn1), (acc0, acc1), body)


def _ffn(h, pre_g, wg, wu, wd, post_g):
    n = h.shape[0]
    tm, tf = FFN_TM, FFN_TF
    n_tiles, n_steps, n_slices = n // tm, D_FF // tf, tm // SLICE_ROWS
    assert n_steps >= n_slices
    lead, lag, step = _pipeline_maps(n_tiles, n_slices, n_steps)
    return pl.pallas_call(
        _ffn_kernel,
        name="ffn",
        grid=(n_tiles + 2, n_steps),
        in_specs=[
            pl.BlockSpec((SLICE_ROWS, D_MODEL), lead),
            pl.BlockSpec((SLICE_ROWS, D_MODEL), lag),
            pl.BlockSpec((1, D_MODEL), lambda r, f: (0, 0)),
            pl.BlockSpec((D_MODEL, tf), lambda r, f: (0, step(r, f))),
            pl.BlockSpec((D_MODEL, tf), lambda r, f: (0, step(r, f))),
            pl.BlockSpec((tf, D_MODEL), lambda r, f: (step(r, f), 0)),
            pl.BlockSpec((1, D_MODEL), lambda r, f: (0, 0)),
        ],
        out_specs=pl.BlockSpec((SLICE_ROWS, D_MODEL), lag),
        out_shape=jax.ShapeDtypeStruct((n, D_MODEL), F32),
        scratch_shapes=[pltpu.VMEM((tm, D_MODEL), BF16), pltpu.VMEM((tm, D_MODEL), BF16),
                        pltpu.VMEM((tm, D_MODEL), F32), pltpu.VMEM((tm, D_MODEL), F32)],
        compiler_params=pltpu.CompilerParams(
            dimension_semantics=("arbitrary", "arbitrary"), vmem_limit_bytes=VMEM_LIMIT),
    )(h, h, pre_g, wg, wu, wd, post_g)


_OFF_AK = ATT_Q
_OFF_MQ = _OFF_AK + 2 * ATT_KV
_OFF_MV = _OFF_MQ + 2 * M_QK
_OFF_MO = _OFF_MV + M_V
W_IN_MAIN = _OFF_MO + M_V
_PROJ_SEGMENTS = (
    (0, 0, 0, ATT_Q),
    (0, ATT_Q, _OFF_MV, M_V),
    (0, ATT_Q + M_V, _OFF_AK, 2 * ATT_KV),
    (1, 0, _OFF_MQ, 2 * M_QK),
    (1, 2 * M_QK, _OFF_MO, M_V),
)


def _dot_t(x, wt):
    return lax.dot_general(x, wt, (((1,), (1,)), ((), ())), preferred_element_type=F32)


def _inproj_kernel(h_ref, g_ref, wt_ref, wtgate_ref, zb_ref, zf_ref, gt_ref, xn_ref):
    xn_ref[...] = _rms(h_ref[...], g_ref[...]).astype(BF16)
    xn = xn_ref[...]
    outs = (zb_ref, zf_ref)
    for which, out0, w0, width in _PROJ_SEGMENTS:
        for c in range(0, width, PROJ_TN):
            z = _dot_t(xn, wt_ref[w0 + c:w0 + c + PROJ_TN, :])
            outs[which][:, out0 + c:out0 + c + PROJ_TN] = z.astype(outs[which].dtype)
    zg = _dot_t(xn, wtgate_ref[...])
    zf_ref[:, 2 * M_QK + M_V:] = zg
    gt_ref[...] = zg.T[0:GATE_ROWS, :]


def _inproj(h, g, wt):
    n = h.shape[0]
    tm = PROJ_TM
    return pl.pallas_call(
        _inproj_kernel,
        name="inproj",
        grid=(n // tm,),
        in_specs=[
            pl.BlockSpec((tm, D_MODEL), lambda i: (i, 0)),
            pl.BlockSpec((1, D_MODEL), lambda i: (0, 0)),
            pl.BlockSpec((W_IN_MAIN, D_MODEL), lambda i: (0, 0), pipeline_mode=pl.Buffered(1)),
            pl.BlockSpec((GATE_PAD, D_MODEL), lambda i: (W_IN_MAIN // GATE_PAD, 0),
                         pipeline_mode=pl.Buffered(1)),
        ],
        out_specs=[
            pl.BlockSpec((tm, ZB_COLS), lambda i: (i, 0)),
            pl.BlockSpec((tm, ZF_COLS), lambda i: (i, 0)),
            pl.BlockSpec((GATE_ROWS, tm), lambda i: (0, i)),
        ],
        out_shape=[jax.ShapeDtypeStruct((n, ZB_COLS), BF16),
                   jax.ShapeDtypeStruct((n, ZF_COLS), F32),
                   jax.ShapeDtypeStruct((GATE_ROWS, n), F32)],
        scratch_shapes=[pltpu.VMEM((tm, D_MODEL), BF16)],
        compiler_params=pltpu.CompilerParams(
            dimension_semantics=("parallel",), vmem_limit_bytes=VMEM_LIMIT),
    )(h, g, wt, wt)


def _attn_kernel(n_side, scal_ref, q_ref, kp_ref, kc_ref, vp_ref, vc_ref, *rest):
    side_in, o_ref, side_out, bias_ref = (rest[:n_side], rest[n_side], rest[n_side + 1:-1], rest[-1])
    for src, dst in zip(side_in, side_out):
        dst[...] = src[...].astype(BF16)
    t = pl.program_id(1)
    blk = ATT_BLOCK
    half = ATT_HEAD_DIM
    scale = 1.0 / math.sqrt(ATT_HEAD_DIM)
    neg_inf = jnp.float32(-jnp.inf)

    qi = lax.broadcasted_iota(jnp.int32, (blk, 2 * blk), 0)
    kj = lax.broadcasted_iota(jnp.int32, (blk, 2 * blk), 1)

    @pl.when(t == 0)
    def _():
        dist = qi - kj + blk
        band = (dist >= 0) & (dist < WINDOW)
        distf = dist.astype(F32)
        for h in range(ATT_HEADS):
            bias_ref[h] = jnp.where(band, -(scal_ref[1, h] * distf), neg_inf)

    kk = jnp.concatenate([kp_ref[...], kc_ref[...]], axis=0).astype(F32) * scale
    vv = jnp.concatenate([vp_ref[...], vc_ref[...]], axis=0).astype(F32)
    rows = kk.shape[0]
    lane = lax.broadcasted_iota(jnp.int32, (rows, LANES), 1)
    low = lane < half

    def split_pair(x):
        xr = pltpu.roll(x, half, axis=1)
        zero = jnp.zeros_like(x)
        first = (jnp.where(low, x, zero).astype(BF16), jnp.where(low, zero, xr).astype(BF16))
        second = (jnp.where(low, xr, zero).astype(BF16), jnp.where(low, zero, x).astype(BF16))
        return first, second

    k_lo_hi, v_lo_hi = [], []
    for pair in range(ATT_KV // LANES):
        ka, kb = split_pair(kk[:, pair * LANES:(pair + 1) * LANES])
        va, vb = split_pair(vv[:, pair * LANES:(pair + 1) * LANES])
        k_lo_hi += [ka, kb]
        v_lo_hi += [va, vb]

    lane_q = lax.broadcasted_iota(jnp.int32, (blk, LANES), 1)
    has_prev = (kj >= blk) | (t > 0)

    low_q = lane_q < half
    pairs = [(g, pr) for g in range(ATT_KV_HEADS) for pr in range(ATT_GROUP // 2)]
    for j in range(ATT_TQ // blk):
        r0 = j * blk
        k2 = [jnp.concatenate([lo[r0:r0 + 2 * blk], hi[r0:r0 + 2 * blk]], axis=0)
              for lo, hi in k_lo_hi]
        v2 = [jnp.concatenate([lo[r0:r0 + 2 * blk], hi[r0:r0 + 2 * blk]], axis=0)
              for lo, hi in v_lo_hi]
        s2 = []
        for g, pr in pairs:
            c0 = (g * ATT_GROUP + 2 * pr) * half
            s2.append(lax.dot_general(q_ref[r0:r0 + blk, c0:c0 + LANES], k2[g],
                                      (((1,), (1,)), ((), ())), preferred_element_type=F32))
        p2, inv2 = [], []
        for idx, (g, pr) in enumerate(pairs):
            h0 = g * ATT_GROUP + 2 * pr
            ps, ms, sums = [], [], []
            for e in range(2):
                s = s2[idx][:, e * 2 * blk:(e + 1) * 2 * blk] + bias_ref[h0 + e]
                if j == 0:
                    s = jnp.where(has_prev, s, neg_inf)
                m = jnp.maximum(jnp.max(s, axis=-1, keepdims=True), scal_ref[0, h0 + e])
                p = jnp.exp(s - m)
                ps.append(p.astype(BF16))
                ms.append(scal_ref[0, h0 + e] - m)
                sums.append(jnp.sum(p, axis=-1, keepdims=True))
            den = jnp.where(low_q, sums[0], sums[1]) + jnp.exp(jnp.where(low_q, ms[0], ms[1]))
            inv2.append(1.0 / den)
            p2.append(jnp.concatenate(ps, axis=1))
        for idx, (g, pr) in enumerate(pairs):
            c0 = (g * ATT_GROUP + 2 * pr) * half
            o2 = jnp.dot(p2[idx], v2[g], preferred_element_type=F32) * inv2[idx]
            o_ref[r0:r0 + blk, c0:c0 + LANES] = o2.astype(BF16)


def _attention(zb, scal, batch, seq, side=()):
    n = zb.shape[0]
    tq = ATT_TQ
    nt = seq // tq
    per = tq // ATT_BLOCK
    kcol = (ATT_Q + M_V) // ATT_KV
    vcol = kcol + 1

    def prev_map(col):
        return lambda b, t: (b * (seq // ATT_BLOCK) + jnp.maximum(t * per - 1, 0), col)

    def cur_map(col):
        return lambda b, t: (b * nt + t, col)

    side_specs = []
    for w in side:
        rows = w.shape[0] // (batch * nt)
        assert rows % CAST_ROWS == 0 and rows * batch * nt == w.shape[0]
        side_specs.append(pl.BlockSpec((rows, w.shape[1]), cur_map(0)))
    outs = pl.pallas_call(
        functools.partial(_attn_kernel, len(side)),
        name="attn",
        grid=(batch, nt),
        in_specs=[
            pl.BlockSpec(memory_space=pltpu.SMEM),
            pl.BlockSpec((tq, ATT_Q), cur_map(0)),
            pl.BlockSpec((ATT_BLOCK, ATT_KV), prev_map(kcol)),
            pl.BlockSpec((tq, ATT_KV), cur_map(kcol)),
            pl.BlockSpec((ATT_BLOCK, ATT_KV), prev_map(vcol)),
            pl.BlockSpec((tq, ATT_KV), cur_map(vcol)),
            *side_specs,
        ],
        out_specs=[pl.BlockSpec((tq, ATT_Q), cur_map(0)), *side_specs],
        out_shape=[jax.ShapeDtypeStruct((n, ATT_Q), BF16),
                   *[jax.ShapeDtypeStruct(w.shape, BF16) for w in side]],
        scratch_shapes=[pltpu.VMEM((ATT_HEADS, ATT_BLOCK, 2 * ATT_BLOCK), F32)],
        compiler_params=pltpu.CompilerParams(
            dimension_semantics=("parallel", "arbitrary"), vmem_limit_bytes=VMEM_LIMIT),
    )(scal, zb, zb, zb, zb, zb, *side)
    return outs


def _mlstm_kernel(qk_ref, v_ref, og_ref, gt_ref, *rest):
    gtt_refs = rest[:M_SEQS]
    (convw_ref, convb_ref, gbias_ref, gbias_col_ref, headg_ref,
     o_ref, xbuf_ref, cbuf_ref, c_ref, n_ref, m_ref) = rest[M_SEQS:]
    L = M_CHUNK
    pad = 8

    @pl.when(pl.program_id(1) == 0)
    def _():
        xbuf_ref[:, 0:pad, :] = jnp.zeros((M_SEQS, pad, 2 * M_QK), F32)
        cbuf_ref[:, 0:pad, :] = jnp.zeros((M_SEQS, pad, 2 * M_QK), F32)
        c_ref[...] = jnp.zeros_like(c_ref)
        n_ref[...] = jnp.zeros_like(n_ref)
        m_ref[...] = jnp.zeros_like(m_ref)

    w = convw_ref[...]
    assert CONV_WIDTH == 4
    qk, gl, lf, gl_t, lf_t = [], [], [], [], []
    for b in range(M_SEQS):
        xbuf_ref[b, pad:pad + L, :] = qk_ref[b]
        x0 = xbuf_ref[b, pad:pad + L, :]
        x1 = xbuf_ref[b, pl.ds(pad - 1, L), :]
        cbuf_ref[b, pad:pad + L, :] = x0 * w[1:2, :] + x1 * w[0:1, :]
        y = (x0 * w[3:4, :] + convb_ref[...]) + x1 * w[2:3, :] + cbuf_ref[b, pl.ds(pad - 2, L), :]
        xbuf_ref[b, 0:pad, :] = xbuf_ref[b, L:L + pad, :]
        cbuf_ref[b, 0:pad, :] = cbuf_ref[b, L:L + pad, :]
        qk.append(y * jax.nn.sigmoid(y))
        gl.append(gt_ref[b] + gbias_ref[...])
        lf.append(_log_sigmoid(gl[b]))
        gl_t.append(gtt_refs[b][...] + gbias_col_ref[...])
        lf_t.append(_log_sigmoid(gl_t[b]))

    ti = lax.broadcasted_iota(jnp.int32, (L, L), 0)
    si = lax.broadcasted_iota(jnp.int32, (L, L), 1)
    tril = si <= ti
    triu = ti <= si
    neg_inf = jnp.float32(-jnp.inf)
    kscale = 1.0 / math.sqrt(M_QK_DIM)

    units = [(b, h) for b in range(M_SEQS) for h in range(M_HEADS)]
    heads = range(len(units))
    q_f = [qk[b][:, h * M_QK_DIM:(h + 1) * M_QK_DIM] for b, h in units]
    k_f = [qk[b][:, M_QK + h * M_QK_DIM:M_QK + (h + 1) * M_QK_DIM] * kscale for b, h in units]
    v_b = [v_ref[b, :, h * M_V_DIM:(h + 1) * M_V_DIM] for b, h in units]
    q_b = [q.astype(BF16) for q in q_f]
    k_b = [k.astype(BF16) for k in k_f]
    m_in = [m_ref[u, 0:1, 0:1] for u in heads]
    c_in = [c_ref[u] for u in heads]
    n_in = [n_ref[u] for u in heads]

    m_t, inter_w, pexp, kwt, decay, m_new = [], [], [], [], [], []
    for b, h in units:
        li_col = gl[b][:, h:h + 1]
        lf_col = lf[b][:, M_HEADS + h:M_HEADS + h + 1]
        li_row = gl_t[b][h:h + 1, :]
        lf_row = lf_t[b][M_HEADS + h:M_HEADS + h + 1, :]
        h = b * M_HEADS + h
        bc = jnp.sum(jnp.where(tril, lf_row, 0.0), axis=1, keepdims=True)
        br = jnp.sum(jnp.where(triu, lf_col, 0.0), axis=0, keepdims=True)
        b_last = bc[L - 1:L, :]
        c_row = li_row - br
        big_m = jnp.maximum(m_in[h], jnp.max(jnp.where(tril, c_row, neg_inf), axis=1, keepdims=True))
        inter_w.append(jnp.exp(m_in[h] - big_m))
        pexp.append(jnp.exp(jnp.where(tril, c_row - big_m, neg_inf)))
        m_t.append(bc + big_m)
        m_last = big_m[L - 1:L, :]
        decay.append(jnp.exp(m_in[h] - m_last))
        kwt.append(jnp.exp(li_col - bc - m_last))
        m_new.append(b_last + m_last)

    p = [lax.dot_general(q_b[h], k_b[h], (((1,), (1,)), ((), ())), preferred_element_type=F32)
         * pexp[h] for h in heads]
    hh = []
    for h in heads:
        num = (jnp.dot(p[h].astype(BF16), v_b[h], preferred_element_type=F32)
               + inter_w[h] * jnp.dot(q_b[h], c_in[h].astype(BF16), preferred_element_type=F32))
        den = (jnp.sum(p[h], axis=1, keepdims=True)
               + inter_w[h] * jnp.sum(q_f[h] * n_in[h], axis=1, keepdims=True))
        hh.append(num / jnp.maximum(jnp.abs(den), jnp.exp(-m_t[h])))
    for h in heads:
        kw = k_f[h] * kwt[h]
        c_ref[h] = decay[h] * c_in[h] + lax.dot_general(
            kw.astype(BF16), v_b[h], (((0,), (0,)), ((), ())), preferred_element_type=F32)
        n_ref[h] = decay[h] * n_in[h] + jnp.sum(kw, axis=0, keepdims=True)
        m_ref[h] = jnp.broadcast_to(m_new[h], m_ref.shape[1:])
    for u, (b, h) in enumerate(units):
        cs = slice(h * M_V_DIM, (h + 1) * M_V_DIM)
        hn = _rms(hh[u], headg_ref[:, cs])
        o_ref[b, :, cs] = (jax.nn.sigmoid(og_ref[b, :, cs]) * hn).astype(BF16)


def _mlstm(zb, zf, gt, conv_w, conv_b, gbias, head_g, batch, seq):
    n = zb.shape[0]
    L = M_CHUNK
    nc = seq // L
    nb = M_SEQS
    assert batch % nb == 0
    zb3 = zb.reshape(batch, seq, ZB_COLS)
    zf3 = zf.reshape(batch, seq, ZF_COLS)
    const = lambda g, c: (0, 0)

    def gtt_spec(b):
        return pl.BlockSpec((GATE_ROWS, L), lambda g, c: (0, (g * nb + b) * nc + c))

    out = pl.pallas_call(
        _mlstm_kernel,
        name="mlstm",
        grid=(batch // nb, nc),
        in_specs=[
            pl.BlockSpec((nb, L, 2 * M_QK), lambda g, c: (g, c, 0)),
            pl.BlockSpec((nb, L, M_V), lambda g, c: (g, c, ATT_Q // M_V)),
            pl.BlockSpec((nb, L, M_V), lambda g, c: (g, c, 2 * M_QK // M_V)),
            pl.BlockSpec((nb, L, GATE_PAD), lambda g, c: (g, c, (2 * M_QK + M_V) // GATE_PAD)),
            *[gtt_spec(b) for b in range(nb)],
            pl.BlockSpec((CONV_WIDTH, 2 * M_QK), const),
            pl.BlockSpec((1, 2 * M_QK), const),
            pl.BlockSpec((1, GATE_PAD), const),
            pl.BlockSpec((GATE_ROWS, 1), const),
            pl.BlockSpec((1, M_V), const),
        ],
        out_specs=pl.BlockSpec((nb, L, M_V), lambda g, c: (g, c, 0)),
        out_shape=jax.ShapeDtypeStruct((batch, seq, M_V), BF16),
        scratch_shapes=[
            pltpu.VMEM((nb, L + 8, 2 * M_QK), F32),
            pltpu.VMEM((nb, L + 8, 2 * M_QK), F32),
            pltpu.VMEM((nb * M_HEADS, M_QK_DIM, M_V_DIM), F32),
            pltpu.VMEM((nb * M_HEADS, 1, M_QK_DIM), F32),
            pltpu.VMEM((nb * M_HEADS, 8, LANES), F32),
        ],
        compiler_params=pltpu.CompilerParams(
            dimension_semantics=("parallel", "arbitrary"), vmem_limit_bytes=VMEM_LIMIT),
    )(zf3, zb3, zf3, zf3, *([gt] * nb), conv_w, conv_b, gbias, gbias[0, :GATE_ROWS, None], head_g)
    return out.reshape(n, M_V)


def _merge_kernel(ha_ref, hb_ref, att_ref, hm_ref, pre_g_ref, wga_ref, wgm_ref, wa_ref, wm_ref, wo_ref,
                  post_g_ref, o_ref, u0, u1, y0, y1, acc0, acc1):
    r = pl.program_id(0)
    j = pl.program_id(1)
    n_tiles = pl.num_programs(0) - 3
    n_steps = D_MODEL // MERGE_TN
    n_slices = MERGE_TM // SLICE_ROWS

    @pl.when((r == 0) & (j == 0))
    def _():
        acc0[...] = jnp.zeros_like(acc0)
        acc1[...] = jnp.zeros_like(acc1)

    def body(u_next, u_cur, y_cur, y_prev, acc_cur, acc_prev, do_y, do_o):
        s = _slice_index(j, n_slices)
        rows = pl.ds(pl.multiple_of(s * SLICE_ROWS, SLICE_ROWS), SLICE_ROWS)
        u_next[rows, :] = _rms(ha_ref[...], pre_g_ref[...]).astype(BF16)
        out_rows = jnp.concatenate([acc_prev[k, rows, :] for k in range(n_steps)], axis=1)
        o_ref[...] = hb_ref[...] + _rms(out_rows, post_g_ref[...])
        if do_y:
            u = u_cur[...]
            g2 = _dot_t(u, jnp.concatenate([wga_ref[...], wgm_ref[...]], axis=0))
            ga, gm = g2[:, :MERGE_TN], g2[:, MERGE_TN:]
            a = jnp.dot(att_ref[...], wa_ref[...], preferred_element_type=F32)
            m = jnp.dot(hm_ref[...], wm_ref[...], preferred_element_type=F32)
            y_cur[j] = (jax.nn.sigmoid(ga) * a + jax.nn.sigmoid(gm) * m).astype(BF16)
        if do_o:
            out = jnp.dot(y_prev[0], wo_ref[0:MERGE_TN, :], preferred_element_type=F32)
            for k in range(1, n_steps):
                out = out + jnp.dot(y_prev[k], wo_ref[k * MERGE_TN:(k + 1) * MERGE_TN, :],
                                    preferred_element_type=F32)
            acc_cur[j] = out

    u_bufs, y_bufs, acc_bufs = (u0, u1), (y0, y1), (acc0, acc1)
    has_y = (r >= 1) & (r <= n_tiles)
    has_o = (r >= 2) & (r <= n_tiles + 1)
    for p in range(2):
        roles = (u_bufs[p], u_bufs[1 - p], y_bufs[1 - p], y_bufs[p], acc_bufs[p], acc_bufs[1 - p])
        for do_y in (False, True):
            for do_o in (False, True):
                cond = (r % 2 == p) & (has_y == do_y) & (has_o == do_o)
                pl.when(cond)(functools.partial(body, *roles, do_y, do_o))


def _merge(h, att, hm, pre_g, wga, wgm, wa, wm, wo, post_g):
    n = h.shape[0]
    tm, tn = MERGE_TM, MERGE_TN
    n_tiles, n_steps, n_slices = n // tm, D_MODEL // tn, tm // SLICE_ROWS
    assert n_steps >= n_slices

    def slice_of(tile, j):
        return tile * n_slices + _slice_index(j, n_slices)

    lead = lambda r, j: (slice_of(jnp.minimum(r, n_tiles - 1), j), 0)
    lag = lambda r, j: (jnp.where(r < 3, 0, slice_of(r - 3, j)), 0)
    tile = lambda r, j: (jnp.clip(r - 1, 0, n_tiles - 1), 0)
    step_y = lambda r, j: jnp.where(r < 1, 0, jnp.where(r > n_tiles, n_steps - 1, j))
    step_o = lambda r, j: jnp.where(r < 2, 0, jnp.where(r > n_tiles + 1, n_steps - 1, j))
    return pl.pallas_call(
        _merge_kernel,
        name="merge",
        grid=(n_tiles + 3, n_steps),
        in_specs=[
            pl.BlockSpec((SLICE_ROWS, D_MODEL), lead),
            pl.BlockSpec((SLICE_ROWS, D_MODEL), lag),
            pl.BlockSpec((tm, ATT_Q), tile),
            pl.BlockSpec((tm, M_V), tile),
            pl.BlockSpec((1, D_MODEL), lambda r, j: (0, 0)),
            pl.BlockSpec((tn, D_MODEL), lambda r, j: (step_y(r, j), 0)),
            pl.BlockSpec((tn, D_MODEL), lambda r, j: (step_y(r, j), 0)),
            pl.BlockSpec((ATT_Q, tn), lambda r, j: (0, step_y(r, j))),
            pl.BlockSpec((M_V, tn), lambda r, j: (0, step_y(r, j))),
            pl.BlockSpec((D_MODEL, tn), lambda r, j: (0, step_o(r, j))),
            pl.BlockSpec((1, D_MODEL), lambda r, j: (0, 0)),
        ],
        out_specs=pl.BlockSpec((SLICE_ROWS, D_MODEL), lag),
        out_shape=jax.ShapeDtypeStruct((n, D_MODEL), F32),
        scratch_shapes=[pltpu.VMEM((tm, D_MODEL), BF16), pltpu.VMEM((tm, D_MODEL), BF16),
                        pltpu.VMEM((n_steps, tm, tn), BF16), pltpu.VMEM((n_steps, tm, tn), BF16),
                        pltpu.VMEM((n_steps, tm, tn), F32), pltpu.VMEM((n_steps, tm, tn), F32)],
        compiler_params=pltpu.CompilerParams(
            dimension_semantics=("arbitrary", "arbitrary"), vmem_limit_bytes=VMEM_LIMIT),
    )(h, h, att, hm, pre_g, wga, wgm, wa, wm, wo, post_g)


def kernel(x, ffn1_pre_g, ffn1_wg, ffn1_wu, ffn1_wd, ffn1_post_g, mix_pre_g, w_in, conv_w, conv_b,
           attn_sinks, m_igate_b, m_fgate_b, m_head_g, w_attn_up, w_mlstm_up, w_out, mix_post_g,
           ffn2_pre_g, ffn2_wg, ffn2_wu, ffn2_wd, ffn2_post_g):
    batch, seq, _ = x.shape
    h = x.reshape(batch * seq, D_MODEL)
    for l in range(ffn1_wg.shape[0]):
        bf = lambda w: w.astype(BF16)
        gate0 = W_IN_MAIN + GATE_ROWS
        w_in_t = bf(jnp.swapaxes(w_in[l], 0, 1))
        w_ga_t = w_in_t[gate0:gate0 + D_MODEL]
        w_gm_t = w_in_t[gate0 + D_MODEL:gate0 + 2 * D_MODEL]
        gbias = jnp.concatenate(
            [m_igate_b[l], m_fgate_b[l], jnp.zeros((GATE_PAD - 2 * M_HEADS,), F32)])[None, :]
        slopes = jnp.exp2(-8.0 * jnp.arange(1, ATT_HEADS + 1, dtype=F32) / ATT_HEADS)
        scal = jnp.stack([attn_sinks[l].astype(F32), slopes])

        h = _ffn(h, ffn1_pre_g[l][None, :], ffn1_wg[l], ffn1_wu[l], ffn1_wd[l],
                 ffn1_post_g[l][None, :])
        zb, zf, gt = _inproj(h, mix_pre_g[l][None, :], w_in_t)
        att, w_au, w_mu, w_o = _attention(zb, scal, batch, seq,
                                          side=(w_attn_up[l], w_mlstm_up[l], w_out[l]))
        hm = _mlstm(zb, zf, gt, conv_w[l], conv_b[l][None, :], gbias, m_head_g[l][None, :],
                    batch, seq)
        h = _merge(h, att, hm, mix_pre_g[l][None, :], w_ga_t, w_gm_t, w_au, w_mu, w_o,
                   mix_post_g[l][None, :])
        h = _ffn(h, ffn2_pre_g[l][None, :], ffn2_wg[l], ffn2_wu[l], ffn2_wd[l],
                 ffn2_post_g[l][None, :])
    return h.reshape(batch, seq, D_MODEL)
```
